```python
import jax, jax.numpy as jnp
from jax import lax
import numpy as np

D_MODEL = 4096
BATCH = 1
SEQ = 8192
DEPTH = 4

HEAD_DIM = 128
DIL_GROUPS = ((128, 1), (512, 4), (2048, 16))
A_HEADS_PER_GROUP = 4
A_GROUPS = len(DIL_GROUPS)
A_HEADS = A_GROUPS * A_HEADS_PER_GROUP
A_WIDTH = A_HEADS * HEAD_DIM
A_OUT = A_HEADS_PER_GROUP * HEAD_DIM
B_HEADS = 8
B_WIDTH = B_HEADS * HEAD_DIM
IDX_HEADS = 16
IDX_DIM = 64
DSA_TOPK = 256
MEM_LEN = 256
MEM_HEADS = 4
MEM_WIDTH = MEM_HEADS * HEAD_DIM
D_FF = 4 * D_MODEL
ROPE_THETA = 10000.0
LN_EPS = 1e-5
BLK = 128
NEG_INF = -1e30
DEEPNORM_ALPHA = (2 * DEPTH) ** 0.25
DEEPNORM_BETA = (8 * DEPTH) ** -0.25

_SIZES = (A_WIDTH, A_WIDTH, A_WIDTH, B_WIDTH, B_WIDTH, B_WIDTH,
          IDX_HEADS * IDX_DIM, IDX_DIM, IDX_HEADS, D_MODEL, D_MODEL)
C_IN = sum(_SIZES)
SPLIT_POINTS = tuple(int(v) for v in np.cumsum(_SIZES)[:-1])

kernel_name = "hybrid_dilated_dsa_gated_deepnorm"


def layer_norm(x, g, b):
    xf = x.astype(jnp.float32)
    mu = jnp.mean(xf, axis=-1, keepdims=True)
    var = jnp.mean(jnp.square(xf - mu), axis=-1, keepdims=True)
    return ((xf - mu) * lax.rsqrt(var + LN_EPS)).astype(x.dtype) * g + b


def rope(x, pos):
    half = x.shape[-1] // 2
    inv = ROPE_THETA ** (-jnp.arange(half, dtype=jnp.float32) / half)
    ang = pos.astype(jnp.float32)[:, None] * inv[None, :]
    cos = jnp.cos(ang)[None, :, None, :]
    sin = jnp.sin(ang)[None, :, None, :]
    xf = x.astype(jnp.float32)
    x1, x2 = xf[..., :half], xf[..., half:]
    return jnp.concatenate([x1 * cos - x2 * sin, x2 * cos + x1 * sin], axis=-1).astype(x.dtype)


def dilated_window_group(q, k, v, window, dilation):
    B, T, H, Dh = q.shape
    n_back = window // dilation
    blk = n_back
    sub_len = -(-T // dilation)
    nb = -(-sub_len // blk)
    t_pad = nb * blk * dilation
    pad = t_pad - T

    def to_sub(a):
        a = jnp.pad(a, ((0, 0), (0, pad), (0, 0), (0, 0)))
        a = a.reshape(B, nb * blk, dilation, H, Dh).transpose(0, 2, 1, 3, 4)
        return a.reshape(B, dilation, nb, blk, H, Dh)

    def band(a):
        prev = jnp.pad(a, ((0, 0), (0, 0), (1, 0), (0, 0), (0, 0), (0, 0)))[:, :, :-1]
        return jnp.concatenate([prev, a], axis=3)

    qs = to_sub(q)
    kb = band(to_sub(k))
    vb = band(to_sub(v))
    s = jnp.einsum('brnqhd,brnkhd->brnhqk', qs, kb).astype(jnp.float32) * (Dh ** -0.5)
    qi = jnp.arange(blk)[:, None]
    ki = jnp.arange(2 * blk)[None, :]
    dist = qi + blk - ki
    band_ok = (dist >= 0) & (dist <= n_back)
    blk_idx = jnp.arange(nb)[:, None, None]
    valid = band_ok[None] & ((blk_idx > 0) | (ki >= blk)[None])
    s = jnp.where(valid[None, None, :, None], s, NEG_INF)
    lse = jax.nn.logsumexp(s, axis=-1)
    p = jnp.exp(s - lse[..., None])
    o = jnp.einsum('brnhqk,brnkhd->brnqhd', p.astype(v.dtype), vb)
    o = o.reshape(B, dilation, nb * blk, H, Dh).transpose(0, 2, 1, 3, 4).reshape(B, t_pad, H, Dh)[:, :T]
    lse = lse.transpose(0, 2, 4, 1, 3).reshape(B, t_pad, H)[:, :T]
    return o, lse


def dsa_attention(q, k, v, q_idx, k_idx, w_idx):
    B, T, H, Dh = q.shape
    topk = min(DSA_TOPK, T // 4)
    nb = T // BLK
    f32 = jnp.float32

    def blocks(a):
        return a.reshape((B, nb, BLK) + a.shape[2:]).swapaxes(0, 1)

    kf = k_idx.astype(f32)
    key_pos = jnp.arange(T)

    def one_block(args):
        qblk, qiblk, wiblk, start = args
        tq = start + jnp.arange(BLK)
        rel = jax.nn.relu(jnp.einsum('bqhd,bsd->bqhs', qiblk.astype(f32), kf) * (IDX_DIM ** -0.5))
        score = jnp.einsum('bqh,bqhs->bqs', wiblk.astype(f32) * (IDX_HEADS ** -0.5), rel)
        causal = key_pos[None, :] <= tq[:, None]
        score = jnp.where(causal[None], score, NEG_INF)
        _, sel = lax.top_k(score, topk)
        ok = sel <= tq[None, :, None]
        kg = jax.vmap(lambda a, i: a[i])(k, sel)
        vg = jax.vmap(lambda a, i: a[i])(v, sel)
        s = jnp.einsum('bqhd,bqkhd->bhqk', qblk, kg).astype(f32) * (Dh ** -0.5)
        s = jnp.where(ok[:, None], s, NEG_INF)
        p = jax.nn.softmax(s, axis=-1)
        return jnp.einsum('bhqk,bqkhd->bqhd', p.astype(v.dtype), vg)

    starts = jnp.arange(nb, dtype=jnp.int32) * BLK
    out = lax.map(one_block, (blocks(q), blocks(q_idx), blocks(w_idx), starts))
    return out.swapaxes(0, 1).reshape(B, T, H, Dh)


def hybrid_mixer(x, w_in, w_pa, w_pb, w_o, pos):
    B, T, _ = x.shape
    h = x @ w_in
    qa, ka, va, qb, kb, vb, qi, ki, wi, ga, gb = jnp.split(h, SPLIT_POINTS, axis=-1)
    qa = rope(qa.reshape(B, T, A_HEADS, HEAD_DIM), pos)
    ka = rope(ka.reshape(B, T, A_HEADS, HEAD_DIM), pos)
    va = va.reshape(B, T, A_HEADS, HEAD_DIM)
    outs, lses = [], []
    for g, (win, dil) in enumerate(DIL_GROUPS):
        sl = slice(g * A_HEADS_PER_GROUP, (g + 1) * A_HEADS_PER_GROUP)
        o, l = dilated_window_group(qa[:, :, sl], ka[:, :, sl], va[:, :, sl], win, dil)
        outs.append(o)
        lses.append(l)
    wgt = jax.nn.softmax(jnp.stack(lses, axis=0), axis=0)
    ya = jnp.einsum('gbth,gbthd->bthd', wgt.astype(x.dtype), jnp.stack(outs, axis=0)).reshape(B, T, A_OUT)
    qb = rope(qb.reshape(B, T, B_HEADS, HEAD_DIM), pos)
    kb = rope(kb.reshape(B, T, B_HEADS, HEAD_DIM), pos)
    vb = vb.reshape(B, T, B_HEADS, HEAD_DIM)
    qi = rope(qi.reshape(B, T, IDX_HEADS, IDX_DIM), pos)
    ki = rope(ki.reshape(B, T, 1, IDX_DIM), pos)[:, :, 0]
    yb = dsa_attention(qb, kb, vb, qi, ki, wi).reshape(B, T, B_WIDTH)
    merged = jax.nn.sigmoid(ga) * (ya @ w_pa) + jax.nn.sigmoid(gb) * (yb @ w_pb)
    return merged @ w_o


def memory_cross_attention(x, mem, w_q, w_kv, w_o):
    B, T, _ = x.shape
    M = mem.shape[1]
    q = (x @ w_q).reshape(B, T, MEM_HEADS, HEAD_DIM)
    k, v = jnp.split(mem @ w_kv, 2, axis=-1)
    k = k.reshape(B, M, MEM_HEADS, HEAD_DIM)
    v = v.reshape(B, M, MEM_HEADS, HEAD_DIM)
    s = jnp.einsum('bthd,bmhd->bhtm', q, k).astype(jnp.float32) * (HEAD_DIM ** -0.5)
    p = jax.nn.softmax(s, axis=-1)
    o = jnp.einsum('bhtm,bmhd->bthd', p.astype(v.dtype), v).reshape(B, T, MEM_WIDTH)
    return o @ w_o


def sq_relu_mlp(x, w_up, w_down):
    return jnp.square(jax.nn.relu(x @ w_up)) @ w_down


def setup_inputs(seed: int = 0) -> dict:
    key = jax.random.key(seed)
    ks = jax.random.split(key, 19)
    f32 = jnp.float32

    def nrm(k, shape, scale):
        return jax.random.normal(k, shape, f32) * scale

    beta = DEEPNORM_BETA
    return {
        "x": nrm(ks[0], (BATCH, SEQ, D_MODEL), 1.0),
        "mem": nrm(ks[1], (BATCH, MEM_LEN, D_MODEL), 1.0),
        "w_in": nrm(ks[2], (DEPTH, D_MODEL, C_IN), D_MODEL ** -0.5),
        "w_pa": nrm(ks[3], (DEPTH, A_OUT, D_MODEL), A_OUT ** -0.5),
        "w_pb": nrm(ks[4], (DEPTH, B_WIDTH, D_MODEL), B_WIDTH ** -0.5),
        "w_o": nrm(ks[5], (DEPTH, D_MODEL, D_MODEL), beta * D_MODEL ** -0.5),
        "ln1_g": 1.0 + nrm(ks[6], (DEPTH, D_MODEL), 0.02),
        "ln1_b": nrm(ks[7], (DEPTH, D_MODEL), 0.02),
        "w_mq": nrm(ks[8], (DEPTH, D_MODEL, MEM_WIDTH), D_MODEL ** -0.5),
        "w_mkv": nrm(ks[9], (DEPTH, D_MODEL, 2 * MEM_WIDTH), D_MODEL ** -0.5),
        "w_mo": nrm(ks[10], (DEPTH, MEM_WIDTH, D_MODEL), beta * MEM_WIDTH ** -0.5),
        "ln2_g": 1.0 + nrm(ks[11], (DEPTH, D_MODEL), 0.02),
        "ln2_b": nrm(ks[12], (DEPTH, D_MODEL), 0.02),
        "w_up": nrm(ks[13], (DEPTH, D_MODEL, D_FF), D_MODEL ** -0.5),
        "w_down": nrm(ks[14], (DEPTH, D_FF, D_MODEL), beta * D_FF ** -0.5),
        "ln3_g": 1.0 + nrm(ks[15], (DEPTH, D_MODEL), 0.02),
        "ln3_b": nrm(ks[16], (DEPTH, D_MODEL), 0.02),
    }


def reference(x, mem, w_in, w_pa, w_pb, w_o, ln1_g, ln1_b, w_mq, w_mkv, w_mo,
              ln2_g, ln2_b, w_up, w_down, ln3_g, ln3_b):
    T = x.shape[1]
    pos = jnp.arange(T, dtype=jnp.int32)
    for l in range(DEPTH):
        y = hybrid_mixer(x, w_in[l], w_pa[l], w_pb[l], w_o[l], pos)
        x = layer_norm(DEEPNORM_ALPHA * x + y, ln1_g[l], ln1_b[l])
        y = memory_cross_attention(x, mem, w_mq[l], w_mkv[l], w_mo[l])
        x = layer_norm(DEEPNORM_ALPHA * x + y, ln2_g[l], ln2_b[l])
        y = sq_relu_mlp(x, w_up[l], w_down[l])
        x = layer_norm(DEEPNORM_ALPHA * x + y, ln3_g[l], ln3_b[l])
    return x
```

```python
import functools

import jax
import jax.numpy as jnp
from jax import lax
from jax.experimental import pallas as pl
from jax.experimental.pallas import tpu as pltpu

F32 = jnp.float32
BF16 = jnp.bfloat16

HEAD_DIM = 128
DIL_GROUPS = ((128, 1), (512, 4), (2048, 16))
A_HEADS_PER_GROUP = 4
A_GROUPS = len(DIL_GROUPS)
A_WIDTH = A_GROUPS * A_HEADS_PER_GROUP * HEAD_DIM
A_OUT = A_HEADS_PER_GROUP * HEAD_DIM
B_HEADS = 8
B_WIDTH = B_HEADS * HEAD_DIM
IDX_HEADS = 16
IDX_DIM = 64
IDX_WIDTH = IDX_HEADS * IDX_DIM
DSA_TOPK = 256
MEM_HEADS = 4
MEM_WIDTH = MEM_HEADS * HEAD_DIM
ROPE_THETA = 10000.0
LN_EPS = 1e-5
NEG_INF = -1e30
BAND = 128

LANES = 128
VMEM_LIMIT = 56 * 1024 * 1024

INT_MIN = -(2 ** 31)
INT_MAX = 2 ** 31 - 1


def _pick(n, prefs):
    for p in prefs:
        if n % p == 0:
            return p
    return n


def _params(sem):
    return pltpu.CompilerParams(dimension_semantics=sem, vmem_limit_bytes=VMEM_LIMIT)


def _rope128(x, cos, sin):
    return x * cos + pltpu.roll(x, 64, 1) * sin


def _rope64(x, cos, sin_lo, sin_hi):
    return x * cos + pltpu.roll(x, 96, 1) * sin_lo + pltpu.roll(x, 32, 1) * sin_hi


def _epilogue(kind, acc, extras, o_ref):
    tn = acc.shape[1]
    if kind == "none":
        o_ref[...] = acc.astype(o_ref.dtype)
    elif kind == "rope128":
        cos, sin = extras[0][...], extras[1][...]
        for c in range(tn // LANES):
            sl = slice(c * LANES, (c + 1) * LANES)
            o_ref[:, sl] = _rope128(acc[:, sl], cos, sin).astype(o_ref.dtype)
    elif kind == "idx":
        cos, s_lo, s_hi = extras[0][...], extras[1][...], extras[2][...]
        nch = tn // LANES
        for c in range(nch - 1):
            sl = slice(c * LANES, (c + 1) * LANES)
            o_ref[:, sl] = _rope64(acc[:, sl], cos, s_lo, s_hi).astype(o_ref.dtype)
        sl = slice((nch - 1) * LANES, nch * LANES)
        o_ref[:, sl] = acc[:, sl].astype(o_ref.dtype)
    elif kind == "sigmoid":
        o_ref[...] = (1.0 / (1.0 + jnp.exp(-acc))).astype(o_ref.dtype)
    elif kind == "relu2":
        r = jnp.maximum(acc, 0.0)
        o_ref[...] = (r * r).astype(o_ref.dtype)
    elif kind == "gate":
        o_ref[...] = (extras[0][...] * acc).astype(o_ref.dtype)
    elif kind == "gate_add":
        o_ref[...] = (extras[0][...] + extras[1][...] * acc).astype(o_ref.dtype)
    else:
        raise ValueError(kind)


def _mm_kernel(*refs, nk, kind, n_extra):
    a_ref, b_ref = refs[0], refs[1]
    extras = refs[2:2 + n_extra]
    o_ref = refs[2 + n_extra]
    part = jnp.dot(a_ref[...], b_ref[...], preferred_element_type=F32)
    if nk == 1:
        _epilogue(kind, part, extras, o_ref)
        return
    acc_ref = refs[3 + n_extra]
    k = pl.program_id(2)

    @pl.when(k == 0)
    def _():
        acc_ref[...] = part

    @pl.when(k > 0)
    def _():
        acc_ref[...] += part

    @pl.when(k == nk - 1)
    def _():
        _epilogue(kind, acc_ref[...], extras, o_ref)


def _matmul(a, b, layer, *, kind="none", extras=(), out_dtype=BF16, name="mm"):
    m, k = a.shape
    n = b.shape[2]
    tm = _pick(m, (1024, 512, 256))
    tn = n if n <= 1280 else _pick(n, (1024, 512, 256, 128))
    tk = k if k <= 4096 else _pick(k, (4096, 2048, 1024))
    nk = k // tk
    assert m % tm == 0 and n % tn == 0 and k % tk == 0
    in_specs = [
        pl.BlockSpec((tm, tk), lambda i, j, kk: (i, kk)),
        pl.BlockSpec((None, tk, tn), lambda i, j, kk: (layer, kk, j)),
    ]
    ops = [a, b]
    for arr, bshape, imap in extras:
        in_specs.append(pl.BlockSpec(bshape, functools.partial(lambda i, j, kk, f: f(i, j), f=imap)))
        ops.append(arr)
    scratch = [pltpu.VMEM((tm, tn), F32)] if nk > 1 else []
    return pl.pallas_call(
        functools.partial(_mm_kernel, nk=nk, kind=kind, n_extra=len(extras)),
        grid=(m // tm, n // tn, nk),
        in_specs=in_specs,
        out_specs=pl.BlockSpec((tm, tn), lambda i, j, kk: (i, j)),
        out_shape=jax.ShapeDtypeStruct((m, n), out_dtype),
        scratch_shapes=scratch,
        compiler_params=_params(("parallel", "parallel", "arbitrary")),
        name=name,
    )(*ops)


def _ln_kernel(x_ref, y_ref, g_ref, b_ref, o_ref, ob_ref, *, alpha):
    z = alpha * x_ref[...] + y_ref[...]
    mu = jnp.mean(z, axis=-1, keepdims=True)
    zc = z - mu
    var = jnp.mean(zc * zc, axis=-1, keepdims=True)
    out = (zc * lax.rsqrt(var + LN_EPS)) * g_ref[...] + b_ref[...]
    o_ref[...] = out
    ob_ref[...] = out.astype(BF16)


def _deepnorm_ln(x, y, g, b, layer, alpha):
    t, d = x.shape
    tm = _pick(t, (128,))
    row = pl.BlockSpec((tm, d), lambda i: (i, 0))
    par = pl.BlockSpec((None, 1, d), lambda i: (layer, 0, 0))
    return pl.pallas_call(
        functools.partial(_ln_kernel, alpha=alpha),
        grid=(t // tm,),
        in_specs=[row, row, par, par],
        out_specs=[row, row],
        out_shape=[jax.ShapeDtypeStruct((t, d), F32), jax.ShapeDtypeStruct((t, d), BF16)],
        compiler_params=_params(("parallel",)),
        name="deepnorm_ln",
    )(x, y, g, b)


def _dil_kernel(q_ref, kp_ref, kc_ref, vp_ref, vc_ref, o_ref, lse_ref):
    i = pl.program_id(1)
    blk = q_ref.shape[0]
    qi = lax.broadcasted_iota(jnp.int32, (blk, blk), 0)
    ki = lax.broadcasted_iota(jnp.int32, (blk, blk), 1)
    ok_prev = ki >= qi + jnp.where(i > 0, 0, blk)
    ok_cur = ki <= qi
    scale = HEAD_DIM ** -0.5
    nt = (((1,), (1,)), ((), ()))
    for h in range(A_HEADS_PER_GROUP):
        sl = slice(h * HEAD_DIM, (h + 1) * HEAD_DIM)
        q = q_ref[:, sl]
        sp = lax.dot_general(q, kp_ref[:, sl], nt, preferred_element_type=F32) * scale
        sc = lax.dot_general(q, kc_ref[:, sl], nt, preferred_element_type=F32) * scale
        sp = jnp.where(ok_prev, sp, NEG_INF)
        sc = jnp.where(ok_cur, sc, NEG_INF)
        m = jnp.maximum(jnp.max(sp, axis=-1, keepdims=True), jnp.max(sc, axis=-1, keepdims=True))
        ep = jnp.exp(sp - m)
        ec = jnp.exp(sc - m)
        l = jnp.sum(ep, axis=-1, keepdims=True) + jnp.sum(ec, axis=-1, keepdims=True)
        lse = m + jnp.log(l)
        inv = 1.0 / l
        pp = (ep * inv).astype(BF16)
        pc = (ec * inv).astype(BF16)
        o = (jnp.dot(pp, vp_ref[:, sl], preferred_element_type=F32)
             + jnp.dot(pc, vc_ref[:, sl], preferred_element_type=F32))
        o_ref[:, sl] = o
        lse_ref[:, sl] = jnp.broadcast_to(lse, (blk, HEAD_DIM))


def _dilated_group(qk, v, g, dil, qk_width, v_width):
    t = qk.shape[0]
    sub = t // dil
    assert t % dil == 0 and sub % BAND == 0
    nb = sub // BAND
    qk_v = qk.reshape(sub, dil * qk_width)
    v_v = v.reshape(sub, dil * v_width)
    qpb = qk_width // A_OUT
    vpb = v_width // A_OUT
    kcol = A_WIDTH // A_OUT
    vcol = B_WIDTH // A_OUT
    blk = (BAND, A_OUT)
    prev = lambda i: jnp.maximum(i - 1, 0)
    in_specs = [
        pl.BlockSpec(blk, lambda r, i: (i, r * qpb + g)),
        pl.BlockSpec(blk, lambda r, i: (prev(i), r * qpb + kcol + g)),
        pl.BlockSpec(blk, lambda r, i: (i, r * qpb + kcol + g)),
        pl.BlockSpec(blk, lambda r, i: (prev(i), r * vpb + vcol + g)),
        pl.BlockSpec(blk, lambda r, i: (i, r * vpb + vcol + g)),
    ]
    out_spec = pl.BlockSpec(blk, lambda r, i: (i, r))
    o, lse = pl.pallas_call(
        _dil_kernel,
        grid=(dil, nb),
        in_specs=in_specs,
        out_specs=[out_spec, out_spec],
        out_shape=[jax.ShapeDtypeStruct((sub, dil * A_OUT), F32)] * 2,
        compiler_params=_params(("parallel", "parallel")),
        name=f"dilated_attn_d{dil}",
    )(qk_v, qk_v, qk_v, v_v, v_v)
    return o.reshape(t, A_OUT), lse.reshape(t, A_OUT)


def _mix_kernel(o0, o1, o2, l0, l1, l2, y_ref):
    a, b, c = l0[...], l1[...], l2[...]
    m = jnp.maximum(jnp.maximum(a, b), c)
    ea, eb, ec = jnp.exp(a - m), jnp.exp(b - m), jnp.exp(c - m)
    inv = 1.0 / (ea + eb + ec)
    y = (ea * inv) * o0[...] + (eb * inv) * o1[...] + (ec * inv) * o2[...]
    y_ref[...] = y.astype(y_ref.dtype)


def _mix_groups(outs, lses):
    t = outs[0].shape[0]
    tm = _pick(t, (1024, 512, 256))
    spec = pl.BlockSpec((tm, A_OUT), lambda i: (i, 0))
    return pl.pallas_call(
        _mix_kernel,
        grid=(t // tm,),
        in_specs=[spec] * 6,
        out_specs=spec,
        out_shape=jax.ShapeDtypeStruct((t, A_OUT), BF16),
        compiler_params=_params(("parallel",)),
        name="mix_groups",
    )(*outs, *lses)


def _sortable(x):
    b = pltpu.bitcast(x, jnp.int32)
    return b ^ ((b >> 31) & jnp.int32(INT_MAX))


def _dsa_kernel(qi_ref, ki_ref, wi_ref, q_ref, k_ref, v_ref, o_ref,
                key_ref, thr_ref, mthr_ref, m_ref, l_ref, acc_ref, *, tq, tkc, topk):
    i = pl.program_id(0)
    c = pl.program_id(1)
    last_c = ((i + 1) * tq - 1) // tkc
    n_ch = last_c + 1
    nt = (((1,), (1,)), ((), ()))
    row_pos = i * tq + lax.broadcasted_iota(jnp.int32, (tq, tkc), 0)
    col_iota = lax.broadcasted_iota(jnp.int32, (tq, tkc), 1)

    @pl.when(c == 0)
    def _select():
        w = wi_ref[...] * (IDX_DIM ** -0.5 * IDX_HEADS ** -0.5)
        lane = lax.broadcasted_iota(jnp.int32, (tq, LANES), 1)
        lo_half = lane < IDX_DIM

        def score_chunk(kc, carry):
            off = pl.multiple_of(kc * tkc, tkc)
            kic = ki_ref[pl.ds(off, tkc), :].astype(BF16)
            sc = jnp.zeros((tq, tkc), F32)
            for hp in range(IDX_HEADS // 2):
                pair = qi_ref[:, hp * LANES:(hp + 1) * LANES]
                for half in range(2):
                    h = 2 * hp + half
                    qh = jnp.where(lo_half if half == 0 else ~lo_half, pair, 0.0).astype(BF16)
                    r = lax.dot_general(qh, kic, nt, preferred_element_type=F32)
                    sc = sc + w[:, h:h + 1] * jnp.maximum(r, 0.0)
            sc = jnp.where(off + col_iota <= row_pos, sc, NEG_INF)
            key_ref[kc] = _sortable(sc)
            return carry

        lax.fori_loop(0, n_ch, score_chunk, 0)

        def count(pred):
            def body(kc, acc):
                off = pl.multiple_of(kc * tkc, tkc)
                hit = pred(key_ref[kc], off)
                part = jnp.where(hit, 1, 0).astype(jnp.int32)
                for s in range(tkc // LANES):
                    acc = acc + part[:, s * LANES:(s + 1) * LANES]
                return acc
            acc = lax.fori_loop(0, n_ch, body, jnp.zeros((tq, LANES), jnp.int32))
            return jnp.sum(acc, axis=1, keepdims=True)

        def bisect(_, lohi):
            lo, hi = lohi
            mid = (lo | hi) - ((lo ^ hi) >> 1)
            ok = count(lambda kk, off: kk >= mid) >= topk
            return jnp.where(ok, mid, lo), jnp.where(ok, hi, mid - 1)

        lo0 = jnp.full((tq, 1), INT_MIN, jnp.int32)
        hi0 = jnp.full((tq, 1), INT_MAX, jnp.int32)
        thr, _ = lax.fori_loop(0, 32, bisect, (lo0, hi0))
        thr_ref[...] = thr

        n_gt = count(lambda kk, off: kk > thr)
        n_ge = count(lambda kk, off: kk >= thr)
        need = topk - n_gt
        mthr_ref[...] = jnp.full((tq, 1), INT_MAX, jnp.int32)

        @pl.when(jnp.max(n_ge) > topk)
        def _ties():
            def bisect_idx(_, lohi):
                lo, hi = lohi
                mid = (lo + hi) >> 1
                ok = count(lambda kk, off: (kk == thr) & (off + col_iota <= mid)) >= need
                return jnp.where(ok, lo, mid + 1), jnp.where(ok, mid, hi)

            lo1 = jnp.zeros((tq, 1), jnp.int32)
            hi1 = jnp.full((tq, 1), 2 ** 30, jnp.int32)
            _, m_idx = lax.fori_loop(0, 31, bisect_idx, (lo1, hi1))
            mthr_ref[...] = m_idx

        m_ref[...] = jnp.full(m_ref.shape, NEG_INF, F32)
        l_ref[...] = jnp.zeros(l_ref.shape, F32)
        acc_ref[...] = jnp.zeros(acc_ref.shape, F32)

    @pl.when(c <= last_c)
    def _attend():
        off = pl.multiple_of(c * tkc, tkc)
        kk = key_ref[c]
        thr = thr_ref[...]
        pos = off + col_iota
        chosen = (kk > thr) | ((kk == thr) & (pos <= mthr_ref[...]))
        valid = chosen & (pos <= row_pos)
        scale = HEAD_DIM ** -0.5
        for h in range(B_HEADS):
            sl = slice(h * HEAD_DIM, (h + 1) * HEAD_DIM)
            s = lax.dot_general(q_ref[:, sl], k_ref[:, sl], nt, preferred_element_type=F32) * scale
            s = jnp.where(valid, s, NEG_INF)
            m_old = m_ref[h]
            m_new = jnp.maximum(m_old, jnp.max(s, axis=-1, keepdims=True))
            p = jnp.where(valid, jnp.exp(s - m_new), 0.0)
            alpha = jnp.exp(m_old - m_new)
            l_ref[h] = alpha * l_ref[h] + jnp.sum(p, axis=-1, keepdims=True)
            acc_ref[h] = alpha * acc_ref[h] + jnp.dot(p.astype(BF16), v_ref[:, sl],
                                                      preferred_element_type=F32)
            m_ref[h] = m_new

    @pl.when(c == last_c)
    def _finish():
        for h in range(B_HEADS):
            o_ref[:, h * HEAD_DIM:(h + 1) * HEAD_DIM] = (acc_ref[h] / l_ref[h]).astype(o_ref.dtype)


def _dsa(idx, qk, v, qk_width, v_width):
    t = idx.shape[0]
    tq = 128
    tkc = _pick(t, (1024, 512, 256))
    topk = min(DSA_TOPK, t // 4)
    assert tkc >= topk and t % tq == 0
    nq, nc = t // tq, t // tkc
    qcol = 2 * A_WIDTH // B_WIDTH
    assert (2 * A_WIDTH) % B_WIDTH == 0

    def kchunk(i, c):
        return jnp.minimum(c, ((i + 1) * tq - 1) // tkc)

    in_specs = [
        pl.BlockSpec((tq, IDX_WIDTH), lambda i, c: (i, 0)),
        pl.BlockSpec((t, LANES), lambda i, c: (0, IDX_WIDTH // LANES)),
        pl.BlockSpec((tq, LANES), lambda i, c: (i, IDX_WIDTH // LANES + 1)),
        pl.BlockSpec((tq, B_WIDTH), lambda i, c: (i, qcol)),
        pl.BlockSpec((tkc, B_WIDTH), lambda i, c: (kchunk(i, c), qcol + 1)),
        pl.BlockSpec((tkc, B_WIDTH), lambda i, c: (kchunk(i, c), 0)),
    ]
    return pl.pallas_call(
        functools.partial(_dsa_kernel, tq=tq, tkc=tkc, topk=topk),
        grid=(nq, nc),
        in_specs=in_specs,
        out_specs=pl.BlockSpec((tq, B_WIDTH), lambda i, c: (i, 0)),
        out_shape=jax.ShapeDtypeStruct((t, B_WIDTH), BF16),
        scratch_shapes=[
            pltpu.VMEM((nc, tq, tkc), jnp.int32),
            pltpu.VMEM((tq, 1), jnp.int32),
            pltpu.VMEM((tq, 1), jnp.int32),
            pltpu.VMEM((B_HEADS, tq, 1), F32),
            pltpu.VMEM((B_HEADS, tq, 1), F32),
            pltpu.VMEM((B_HEADS, tq, HEAD_DIM), F32),
        ],
        compiler_params=_params(("parallel", "arbitrary")),
        name="dsa",
    )(idx, idx, idx, qk, qk, v)


def _mem_kernel(q_ref, kv_ref, o_ref):
    nt = (((1,), (1,)), ((), ()))
    scale = HEAD_DIM ** -0.5
    for h in range(MEM_HEADS):
        sl = slice(h * HEAD_DIM, (h + 1) * HEAD_DIM)
        vsl = slice(MEM_WIDTH + h * HEAD_DIM, MEM_WIDTH + (h + 1) * HEAD_DIM)
        s = lax.dot_general(q_ref[:, sl], kv_ref[:, sl], nt, preferred_element_type=F32) * scale
        m = jnp.max(s, axis=-1, keepdims=True)
        e = jnp.exp(s - m)
        p = e / jnp.sum(e, axis=-1, keepdims=True)
        o_ref[:, sl] = jnp.dot(p.astype(BF16), kv_ref[:, vsl], preferred_element_type=F32).astype(o_ref.dtype)


def _mem_attention(q, kv):
    t = q.shape[0]
    mlen = kv.shape[0]
    tq = _pick(t, (512, 256, 128))
    return pl.pallas_call(
        _mem_kernel,
        grid=(t // tq,),
        in_specs=[pl.BlockSpec((tq, MEM_WIDTH), lambda i: (i, 0)),
                  pl.BlockSpec((mlen, 2 * MEM_WIDTH), lambda i: (0, 0))],
        out_specs=pl.BlockSpec((tq, MEM_WIDTH), lambda i: (i, 0)),
        out_shape=jax.ShapeDtypeStruct((t, MEM_WIDTH), BF16),
        compiler_params=_params(("parallel",)),
        name="mem_attention",
    )(q, kv)


def _cast_kernel(x_ref, o_ref):
    o_ref[...] = x_ref[...].astype(o_ref.dtype)


def _to_bf16(x):
    t, d = x.shape
    tm = _pick(t, (256, 128))
    spec = pl.BlockSpec((tm, d), lambda i: (i, 0))
    return pl.pallas_call(
        _cast_kernel, grid=(t // tm,), in_specs=[spec], out_specs=spec,
        out_shape=jax.ShapeDtypeStruct((t, d), BF16),
        compiler_params=_params(("parallel",)), name="cast_bf16",
    )(x)


def _rope_tables(t):
    pos = jnp.arange(t, dtype=jnp.int32).astype(F32)[:, None]
    half = HEAD_DIM // 2
    inv = ROPE_THETA ** (-jnp.arange(half, dtype=F32) / half)
    ang = pos * inv[None, :]
    cos128 = jnp.concatenate([jnp.cos(ang)] * 2, axis=1)
    sin128 = jnp.concatenate([-jnp.sin(ang), jnp.sin(ang)], axis=1)
    half = IDX_DIM // 2
    inv = ROPE_THETA ** (-jnp.arange(half, dtype=F32) / half)
    ang = pos * inv[None, :]
    z = jnp.zeros_like(ang)
    cos64 = jnp.concatenate([jnp.cos(ang)] * 4, axis=1)
    sin_lo = jnp.concatenate([-jnp.sin(ang), z] * 2, axis=1)
    sin_hi = jnp.concatenate([z, jnp.sin(ang)] * 2, axis=1)
    return (cos128, sin128), (cos64, sin_lo, sin_hi)


def kernel(x, mem, w_in, w_pa, w_pb, w_o, ln1_g, ln1_b, w_mq, w_mkv, w_mo, ln2_g, ln2_b, w_up, w_down, ln3_g, ln3_b):
    bsz, t, d = x.shape
    depth = w_in.shape[0]
    assert bsz == 1
    alpha = (2 * depth) ** 0.25

    sizes = (A_WIDTH, A_WIDTH, A_WIDTH, B_WIDTH, B_WIDTH, B_WIDTH, IDX_WIDTH, IDX_DIM, IDX_HEADS, d, d)
    offs = [0]
    for s in sizes:
        offs.append(offs[-1] + s)
    assert offs[-1] == w_in.shape[2]
    col = lambda a, b: w_in[:, :, offs[a]:offs[b]]
    w_rope = jnp.concatenate([col(0, 2), col(3, 5)], axis=2).astype(BF16)
    w_v = jnp.concatenate([col(5, 6), col(2, 3)], axis=2).astype(BF16)
    idx_pad = LANES - IDX_HEADS
    w_idx = jnp.concatenate([col(6, 8), col(7, 9), jnp.zeros((depth, d, idx_pad), F32)], axis=2).astype(BF16)
    w_gate = col(9, 11).astype(BF16)
    w_pa_b, w_pb_b, w_o_b = w_pa.astype(BF16), w_pb.astype(BF16), w_o.astype(BF16)
    w_mq_b, w_mkv_b, w_mo_b = w_mq.astype(BF16), w_mkv.astype(BF16), w_mo.astype(BF16)
    w_up_b, w_down_b = w_up.astype(BF16), w_down.astype(BF16)
    lnp = [p.reshape(depth, 1, d) for p in (ln1_g, ln1_b, ln2_g, ln2_b, ln3_g, ln3_b)]

    (cos128, sin128), (cos64, sin_lo, sin_hi) = _rope_tables(t)
    qk_width = w_rope.shape[2]
    v_width = w_v.shape[2]

    xf = x[0]
    xb = _to_bf16(xf)
    memb = _to_bf16(mem[0])

    def row_tab(arr):
        return (arr, (_pick(t, (1024, 512, 256)), LANES), lambda i, j: (i, 0))

    def tile(arr, tn, joff=0):
        return (arr, (_pick(t, (1024, 512, 256)), tn), lambda i, j: (i, j + joff))

    for l in range(depth):
        qk = _matmul(xb, w_rope, l, kind="rope128", extras=(row_tab(cos128), row_tab(sin128)), name="proj_qk")
        v = _matmul(xb, w_v, l, name="proj_v")
        idx = _matmul(xb, w_idx, l, kind="idx", out_dtype=F32,
                      extras=(row_tab(cos64), row_tab(sin_lo), row_tab(sin_hi)), name="proj_idx")
        gates = _matmul(xb, w_gate, l, kind="sigmoid", out_dtype=F32, name="proj_gate")
        outs, lses = [], []
        for g, (win, dil) in enumerate(DIL_GROUPS):
            assert win // dil == BAND
            o, lse = _dilated_group(qk, v, g, dil, qk_width, v_width)
            outs.append(o)
            lses.append(lse)
        ya = _mix_groups(outs, lses)
        yb = _dsa(idx, qk, v, qk_width, v_width)
        tn_d = d if d <= 1280 else _pick(d, (1024, 512, 256, 128))
        pa = _matmul(ya, w_pa_b, l, kind="gate", out_dtype=F32, extras=(tile(gates, tn_d),), name="proj_a")
        merged = _matmul(yb, w_pb_b, l, kind="gate_add",
                         extras=(tile(pa, tn_d), tile(gates, tn_d, d // tn_d)), name="proj_b_merge")
        y = _matmul(merged, w_o_b, l, out_dtype=F32, name="mixer_out")
        xf, xb = _deepnorm_ln(xf, y, lnp[0], lnp[1], l, alpha)
        qm = _matmul(xb, w_mq_b, l, name="mem_q")
        kvm = _matmul(memb, w_mkv_b, l, name="mem_kv")
        om = _mem_attention(qm, kvm)
        y = _matmul(om, w_mo_b, l, out_dtype=F32, name="mem_out")
        xf, xb = _deepnorm_ln(xf, y, lnp[2], lnp[3], l, alpha)
        hdn = _matmul(xb, w_up_b, l, kind="relu2", name="mlp_up")
        y = _matmul(hdn, w_down_b, l, out_dtype=F32, name="mlp_down")
        xf, xb = _deepnorm_ln(xf, y, lnp[4], lnp[5], l, alpha)
    return xf[None]
```

```python
import functools

import jax
import jax.numpy as jnp
from jax import lax
from jax.experimental import pallas as pl
from jax.experimental.pallas import tpu as pltpu

F32 = jnp.float32
BF16 = jnp.bfloat16

HEAD_DIM = 128
DIL_GROUPS = ((128, 1), (512, 4), (2048, 16))
A_HEADS_PER_GROUP = 4
A_GROUPS = len(DIL_GROUPS)
A_WIDTH = A_GROUPS * A_HEADS_PER_GROUP * HEAD_DIM
A_OUT = A_HEADS_PER_GROUP * HEAD_DIM
B_HEADS = 8
B_WIDTH = B_HEADS * HEAD_DIM
IDX_HEADS = 16
IDX_DIM = 64
IDX_WIDTH = IDX_HEADS * IDX_DIM
DSA_TOPK = 256
MEM_HEADS = 4
MEM_WIDTH = MEM_HEADS * HEAD_DIM
ROPE_THETA = 10000.0
LN_EPS = 1e-5
NEG_INF = -1e30
M_FLOOR = -1e29
BAND = 128
LOG2E = 1.4426950408889634

LANES = 128
MXU_COLS = 256
VMEM_LIMIT = 56 * 1024 * 1024
B_TILE_BYTES = 8 * 1024 * 1024
COL_BLOCK = 512

INT_MIN = -(2 ** 31)
INT_MAX = 2 ** 31 - 1
NT_DIMS = (((1,), (1,)), ((), ()))


def _pick(n, prefs):
    for p in prefs:
        if n % p == 0:
            return p
    return n


def _params(sem):
    return pltpu.CompilerParams(dimension_semantics=sem, vmem_limit_bytes=VMEM_LIMIT)


def _rope128(x, cos, sin):
    return x * cos + pltpu.roll(x, 64, 1) * sin


def _rope64(x, cos, sin_lo, sin_hi):
    return x * cos + pltpu.roll(x, 96, 1) * sin_lo + pltpu.roll(x, 32, 1) * sin_hi


def _rope128_tile(acc, cos, sin):
    return [_rope128(acc[:, c * LANES:(c + 1) * LANES], cos, sin) for c in range(acc.shape[1] // LANES)]


def _epilogue(kind, acc, extras, o_ref, stage_ref, dil, j):
    tn = acc.shape[1]
    if kind == "none":
        o_ref[...] = acc.astype(o_ref.dtype)
    elif kind == "rope128":
        for c, y in enumerate(_rope128_tile(acc, extras[0][...], extras[1][...])):
            o_ref[:, c * LANES:(c + 1) * LANES] = y.astype(o_ref.dtype)
    elif kind == "rope128_qk":
        qs = jnp.where(j < B_WIDTH // COL_BLOCK, HEAD_DIM ** -0.5 * LOG2E, 1.0)
        for c, y in enumerate(_rope128_tile(acc, extras[0][...], extras[1][...])):
            o_ref[:, c * LANES:(c + 1) * LANES] = (y * qs).astype(o_ref.dtype)
    elif kind == "rope64":
        cos, s_lo, s_hi = extras[0][...], extras[1][...], extras[2][...]
        for c in range(tn // LANES):
            sl = slice(c * LANES, (c + 1) * LANES)
            o_ref[:, sl] = _rope64(acc[:, sl], cos, s_lo, s_hi).astype(o_ref.dtype)
    elif kind == "kiwi":
        cos, s_lo, s_hi = extras[0][...], extras[1][...], extras[2][...]
        o_ref[:, :LANES] = _rope64(acc[:, :LANES], cos, s_lo, s_hi).astype(o_ref.dtype)
        o_ref[:, LANES:] = acc[:, LANES:].astype(o_ref.dtype)
    elif kind == "a_group":
        @pl.when(j < 2)
        def _():
            for c, y in enumerate(_rope128_tile(acc, extras[0][...], extras[1][...])):
                stage_ref[c] = y

        @pl.when(j == 2)
        def _():
            for c in range(tn // LANES):
                stage_ref[c] = acc[:, c * LANES:(c + 1) * LANES]

        rows = acc.shape[0] // dil
        for r in range(dil):
            for c in range(tn // LANES):
                src = stage_ref[c] if dil == 1 else stage_ref[c, pl.ds(r, rows, stride=dil), :]
                o_ref[r, :, c * LANES:(c + 1) * LANES] = src.astype(o_ref.dtype)
    elif kind == "sigmoid":
        o_ref[...] = (1.0 / (1.0 + jnp.exp(-acc))).astype(o_ref.dtype)
    elif kind == "relu2":
        r = jnp.maximum(acc, 0.0)
        o_ref[...] = (r * r).astype(o_ref.dtype)
    elif kind == "gate":
        o_ref[...] = (extras[0][...] * acc).astype(o_ref.dtype)
    elif kind == "gate_add":
        o_ref[...] = (extras[0][...] + extras[1][...] * acc).astype(o_ref.dtype)
    else:
        raise ValueError(kind)


def _mm_kernel(*refs, nk, kind, n_extra, dil):
    a_ref, b_ref = refs[0], refs[1]
    extras = refs[2:2 + n_extra]
    o_ref = refs[2 + n_extra]
    scratch = list(refs[3 + n_extra:])
    acc_ref = scratch.pop(0) if nk > 1 else None
    stage_ref = scratch.pop(0) if kind == "a_group" else None
    j = pl.program_id(1)
    k = pl.program_id(2)
    part = jnp.dot(a_ref[...], b_ref[...].astype(BF16), preferred_element_type=F32)
    if nk == 1:
        _epilogue(kind, part, extras, o_ref, stage_ref, dil, j)
        return

    @pl.when(k == 0)
    def _():
        acc_ref[...] = part

    @pl.when(k > 0)
    def _():
        acc_ref[...] += part

    @pl.when(k == nk - 1)
    def _():
        _epilogue(kind, acc_ref[...], extras, o_ref, stage_ref, dil, j)


def _matmul(a, b, layer, *, kind="none", extras=(), out_dtype=BF16, name="mm",
            tn=None, n_tiles=None, src_block=None, dil=1):
    m, k = a.shape
    nb = b.shape[2]
    tm = _pick(m, (1024, 512, 256))
    if tn is None:
        tn = nb if nb <= 1280 else _pick(nb, (1024, 512, 256, 128))
    if n_tiles is None:
        n_tiles = nb // tn
    if src_block is None:
        src_block = lambda j: j
    itemsize = jnp.dtype(b.dtype).itemsize
    tk = k
    for cand in (k, 4096, 2048, 1024, 512):
        if cand <= k and k % cand == 0:
            tk = cand
            if cand * tn * itemsize <= B_TILE_BYTES:
                break
    nk = k // tk
    assert m % tm == 0 and k % tk == 0 and tm % dil == 0
    in_specs = [
        pl.BlockSpec((tm, tk), lambda i, j, kk: (i, kk)),
        pl.BlockSpec((None, tk, tn), lambda i, j, kk: (layer, kk, src_block(j))),
    ]
    ops = [a, b]
    for arr, bshape, imap in extras:
        in_specs.append(pl.BlockSpec(bshape, functools.partial(lambda i, j, kk, f: f(i, j), f=imap)))
        ops.append(arr)
    scratch = [pltpu.VMEM((tm, tn), F32)] if nk > 1 else []
    if kind == "a_group":
        scratch.append(pltpu.VMEM((tn // LANES, tm, LANES), F32))
        out_shape = jax.ShapeDtypeStruct((n_tiles, dil, m // dil, tn), out_dtype)
        out_spec = pl.BlockSpec((None, dil, tm // dil, tn), lambda i, j, kk: (j, 0, i, 0))
    else:
        out_shape = jax.ShapeDtypeStruct((m, n_tiles * tn), out_dtype)
        out_spec = pl.BlockSpec((tm, tn), lambda i, j, kk: (i, j))
    return pl.pallas_call(
        functools.partial(_mm_kernel, nk=nk, kind=kind, n_extra=len(extras), dil=dil),
        grid=(m // tm, n_tiles, nk),
        in_specs=in_specs,
        out_specs=out_spec,
        out_shape=out_shape,
        scratch_shapes=scratch,
        compiler_params=_params(("parallel", "parallel", "arbitrary")),
        name=name,
    )(*ops)


def _ln_kernel(x_ref, y_ref, g_ref, b_ref, o_ref, ob_ref, *, alpha):
    z = alpha * x_ref[...] + y_ref[...]
    mu = jnp.mean(z, axis=-1, keepdims=True)
    zc = z - mu
    var = jnp.mean(zc * zc, axis=-1, keepdims=True)
    out = (zc * lax.rsqrt(var + LN_EPS)) * g_ref[...] + b_ref[...]
    o_ref[...] = out
    ob_ref[...] = out.astype(BF16)


def _deepnorm_ln(x, y, g, b, layer, alpha):
    t, d = x.shape
    tm = _pick(t, (128,))
    row = pl.BlockSpec((tm, d), lambda i: (i, 0))
    par = pl.BlockSpec((None, 1, d), lambda i: (layer, 0, 0))
    return pl.pallas_call(
        functools.partial(_ln_kernel, alpha=alpha),
        grid=(t // tm,),
        in_specs=[row, row, par, par],
        out_specs=[row, row],
        out_shape=[jax.ShapeDtypeStruct((t, d), F32), jax.ShapeDtypeStruct((t, d), BF16)],
        compiler_params=_params(("parallel",)),
        name="deepnorm_ln",
    )(x, y, g, b)


def _dil_kernel(q_ref, kp_ref, kc_ref, vp_ref, vc_ref, o_ref, lse_ref):
    i = pl.program_id(1)
    blk = q_ref.shape[0]
    qi = lax.broadcasted_iota(jnp.int32, (blk, blk), 0)
    ki = lax.broadcasted_iota(jnp.int32, (blk, blk), 1)
    ok_prev = ki >= qi + jnp.where(i > 0, 0, blk)
    ok_cur = ki <= qi
    scale = HEAD_DIM ** -0.5
    for h in range(A_HEADS_PER_GROUP):
        sl = slice(h * HEAD_DIM, (h + 1) * HEAD_DIM)
        q = q_ref[:, sl]
        sp = lax.dot_general(q, kp_ref[:, sl], NT_DIMS, preferred_element_type=F32) * scale
        sc = lax.dot_general(q, kc_ref[:, sl], NT_DIMS, preferred_element_type=F32) * scale
        sp = jnp.where(ok_prev, sp, NEG_INF)
        sc = jnp.where(ok_cur, sc, NEG_INF)
        m = jnp.maximum(jnp.max(sp, axis=-1, keepdims=True), jnp.max(sc, axis=-1, keepdims=True))
        ep = jnp.exp(sp - m)
        ec = jnp.exp(sc - m)
        l = jnp.sum(ep, axis=-1, keepdims=True) + jnp.sum(ec, axis=-1, keepdims=True)
        lse = m + jnp.log(l)
        inv = 1.0 / l
        pp = (ep * inv).astype(BF16)
        pc = (ec * inv).astype(BF16)
        o = (jnp.dot(pp, vp_ref[:, sl], preferred_element_type=F32)
             + jnp.dot(pc, vc_ref[:, sl], preferred_element_type=F32))
        o_ref[:, sl] = o
        lse_ref[:, sl] = jnp.broadcast_to(lse, (blk, HEAD_DIM))


def _dilated_group(qkv, dil):
    sub = qkv.shape[2]
    assert sub % BAND == 0
    nb = sub // BAND
    prev = lambda i: jnp.maximum(i - 1, 0)

    def spec(which, row):
        return pl.BlockSpec((None, None, BAND, A_OUT), lambda r, i: (which, r, row(i), 0))

    cur = lambda i: i
    out_spec = pl.BlockSpec((None, BAND, A_OUT), lambda r, i: (r, i, 0))
    return pl.pallas_call(
        _dil_kernel,
        grid=(dil, nb),
        in_specs=[spec(0, cur), spec(1, prev), spec(1, cur), spec(2, prev), spec(2, cur)],
        out_specs=[out_spec, out_spec],
        out_shape=[jax.ShapeDtypeStruct((dil, sub, A_OUT), F32)] * 2,
        compiler_params=_params(("parallel", "parallel")),
        name=f"dilated_attn_d{dil}",
    )(qkv, qkv, qkv, qkv, qkv)


def _mix_kernel(*refs, dils):
    n = len(dils)
    o_refs, l_refs, y_ref = refs[:n], refs[n:2 * n], refs[2 * n]
    stage = refs[2 * n + 1:]
    tm = y_ref.shape[0]

    def natural(ref, slot, dil, cs):
        if dil == 1:
            return ref[0, :, cs]
        for r in range(dil):
            slot[pl.ds(r, tm // dil, stride=dil), :] = ref[r, :, cs]
        return slot[...]

    for c in range(A_OUT // LANES):
        cs = slice(c * LANES, (c + 1) * LANES)
        outs = [natural(o_refs[g], stage[2 * g], dils[g], cs) for g in range(n)]
        lses = [natural(l_refs[g], stage[2 * g + 1], dils[g], cs) for g in range(n)]
        m = functools.reduce(jnp.maximum, lses)
        es = [jnp.exp(l - m) for l in lses]
        inv = 1.0 / functools.reduce(lambda a, b: a + b, es)
        y = functools.reduce(lambda a, b: a + b, [(e * inv) * o for e, o in zip(es, outs)])
        y_ref[:, cs] = y.astype(y_ref.dtype)


def _mix_groups(outs, lses, dils, t):
    tm = _pick(t, (512, 256))
    specs = [pl.BlockSpec((d, tm // d, A_OUT), lambda i: (0, i, 0)) for d in dils]
    return pl.pallas_call(
        functools.partial(_mix_kernel, dils=tuple(dils)),
        grid=(t // tm,),
        in_specs=specs + specs,
        out_specs=pl.BlockSpec((tm, A_OUT), lambda i: (i, 0)),
        out_shape=jax.ShapeDtypeStruct((t, A_OUT), BF16),
        scratch_shapes=[pltpu.VMEM((tm, LANES), F32)] * (2 * len(dils)),
        compiler_params=_params(("parallel",)),
        name="mix_groups",
    )(*outs, *lses)


def _sortable(x):
    b = pltpu.bitcast(x, jnp.int32)
    return b ^ ((b >> 31) & jnp.int32(INT_MAX))


def _dsa_kernel(qi_ref, ki_ref, wi_ref, q_ref, k_ref, v_ref, o_ref,
                key_ref, bias_ref, qs_ref, wb_ref, thr_ref, mthr_ref, m_ref, acc_ref,
                *, tq, tkc, topk):
    i = pl.program_id(0)
    c = pl.program_id(1)
    last_c = ((i + 1) * tq - 1) // tkc
    n_ch = last_c + 1
    row_pos = i * tq + lax.broadcasted_iota(jnp.int32, (tq, tkc), 0)
    col_iota = lax.broadcasted_iota(jnp.int32, (tq, tkc), 1)
    sub = MXU_COLS
    row_sub = i * tq + lax.broadcasted_iota(jnp.int32, (tq, sub), 0)
    col_sub = lax.broadcasted_iota(jnp.int32, (tq, sub), 1)

    @pl.when(c == 0)
    def _select():
        lo_half = lax.broadcasted_iota(jnp.int32, (tq, LANES), 1) < IDX_DIM
        for hp in range(IDX_HEADS // 2):
            pair = qi_ref[:, hp * LANES:(hp + 1) * LANES]
            qs_ref[(2 * hp) * tq:(2 * hp + 1) * tq, :] = jnp.where(lo_half, pair, 0.0).astype(BF16)
            qs_ref[(2 * hp + 1) * tq:(2 * hp + 2) * tq, :] = jnp.where(lo_half, 0.0, pair).astype(BF16)
        w = wi_ref[...] * (IDX_DIM ** -0.5 * IDX_HEADS ** -0.5)
        for h in range(IDX_HEADS):
            wb_ref[h] = w[:, h:h + 1] + jnp.zeros((tq, LANES), F32)

        def score_chunk(kc, carry):
            off = pl.multiple_of(kc * tkc, tkc)
            for s in range(tkc // sub):
                kis = ki_ref[pl.ds(off + s * sub, sub), :].astype(BF16)
                r = lax.dot_general(qs_ref[...], kis, NT_DIMS, preferred_element_type=F32)
                halves = []
                for half in range(sub // LANES):
                    cs = slice(half * LANES, (half + 1) * LANES)
                    sc = jnp.zeros((tq, LANES), F32)
                    for h in range(IDX_HEADS):
                        sc = sc + wb_ref[h] * jnp.maximum(r[h * tq:(h + 1) * tq, cs], 0.0)
                    halves.append(sc)
                sc = jnp.concatenate(halves, axis=1)
                cols = slice(s * sub, (s + 1) * sub)
                sc = jnp.where(off + s * sub + col_sub <= row_sub, sc, NEG_INF)
                key_ref[kc, :, cols] = _sortable(sc)
            return carry

        lax.fori_loop(0, n_ch, score_chunk, 0)

        def count(pred):
            def body(kc, acc):
                off = pl.multiple_of(kc * tkc, tkc)
                part = jnp.where(pred(key_ref[kc], off), 1, 0).astype(jnp.int32)
                for s in range(tkc // LANES):
                    acc = acc + part[:, s * LANES:(s + 1) * LANES]
                return acc
            acc = lax.fori_loop(0, n_ch, body, jnp.zeros((tq, LANES), jnp.int32))
            return jnp.sum(acc, axis=1, keepdims=True)

        def unsettled(state):
            it, lo, hi, n_lo = state
            open_rows = jnp.where((n_lo == topk) | (lo == hi), 0, 1)
            return jnp.logical_and(it < 33, jnp.max(open_rows) > 0)

        def bisect(state):
            it, lo, hi, n_lo = state
            mid = (lo | hi) - ((lo ^ hi) >> 1)
            n_mid = count(lambda kk, off: kk >= mid)
            ok = n_mid >= topk
            return (it + 1, jnp.where(ok, mid, lo), jnp.where(ok, hi, mid - 1), jnp.where(ok, n_mid, n_lo))

        lo0 = jnp.full((tq, 1), INT_MIN, jnp.int32)
        hi0 = jnp.full((tq, 1), INT_MAX, jnp.int32)
        n0 = jnp.full((tq, 1), INT_MAX, jnp.int32)
        _, thr, _, n_ge = lax.while_loop(unsettled, bisect, (jnp.int32(0), lo0, hi0, n0))
        thr_ref[...] = thr
        mthr_ref[...] = jnp.full((tq, 1), INT_MAX, jnp.int32)

        @pl.when(jnp.max(n_ge) > topk)
        def _ties():
            need = topk - count(lambda kk, off: kk > thr)

            def bisect_idx(_, lohi):
                lo, hi = lohi
                mid = (lo + hi) >> 1
                ok = count(lambda kk, off: (kk == thr) & (off + col_iota <= mid)) >= need
                return jnp.where(ok, lo, mid + 1), jnp.where(ok, mid, hi)

            lo1 = jnp.zeros((tq, 1), jnp.int32)
            hi1 = jnp.full((tq, 1), 2 ** 30, jnp.int32)
            _, m_idx = lax.fori_loop(0, 31, bisect_idx, (lo1, hi1))
            mthr_ref[...] = m_idx

        m_ref[...] = jnp.full(m_ref.shape, M_FLOOR, F32)
        acc_ref[...] = jnp.zeros(acc_ref.shape, F32)

    @pl.when(c <= last_c)
    def _attend():
        off = pl.multiple_of(c * tkc, tkc)
        kk = key_ref[c]
        thr = thr_ref[...]
        pos = off + col_iota
        chosen = (kk > thr) | ((kk == thr) & (pos <= mthr_ref[...]))
        bias_ref[...] = jnp.where(chosen & (pos <= row_pos), 0.0, NEG_INF)
        ones = jnp.ones((tkc, HEAD_DIM), BF16)
        for h in range(B_HEADS):
            sl = slice(h * HEAD_DIM, (h + 1) * HEAD_DIM)
            s = lax.dot_general(q_ref[:, sl], k_ref[:, sl], NT_DIMS, preferred_element_type=F32) + bias_ref[...]
            m_old = m_ref[h]
            m_new = jnp.maximum(m_old, jnp.max(s, axis=-1, keepdims=True))
            p = jnp.exp2(s - m_new).astype(BF16)
            alpha = jnp.exp2(m_old - m_new)
            v_one = jnp.concatenate([v_ref[:, sl], ones], axis=1)
            acc_ref[h] = alpha * acc_ref[h] + jnp.dot(p, v_one, preferred_element_type=F32)
            m_ref[h] = m_new

    @pl.when(c == last_c)
    def _finish():
        for h in range(B_HEADS):
            a = acc_ref[h]
            o_ref[:, h * HEAD_DIM:(h + 1) * HEAD_DIM] = (a[:, :HEAD_DIM] / a[:, HEAD_DIM:]).astype(o_ref.dtype)


def _dsa(qi, kiwi, qk, v):
    t = qi.shape[0]
    tq = _pick(t, (256, 128))
    tkc = _pick(t, (1024, 512, 256))
    topk = min(DSA_TOPK, t // 4)
    assert tkc >= topk and t % tq == 0 and tkc % MXU_COLS == 0
    nq, nc = t // tq, t // tkc

    def kchunk(i, c):
        return jnp.minimum(c, ((i + 1) * tq - 1) // tkc)

    in_specs = [
        pl.BlockSpec((tq, IDX_WIDTH), lambda i, c: (i, 0)),
        pl.BlockSpec((t, LANES), lambda i, c: (0, 0)),
        pl.BlockSpec((tq, LANES), lambda i, c: (i, 1)),
        pl.BlockSpec((tq, B_WIDTH), lambda i, c: (i, 0)),
        pl.BlockSpec((tkc, B_WIDTH), lambda i, c: (kchunk(i, c), 1)),
        pl.BlockSpec((tkc, B_WIDTH), lambda i, c: (kchunk(i, c), 0)),
    ]
    return pl.pallas_call(
        functools.partial(_dsa_kernel, tq=tq, tkc=tkc, topk=topk),
        grid=(nq, nc),
        in_specs=in_specs,
        out_specs=pl.BlockSpec((tq, B_WIDTH), lambda i, c: (i, 0)),
        out_shape=jax.ShapeDtypeStruct((t, B_WIDTH), BF16),
        scratch_shapes=[
            pltpu.VMEM((nc, tq, tkc), jnp.int32),
            pltpu.VMEM((tq, tkc), F32),
            pltpu.VMEM((IDX_HEADS * tq, LANES), BF16),
            pltpu.VMEM((IDX_HEADS, tq, LANES), F32),
            pltpu.VMEM((tq, 1), jnp.int32),
            pltpu.VMEM((tq, 1), jnp.int32),
            pltpu.VMEM((B_HEADS, tq, 1), F32),
            pltpu.VMEM((B_HEADS, tq, 2 * HEAD_DIM), F32),
        ],
        compiler_params=_params(("parallel", "arbitrary")),
        name="dsa",
    )(qi, kiwi, kiwi, qk, qk, v)


def _mem_kernel(q_ref, kv_ref, o_ref):
    scale = HEAD_DIM ** -0.5
    for h in range(MEM_HEADS):
        sl = slice(h * HEAD_DIM, (h + 1) * HEAD_DIM)
        vsl = slice(MEM_WIDTH + h * HEAD_DIM, MEM_WIDTH + (h + 1) * HEAD_DIM)
        s = lax.dot_general(q_ref[:, sl], kv_ref[:, sl], NT_DIMS, preferred_element_type=F32) * scale
        m = jnp.max(s, axis=-1, keepdims=True)
        e = jnp.exp(s - m)
        p = e / jnp.sum(e, axis=-1, keepdims=True)
        o_ref[:, sl] = jnp.dot(p.astype(BF16), kv_ref[:, vsl], preferred_element_type=F32).astype(o_ref.dtype)


def _mem_attention(q, kv):
    t = q.shape[0]
    mlen = kv.shape[0]
    tq = _pick(t, (512, 256, 128))
    return pl.pallas_call(
        _mem_kernel,
        grid=(t // tq,),
        in_specs=[pl.BlockSpec((tq, MEM_WIDTH), lambda i: (i, 0)),
                  pl.BlockSpec((mlen, 2 * MEM_WIDTH), lambda i: (0, 0))],
        out_specs=pl.BlockSpec((tq, MEM_WIDTH), lambda i: (i, 0)),
        out_shape=jax.ShapeDtypeStruct((t, MEM_WIDTH), BF16),
        compiler_params=_params(("parallel",)),
        name="mem_attention",
    )(q, kv)


def _cast_kernel(x_ref, o_ref):
    o_ref[...] = x_ref[...].astype(o_ref.dtype)


def _to_bf16(x):
    t, d = x.shape
    tm = _pick(t, (256, 128))
    spec = pl.BlockSpec((tm, d), lambda i: (i, 0))
    return pl.pallas_call(
        _cast_kernel, grid=(t // tm,), in_specs=[spec], out_specs=spec,
        out_shape=jax.ShapeDtypeStruct((t, d), BF16),
        compiler_params=_params(("parallel",)), name="cast_bf16",
    )(x)


def _rope_tables(t):
    pos = jnp.arange(t, dtype=jnp.int32).astype(F32)[:, None]
    half = HEAD_DIM // 2
    inv = ROPE_THETA ** (-jnp.arange(half, dtype=F32) / half)
    ang = pos * inv[None, :]
    cos128 = jnp.concatenate([jnp.cos(ang)] * 2, axis=1)
    sin128 = jnp.concatenate([-jnp.sin(ang), jnp.sin(ang)], axis=1)
    half = IDX_DIM // 2
    inv = ROPE_THETA ** (-jnp.arange(half, dtype=F32) / half)
    ang = pos * inv[None, :]
    z = jnp.zeros_like(ang)
    cos64 = jnp.concatenate([jnp.cos(ang)] * 4, axis=1)
    sin_lo = jnp.concatenate([-jnp.sin(ang), z] * 2, axis=1)
    sin_hi = jnp.concatenate([z, jnp.sin(ang)] * 2, axis=1)
    return (cos128, sin128), (cos64, sin_lo, sin_hi)


def _mixer_branches(xb, w_in, w_kiwi, l, blk, rope128, rope64):
    t = xb.shape[0]
    outs, lses, dils = [], [], []
    for g, (win, dil) in enumerate(DIL_GROUPS):
        assert win // dil == BAND
        qkv = _matmul(xb, w_in, l, kind="a_group", extras=rope128, tn=COL_BLOCK, n_tiles=3, dil=dil,
                      src_block=lambda j, g=g: blk[0] + g + j * (A_WIDTH // COL_BLOCK), name=f"proj_a_g{g}")
        o, lse = _dilated_group(qkv, dil)
        outs.append(o)
        lses.append(lse)
        dils.append(dil)
    ya = _mix_groups(outs, lses, dils, t)
    qk_b = _matmul(xb, w_in, l, kind="rope128_qk", extras=rope128, tn=COL_BLOCK,
                   n_tiles=2 * B_WIDTH // COL_BLOCK, src_block=lambda j: blk[3] + j, name="proj_b_qk")
    v_b = _matmul(xb, w_in, l, tn=COL_BLOCK, n_tiles=B_WIDTH // COL_BLOCK,
                  src_block=lambda j: blk[5] + j, name="proj_b_v")
    qi = _matmul(xb, w_in, l, kind="rope64", extras=rope64, out_dtype=F32, tn=COL_BLOCK,
                 n_tiles=IDX_WIDTH // COL_BLOCK, src_block=lambda j: blk[6] + j, name="proj_qi")
    kiwi = _matmul(xb, w_kiwi, l, kind="kiwi", extras=rope64, out_dtype=F32, name="proj_kiwi")
    yb = _dsa(qi, kiwi, qk_b, v_b)
    return ya, yb


def kernel(x, mem, w_in, w_pa, w_pb, w_o, ln1_g, ln1_b, w_mq, w_mkv, w_mo, ln2_g, ln2_b, w_up, w_down, ln3_g, ln3_b):
    bsz, t, d = x.shape
    depth = w_in.shape[0]
    assert bsz == 1
    alpha = (2 * depth) ** 0.25

    sizes = (A_WIDTH, A_WIDTH, A_WIDTH, B_WIDTH, B_WIDTH, B_WIDTH, IDX_WIDTH, IDX_DIM, IDX_HEADS, d, d)
    offs = [0]
    for s in sizes:
        offs.append(offs[-1] + s)
    assert offs[-1] == w_in.shape[2] and all(o % COL_BLOCK == 0 for o in offs[:8])
    blk = [o // COL_BLOCK for o in offs[:8]]
    col = lambda a, b: w_in[:, :, offs[a]:offs[b]]
    w_kiwi = jnp.concatenate(
        [col(7, 8), col(7, 9), jnp.zeros((depth, d, LANES - IDX_HEADS), F32)], axis=2).astype(BF16)
    w_gate = col(9, 11).astype(BF16)
    lnp = [p.reshape(depth, 1, d) for p in (ln1_g, ln1_b, ln2_g, ln2_b, ln3_g, ln3_b)]

    (cos128, sin128), (cos64, sin_lo, sin_hi) = _rope_tables(t)
    tm_rows = _pick(t, (1024, 512, 256))

    xf = x[0]
    xb = _to_bf16(xf)
    memb = _to_bf16(mem[0])

    def row_tab(arr):
        return (arr, (tm_rows, LANES), lambda i, j: (i, 0))

    def tile(arr, tn, joff=0):
        return (arr, (tm_rows, tn), lambda i, j: (i, j + joff))

    rope128 = (row_tab(cos128), row_tab(sin128))
    rope64 = (row_tab(cos64), row_tab(sin_lo), row_tab(sin_hi))
    tn_d = d if d <= 1280 else _pick(d, (1024, 512, 256, 128))

    for l in range(depth):
        ya, yb = _mixer_branches(xb, w_in, w_kiwi, l, blk, rope128, rope64)
        gates = _matmul(xb, w_gate, l, kind="sigmoid", out_dtype=F32, name="proj_gate")
        pa = _matmul(ya, w_pa, l, kind="gate", out_dtype=F32, extras=(tile(gates, tn_d),), name="proj_a")
        merged = _matmul(yb, w_pb, l, kind="gate_add",
                         extras=(tile(pa, tn_d), tile(gates, tn_d, d // tn_d)), name="proj_b_merge")
        y = _matmul(merged, w_o, l, out_dtype=F32, name="mixer_out")
        xf, xb = _deepnorm_ln(xf, y, lnp[0], lnp[1], l, alpha)
        qm = _matmul(xb, w_mq, l, name="mem_q")
        kvm = _matmul(memb, w_mkv, l, name="mem_kv")
        om = _mem_attention(qm, kvm)
        y = _matmul(om, w_mo, l, out_dtype=F32, name="mem_out")
        xf, xb = _deepnorm_ln(xf, y, lnp[2], lnp[3], l, alpha)
        hdn = _matmul(xb, w_up, l, kind="relu2", name="mlp_up")
        y = _matmul(hdn, w_down, l, out_dtype=F32, name="mlp_down")
        xf, xb = _deepnorm_ln(xf, y, lnp[4], lnp[5], l, alpha)
    return xf[None]
```

```python
import functools

import jax
import jax.numpy as jnp
from jax import lax
from jax.experimental import pallas as pl
from jax.experimental.pallas import tpu as pltpu

F32 = jnp.float32
BF16 = jnp.bfloat16

HEAD_DIM = 128
DIL_GROUPS = ((128, 1), (512, 4), (2048, 16))
A_HEADS_PER_GROUP = 4
A_GROUPS = len(DIL_GROUPS)
A_WIDTH = A_GROUPS * A_HEADS_PER_GROUP * HEAD_DIM
A_OUT = A_HEADS_PER_GROUP * HEAD_DIM
B_HEADS = 8
B_WIDTH = B_HEADS * HEAD_DIM
IDX_HEADS = 16
IDX_DIM = 64
IDX_WIDTH = IDX_HEADS * IDX_DIM
DSA_TOPK = 256
MEM_HEADS = 4
MEM_WIDTH = MEM_HEADS * HEAD_DIM
ROPE_THETA = 10000.0
LN_EPS = 1e-5
NEG_INF = -1e30
M_FLOOR = -1e29
BAND = 128
LOG2E = 1.4426950408889634

LANES = 128
MXU_COLS = 256
VMEM_LIMIT = 56 * 1024 * 1024
B_TILE_BYTES = 8 * 1024 * 1024
COL_BLOCK = 512

INT_MIN = -(2 ** 31)
INT_MAX = 2 ** 31 - 1
NT_DIMS = (((1,), (1,)), ((), ()))


def _pick(n, prefs):
    for p in prefs:
        if n % p == 0:
            return p
    return n


def _params(sem):
    return pltpu.CompilerParams(dimension_semantics=sem, vmem_limit_bytes=VMEM_LIMIT)


def _rope128(x, cos, sin):
    return x * cos + pltpu.roll(x, 64, 1) * sin


def _rope64(x, cos, sin_lo, sin_hi):
    return x * cos + pltpu.roll(x, 96, 1) * sin_lo + pltpu.roll(x, 32, 1) * sin_hi


def _rope128_tile(acc, cos, sin):
    return [_rope128(acc[:, c * LANES:(c + 1) * LANES], cos, sin) for c in range(acc.shape[1] // LANES)]


def _epilogue(kind, acc, extras, o_ref, stage_ref, dil, j):
    tn = acc.shape[1]
    if kind == "none":
        o_ref[...] = acc.astype(o_ref.dtype)
    elif kind == "rope128":
        for c, y in enumerate(_rope128_tile(acc, extras[0][...], extras[1][...])):
            o_ref[:, c * LANES:(c + 1) * LANES] = y.astype(o_ref.dtype)
    elif kind == "rope128_qk":
        qs = jnp.where(j < B_WIDTH // COL_BLOCK, HEAD_DIM ** -0.5 * LOG2E, 1.0)
        for c, y in enumerate(_rope128_tile(acc, extras[0][...], extras[1][...])):
            o_ref[:, c * LANES:(c + 1) * LANES] = (y * qs).astype(o_ref.dtype)
    elif kind == "rope64":
        cos, s_lo, s_hi = extras[0][...], extras[1][...], extras[2][...]
        for c in range(tn // LANES):
            sl = slice(c * LANES, (c + 1) * LANES)
            o_ref[:, sl] = _rope64(acc[:, sl], cos, s_lo, s_hi).astype(o_ref.dtype)
    elif kind == "kiwi":
        cos, s_lo, s_hi = extras[0][...], extras[1][...], extras[2][...]
        o_ref[:, :LANES] = _rope64(acc[:, :LANES], cos, s_lo, s_hi).astype(o_ref.dtype)
        o_ref[:, LANES:] = acc[:, LANES:].astype(o_ref.dtype)
    elif kind == "a_group":
        @pl.when(j < 2)
        def _():
            for c, y in enumerate(_rope128_tile(acc, extras[0][...], extras[1][...])):
                stage_ref[c] = y

        @pl.when(j == 2)
        def _():
            for c in range(tn // LANES):
                stage_ref[c] = acc[:, c * LANES:(c + 1) * LANES]

        rows = acc.shape[0] // dil
        for r in range(dil):
            for c in range(tn // LANES):
                src = stage_ref[c] if dil == 1 else stage_ref[c, pl.ds(r, rows, stride=dil), :]
                o_ref[r, :, c * LANES:(c + 1) * LANES] = src.astype(o_ref.dtype)
    elif kind == "sigmoid":
        o_ref[...] = (1.0 / (1.0 + jnp.exp(-acc))).astype(o_ref.dtype)
    elif kind == "relu2":
        r = jnp.maximum(acc, 0.0)
        o_ref[...] = (r * r).astype(o_ref.dtype)
    elif kind == "gate":
        o_ref[...] = (extras[0][...] * acc).astype(o_ref.dtype)
    elif kind == "gate_add":
        o_ref[...] = (extras[0][...] + extras[1][...] * acc).astype(o_ref.dtype)
    else:
        raise ValueError(kind)


def _mm_kernel(*refs, nk, kind, n_extra, dil):
    a_ref, b_ref = refs[0], refs[1]
    extras = refs[2:2 + n_extra]
    o_ref = refs[2 + n_extra]
    scratch = list(refs[3 + n_extra:])
    acc_ref = scratch.pop(0) if nk > 1 else None
    stage_ref = scratch.pop(0) if kind == "a_group" else None
    j = pl.program_id(1)
    k = pl.program_id(2)
    part = jnp.dot(a_ref[...], b_ref[...].astype(BF16), preferred_element_type=F32)
    if nk == 1:
        _epilogue(kind, part, extras, o_ref, stage_ref, dil, j)
        return

    @pl.when(k == 0)
    def _():
        acc_ref[...] = part

    @pl.when(k > 0)
    def _():
        acc_ref[...] += part

    @pl.when(k == nk - 1)
    def _():
        _epilogue(kind, acc_ref[...], extras, o_ref, stage_ref, dil, j)


def _matmul(a, b, layer, *, kind="none", extras=(), out_dtype=BF16, name="mm",
            tn=None, n_tiles=None, src_block=None, dil=1):
    m, k = a.shape
    nb = b.shape[2]
    tm = _pick(m, (1024, 512, 256))
    if tn is None:
        tn = nb if nb <= 1280 else _pick(nb, (1024, 512, 256, 128))
    if n_tiles is None:
        n_tiles = nb // tn
    if src_block is None:
        src_block = lambda j: j
    itemsize = jnp.dtype(b.dtype).itemsize
    tk = k
    for cand in (k, 4096, 2048, 1024, 512):
        if cand <= k and k % cand == 0:
            tk = cand
            if cand * tn * itemsize <= B_TILE_BYTES:
                break
    nk = k // tk
    assert m % tm == 0 and k % tk == 0 and tm % dil == 0
    in_specs = [
        pl.BlockSpec((tm, tk), lambda i, j, kk: (i, kk)),
        pl.BlockSpec((None, tk, tn), lambda i, j, kk: (layer, kk, src_block(j))),
    ]
    ops = [a, b]
    for arr, bshape, imap in extras:
        in_specs.append(pl.BlockSpec(bshape, functools.partial(lambda i, j, kk, f: f(i, j), f=imap)))
        ops.append(arr)
    scratch = [pltpu.VMEM((tm, tn), F32)] if nk > 1 else []
    if kind == "a_group":
        scratch.append(pltpu.VMEM((tn // LANES, tm, LANES), F32))
        out_shape = jax.ShapeDtypeStruct((n_tiles, dil, m // dil, tn), out_dtype)
        out_spec = pl.BlockSpec((None, dil, tm // dil, tn), lambda i, j, kk: (j, 0, i, 0))
    else:
        out_shape = jax.ShapeDtypeStruct((m, n_tiles * tn), out_dtype)
        out_spec = pl.BlockSpec((tm, tn), lambda i, j, kk: (i, j))
    return pl.pallas_call(
        functools.partial(_mm_kernel, nk=nk, kind=kind, n_extra=len(extras), dil=dil),
        grid=(m // tm, n_tiles, nk),
        in_specs=in_specs,
        out_specs=out_spec,
        out_shape=out_shape,
        scratch_shapes=scratch,
        compiler_params=_params(("parallel", "parallel", "arbitrary")),
        name=name,
    )(*ops)


def _panel_kernel(*refs, kind, n_extra, dil, shift, trans_b, tn):
    a_ref, b_ref = refs[0], refs[1]
    n_b = 2 if shift else 1
    extras = refs[1 + n_b:1 + n_b + n_extra]
    o_ref = refs[1 + n_b + n_extra]
    scratch = list(refs[2 + n_b + n_extra:])
    bb_ref = scratch.pop(0)
    stage_ref = scratch.pop(0) if kind == "a_group" else None
    j = pl.program_id(0)
    i = pl.program_id(1)

    @pl.when(i == 0)
    def _():
        w = b_ref[...].astype(BF16)
        if kind == "kiwi":
            q = IDX_DIM
            bb_ref[0:q, :] = w[0:q]
            bb_ref[q:2 * q, :] = w[0:q]
            bb_ref[2 * q:3 * q, :] = w[q:2 * q]
            bb_ref[3 * q:4 * q, :] = w[q:2 * q]
        elif trans_b:
            bb_ref[:b_ref.shape[0], :] = w
            if shift:
                bb_ref[b_ref.shape[0]:, :] = refs[2][...].astype(BF16)
        else:
            bb_ref[...] = w

    if trans_b:
        acc = lax.dot_general(a_ref[...], bb_ref[shift:shift + tn, :], NT_DIMS, preferred_element_type=F32)
    else:
        acc = jnp.dot(a_ref[...], bb_ref[...], preferred_element_type=F32)
    _epilogue(kind, acc, extras, o_ref, stage_ref, dil, j)


def _panel_matmul(a, b, layer, *, tn, n_tiles, src_block, kind="none", extras=(), out_dtype=BF16,
                  name="mm", dil=1, shift=0, trans_b=False):
    m, k = a.shape
    tm = _pick(m, (1024, 512, 256))
    assert m % tm == 0 and tm % dil == 0 and tn % LANES == 0
    assert shift == 0 or (trans_b and shift % 16 == 0 and shift < LANES)
    out_tn = 2 * tn if kind == "kiwi" else tn
    b_rows = tn + (LANES if shift else 0)
    in_specs = [pl.BlockSpec((tm, k), lambda j, i: (i, 0))]
    ops = [a, b]
    if trans_b:
        in_specs.append(pl.BlockSpec((None, tn, k), lambda j, i: (layer, src_block(j), 0)))
        if shift:
            per = tn // LANES
            in_specs.append(pl.BlockSpec((None, LANES, k), lambda j, i: (layer, (src_block(j) + 1) * per, 0)))
            ops.append(b)
        scratch = [pltpu.VMEM((out_tn if kind == "kiwi" else b_rows, k), BF16)]
    else:
        in_specs.append(pl.BlockSpec((None, k, tn), lambda j, i: (layer, 0, src_block(j))))
        scratch = [pltpu.VMEM((k, tn), BF16)]
    for arr, bshape, imap in extras:
        in_specs.append(pl.BlockSpec(bshape, functools.partial(lambda j, i, f: f(i, j), f=imap)))
        ops.append(arr)
    if kind == "a_group":
        scratch.append(pltpu.VMEM((tn // LANES, tm, LANES), F32))
        out_shape = jax.ShapeDtypeStruct((n_tiles, dil, m // dil, tn), out_dtype)
        out_spec = pl.BlockSpec((None, dil, tm // dil, tn), lambda j, i: (j, 0, i, 0))
    else:
        out_shape = jax.ShapeDtypeStruct((m, n_tiles * out_tn), out_dtype)
        out_spec = pl.BlockSpec((tm, out_tn), lambda j, i: (i, j))
    return pl.pallas_call(
        functools.partial(_panel_kernel, kind=kind, n_extra=len(extras), dil=dil, shift=shift,
                          trans_b=trans_b, tn=out_tn),
        grid=(n_tiles, m // tm),
        in_specs=in_specs,
        out_specs=out_spec,
        out_shape=out_shape,
        scratch_shapes=scratch,
        compiler_params=_params(("arbitrary", "arbitrary")),
        name=name,
    )(*ops)


def _ln_kernel(x_ref, y_ref, g_ref, b_ref, o_ref, ob_ref, *, alpha):
    z = alpha * x_ref[...] + y_ref[...]
    mu = jnp.mean(z, axis=-1, keepdims=True)
    zc = z - mu
    var = jnp.mean(zc * zc, axis=-1, keepdims=True)
    out = (zc * lax.rsqrt(var + LN_EPS)) * g_ref[...] + b_ref[...]
    o_ref[...] = out
    ob_ref[...] = out.astype(BF16)


def _deepnorm_ln(x, y, g, b, layer, alpha):
    t, d = x.shape
    tm = _pick(t, (128,))
    row = pl.BlockSpec((tm, d), lambda i: (i, 0))
    par = pl.BlockSpec((None, 1, d), lambda i: (layer, 0, 0))
    return pl.pallas_call(
        functools.partial(_ln_kernel, alpha=alpha),
        grid=(t // tm,),
        in_specs=[row, row, par, par],
        out_specs=[row, row],
        out_shape=[jax.ShapeDtypeStruct((t, d), F32), jax.ShapeDtypeStruct((t, d), BF16)],
        compiler_params=_params(("parallel",)),
        name="deepnorm_ln",
    )(x, y, g, b)


def _dil_kernel(q_ref, kp_ref, kc_ref, vp_ref, vc_ref, o_ref, lse_ref):
    i = pl.program_id(1)
    blk = q_ref.shape[0]
    qi = lax.broadcasted_iota(jnp.int32, (blk, blk), 0)
    ki = lax.broadcasted_iota(jnp.int32, (blk, blk), 1)
    ok_prev = ki >= qi + jnp.where(i > 0, 0, blk)
    ok_cur = ki <= qi
    scale = HEAD_DIM ** -0.5
    for h in range(A_HEADS_PER_GROUP):
        sl = slice(h * HEAD_DIM, (h + 1) * HEAD_DIM)
        q = q_ref[:, sl]
        sp = lax.dot_general(q, kp_ref[:, sl], NT_DIMS, preferred_element_type=F32) * scale
        sc = lax.dot_general(q, kc_ref[:, sl], NT_DIMS, preferred_element_type=F32) * scale
        sp = jnp.where(ok_prev, sp, NEG_INF)
        sc = jnp.where(ok_cur, sc, NEG_INF)
        m = jnp.maximum(jnp.max(sp, axis=-1, keepdims=True), jnp.max(sc, axis=-1, keepdims=True))
        ep = jnp.exp(sp - m)
        ec = jnp.exp(sc - m)
        l = jnp.sum(ep, axis=-1, keepdims=True) + jnp.sum(ec, axis=-1, keepdims=True)
        lse = m + jnp.log(l)
        inv = 1.0 / l
        pp = (ep * inv).astype(BF16)
        pc = (ec * inv).astype(BF16)
        o = (jnp.dot(pp, vp_ref[:, sl], preferred_element_type=F32)
             + jnp.dot(pc, vc_ref[:, sl], preferred_element_type=F32))
        o_ref[:, sl] = o
        lse_ref[:, sl] = jnp.broadcast_to(lse, (blk, HEAD_DIM))


def _dilated_group(qkv, dil):
    sub = qkv.shape[2]
    assert sub % BAND == 0
    nb = sub // BAND
    prev = lambda i: jnp.maximum(i - 1, 0)

    def spec(which, row):
        return pl.BlockSpec((None, None, BAND, A_OUT), lambda r, i: (which, r, row(i), 0))

    cur = lambda i: i
    out_spec = pl.BlockSpec((None, BAND, A_OUT), lambda r, i: (r, i, 0))
    return pl.pallas_call(
        _dil_kernel,
        grid=(dil, nb),
        in_specs=[spec(0, cur), spec(1, prev), spec(1, cur), spec(2, prev), spec(2, cur)],
        out_specs=[out_spec, out_spec],
        out_shape=[jax.ShapeDtypeStruct((dil, sub, A_OUT), F32)] * 2,
        compiler_params=_params(("parallel", "parallel")),
        name=f"dilated_attn_d{dil}",
    )(qkv, qkv, qkv, qkv, qkv)


def _mix_kernel(*refs, dils):
    n = len(dils)
    o_refs, l_refs, y_ref = refs[:n], refs[n:2 * n], refs[2 * n]
    stage = refs[2 * n + 1:]
    tm = y_ref.shape[0]

    def natural(ref, slot, dil, cs):
        if dil == 1:
            return ref[0, :, cs]
        for r in range(dil):
            slot[pl.ds(r, tm // dil, stride=dil), :] = ref[r, :, cs]
        return slot[...]

    for c in range(A_OUT // LANES):
        cs = slice(c * LANES, (c + 1) * LANES)
        outs = [natural(o_refs[g], stage[2 * g], dils[g], cs) for g in range(n)]
        lses = [natural(l_refs[g], stage[2 * g + 1], dils[g], cs) for g in range(n)]
        m = functools.reduce(jnp.maximum, lses)
        es = [jnp.exp(l - m) for l in lses]
        inv = 1.0 / functools.reduce(lambda a, b: a + b, es)
        y = functools.reduce(lambda a, b: a + b, [(e * inv) * o for e, o in zip(es, outs)])
        y_ref[:, cs] = y.astype(y_ref.dtype)


def _mix_groups(outs, lses, dils, t):
    tm = _pick(t, (512, 256))
    specs = [pl.BlockSpec((d, tm // d, A_OUT), lambda i: (0, i, 0)) for d in dils]
    return pl.pallas_call(
        functools.partial(_mix_kernel, dils=tuple(dils)),
        grid=(t // tm,),
        in_specs=specs + specs,
        out_specs=pl.BlockSpec((tm, A_OUT), lambda i: (i, 0)),
        out_shape=jax.ShapeDtypeStruct((t, A_OUT), BF16),
        scratch_shapes=[pltpu.VMEM((tm, LANES), F32)] * (2 * len(dils)),
        compiler_params=_params(("parallel",)),
        name="mix_groups",
    )(*outs, *lses)


def _sortable(x):
    b = pltpu.bitcast(x, jnp.int32)
    return b ^ ((b >> 31) & jnp.int32(INT_MAX))


def _dsa_kernel(qi_ref, ki_ref, wi_ref, q_ref, k_ref, v_ref, o_ref,
                key_ref, bias_ref, qs_ref, wb_ref, thr_ref, mthr_ref, m_ref, acc_ref,
                *, tq, tkc, topk):
    i = pl.program_id(0)
    c = pl.program_id(1)
    last_c = ((i + 1) * tq - 1) // tkc
    n_ch = last_c + 1
    row_pos = i * tq + lax.broadcasted_iota(jnp.int32, (tq, tkc), 0)
    col_iota = lax.broadcasted_iota(jnp.int32, (tq, tkc), 1)
    sub = MXU_COLS
    row_sub = i * tq + lax.broadcasted_iota(jnp.int32, (tq, sub), 0)
    col_sub = lax.broadcasted_iota(jnp.int32, (tq, sub), 1)

    @pl.when(c == 0)
    def _select():
        lo_half = lax.broadcasted_iota(jnp.int32, (tq, LANES), 1) < IDX_DIM
        for hp in range(IDX_HEADS // 2):
            pair = qi_ref[:, hp * LANES:(hp + 1) * LANES]
            qs_ref[(2 * hp) * tq:(2 * hp + 1) * tq, :] = jnp.where(lo_half, pair, 0.0).astype(BF16)
            qs_ref[(2 * hp + 1) * tq:(2 * hp + 2) * tq, :] = jnp.where(lo_half, 0.0, pair).astype(BF16)
        w = wi_ref[...] * (IDX_DIM ** -0.5 * IDX_HEADS ** -0.5)
        for h in range(IDX_HEADS):
            wb_ref[h] = w[:, h:h + 1] + jnp.zeros((tq, LANES), F32)

        def score_chunk(kc, carry):
            off = pl.multiple_of(kc * tkc, tkc)
            for s in range(tkc // sub):
                kis = ki_ref[pl.ds(off + s * sub, sub), :].astype(BF16)
                r = lax.dot_general(qs_ref[...], kis, NT_DIMS, preferred_element_type=F32)
                halves = []
                for half in range(sub // LANES):
                    cs = slice(half * LANES, (half + 1) * LANES)
                    sc = jnp.zeros((tq, LANES), F32)
                    for h in range(IDX_HEADS):
                        sc = sc + wb_ref[h] * jnp.maximum(r[h * tq:(h + 1) * tq, cs], 0.0)
                    halves.append(sc)
                sc = jnp.concatenate(halves, axis=1)
                cols = slice(s * sub, (s + 1) * sub)
                sc = jnp.where(off + s * sub + col_sub <= row_sub, sc, NEG_INF)
                key_ref[kc, :, cols] = _sortable(sc)
            return carry

        lax.fori_loop(0, n_ch, score_chunk, 0)

        def count(pred):
            def body(kc, acc):
                off = pl.multiple_of(kc * tkc, tkc)
                part = jnp.where(pred(key_ref[kc], off), 1, 0).astype(jnp.int32)
                for s in range(tkc // LANES):
                    acc = acc + part[:, s * LANES:(s + 1) * LANES]
                return acc
            acc = lax.fori_loop(0, n_ch, body, jnp.zeros((tq, LANES), jnp.int32))
            return jnp.sum(acc, axis=1, keepdims=True)

        def unsettled(state):
            it, lo, hi, n_lo = state
            open_rows = jnp.where((n_lo == topk) | (lo == hi), 0, 1)
            return jnp.logical_and(it < 33, jnp.max(open_rows) > 0)

        def bisect(state):
            it, lo, hi, n_lo = state
            mid = (lo | hi) - ((lo ^ hi) >> 1)
            n_mid = count(lambda kk, off: kk >= mid)
            ok = n_mid >= topk
            return (it + 1, jnp.where(ok, mid, lo), jnp.where(ok, hi, mid - 1), jnp.where(ok, n_mid, n_lo))

        lo0 = jnp.full((tq, 1), INT_MIN, jnp.int32)
        hi0 = jnp.full((tq, 1), INT_MAX, jnp.int32)
        n0 = jnp.full((tq, 1), INT_MAX, jnp.int32)
        _, thr, _, n_ge = lax.while_loop(unsettled, bisect, (jnp.int32(0), lo0, hi0, n0))
        thr_ref[...] = thr
        mthr_ref[...] = jnp.full((tq, 1), INT_MAX, jnp.int32)

        @pl.when(jnp.max(n_ge) > topk)
        def _ties():
            need = topk - count(lambda kk, off: kk > thr)

            def bisect_idx(_, lohi):
                lo, hi = lohi
                mid = (lo + hi) >> 1
                ok = count(lambda kk, off: (kk == thr) & (off + col_iota <= mid)) >= need
                return jnp.where(ok, lo, mid + 1), jnp.where(ok, mid, hi)

            lo1 = jnp.zeros((tq, 1), jnp.int32)
            hi1 = jnp.full((tq, 1), 2 ** 30, jnp.int32)
            _, m_idx = lax.fori_loop(0, 31, bisect_idx, (lo1, hi1))
            mthr_ref[...] = m_idx

        m_ref[...] = jnp.full(m_ref.shape, M_FLOOR, F32)
        acc_ref[...] = jnp.zeros(acc_ref.shape, F32)

    @pl.when(c <= last_c)
    def _attend():
        off = pl.multiple_of(c * tkc, tkc)
        kk = key_ref[c]
        thr = thr_ref[...]
        pos = off + col_iota
        chosen = (kk > thr) | ((kk == thr) & (pos <= mthr_ref[...]))
        bias_ref[...] = jnp.where(chosen & (pos <= row_pos), 0.0, NEG_INF)
        ones = jnp.ones((tkc, HEAD_DIM), BF16)
        for h in range(B_HEADS):
            sl = slice(h * HEAD_DIM, (h + 1) * HEAD_DIM)
            s = lax.dot_general(q_ref[:, sl], k_ref[:, sl], NT_DIMS, preferred_element_type=F32) + bias_ref[...]
            m_old = m_ref[h]
            m_new = jnp.maximum(m_old, jnp.max(s, axis=-1, keepdims=True))
            p = jnp.exp2(s - m_new).astype(BF16)
            alpha = jnp.exp2(m_old - m_new)
            v_one = jnp.concatenate([v_ref[:, sl], ones], axis=1)
            acc_ref[h] = alpha * acc_ref[h] + jnp.dot(p, v_one, preferred_element_type=F32)
            m_ref[h] = m_new

    @pl.when(c == last_c)
    def _finish():
        for h in range(B_HEADS):
            a = acc_ref[h]
            o_ref[:, h * HEAD_DIM:(h + 1) * HEAD_DIM] = (a[:, :HEAD_DIM] / a[:, HEAD_DIM:]).astype(o_ref.dtype)


def _dsa(qi, kiwi, qk, v):
    t = qi.shape[0]
    tq = _pick(t, (256, 128))
    tkc = _pick(t, (1024, 512, 256))
    topk = min(DSA_TOPK, t // 4)
    assert tkc >= topk and t % tq == 0 and tkc % MXU_COLS == 0
    nq, nc = t // tq, t // tkc

    def kchunk(i, c):
        return jnp.minimum(c, ((i + 1) * tq - 1) // tkc)

    in_specs = [
        pl.BlockSpec((tq, IDX_WIDTH), lambda i, c: (i, 0)),
        pl.BlockSpec((t, LANES), lambda i, c: (0, 0)),
        pl.BlockSpec((tq, LANES), lambda i, c: (i, 1)),
        pl.BlockSpec((tq, B_WIDTH), lambda i, c: (i, 0)),
        pl.BlockSpec((tkc, B_WIDTH), lambda i, c: (kchunk(i, c), 1)),
        pl.BlockSpec((tkc, B_WIDTH), lambda i, c: (kchunk(i, c), 0)),
    ]
    return pl.pallas_call(
        functools.partial(_dsa_kernel, tq=tq, tkc=tkc, topk=topk),
        grid=(nq, nc),
        in_specs=in_specs,
        out_specs=pl.BlockSpec((tq, B_WIDTH), lambda i, c: (i, 0)),
        out_shape=jax.ShapeDtypeStruct((t, B_WIDTH), BF16),
        scratch_shapes=[
            pltpu.VMEM((nc, tq, tkc), jnp.int32),
            pltpu.VMEM((tq, tkc), F32),
            pltpu.VMEM((IDX_HEADS * tq, LANES), BF16),
            pltpu.VMEM((IDX_HEADS, tq, LANES), F32),
            pltpu.VMEM((tq, 1), jnp.int32),
            pltpu.VMEM((tq, 1), jnp.int32),
            pltpu.VMEM((B_HEADS, tq, 1), F32),
            pltpu.VMEM((B_HEADS, tq, 2 * HEAD_DIM), F32),
        ],
        compiler_params=_params(("parallel", "arbitrary")),
        name="dsa",
    )(qi, kiwi, kiwi, qk, qk, v)


def _mem_kernel(q_ref, kv_ref, o_ref):
    scale = HEAD_DIM ** -0.5
    for h in range(MEM_HEADS):
        sl = slice(h * HEAD_DIM, (h + 1) * HEAD_DIM)
        vsl = slice(MEM_WIDTH + h * HEAD_DIM, MEM_WIDTH + (h + 1) * HEAD_DIM)
        s = lax.dot_general(q_ref[:, sl], kv_ref[:, sl], NT_DIMS, preferred_element_type=F32) * scale
        m = jnp.max(s, axis=-1, keepdims=True)
        e = jnp.exp(s - m)
        p = e / jnp.sum(e, axis=-1, keepdims=True)
        o_ref[:, sl] = jnp.dot(p.astype(BF16), kv_ref[:, vsl], preferred_element_type=F32).astype(o_ref.dtype)


def _mem_attention(q, kv):
    t = q.shape[0]
    mlen = kv.shape[0]
    tq = _pick(t, (512, 256, 128))
    return pl.pallas_call(
        _mem_kernel,
        grid=(t // tq,),
        in_specs=[pl.BlockSpec((tq, MEM_WIDTH), lambda i: (i, 0)),
                  pl.BlockSpec((mlen, 2 * MEM_WIDTH), lambda i: (0, 0))],
        out_specs=pl.BlockSpec((tq, MEM_WIDTH), lambda i: (i, 0)),
        out_shape=jax.ShapeDtypeStruct((t, MEM_WIDTH), BF16),
        compiler_params=_params(("parallel",)),
        name="mem_attention",
    )(q, kv)


def _cast_kernel(x_ref, o_ref):
    o_ref[...] = x_ref[...].astype(o_ref.dtype)


def _to_bf16(x):
    t, d = x.shape
    tm = _pick(t, (256, 128))
    spec = pl.BlockSpec((tm, d), lambda i: (i, 0))
    return pl.pallas_call(
        _cast_kernel, grid=(t // tm,), in_specs=[spec], out_specs=spec,
        out_shape=jax.ShapeDtypeStruct((t, d), BF16),
        compiler_params=_params(("parallel",)), name="cast_bf16",
    )(x)


def _rope_tables(t):
    pos = jnp.arange(t, dtype=jnp.int32).astype(F32)[:, None]
    half = HEAD_DIM // 2
    inv = ROPE_THETA ** (-jnp.arange(half, dtype=F32) / half)
    ang = pos * inv[None, :]
    cos128 = jnp.concatenate([jnp.cos(ang)] * 2, axis=1)
    sin128 = jnp.concatenate([-jnp.sin(ang), jnp.sin(ang)], axis=1)
    half = IDX_DIM // 2
    inv = ROPE_THETA ** (-jnp.arange(half, dtype=F32) / half)
    ang = pos * inv[None, :]
    z = jnp.zeros_like(ang)
    cos64 = jnp.concatenate([jnp.cos(ang)] * 4, axis=1)
    sin_lo = jnp.concatenate([-jnp.sin(ang), z] * 2, axis=1)
    sin_hi = jnp.concatenate([z, jnp.sin(ang)] * 2, axis=1)
    return (cos128, sin128), (cos64, sin_lo, sin_hi)


def _mixer_branches(xb, w_in_t, l, blk, rope128, rope64):
    t = xb.shape[0]
    proj = functools.partial(_panel_matmul, xb, w_in_t, l, trans_b=True)
    outs, lses, dils = [], [], []
    for g, (win, dil) in enumerate(DIL_GROUPS):
        assert win // dil == BAND
        qkv = proj(kind="a_group", extras=rope128, tn=COL_BLOCK, n_tiles=3, dil=dil,
                   src_block=lambda j, g=g: blk[0] + g + j * (A_WIDTH // COL_BLOCK), name=f"proj_a_g{g}")
        o, lse = _dilated_group(qkv, dil)
        outs.append(o)
        lses.append(lse)
        dils.append(dil)
    ya = _mix_groups(outs, lses, dils, t)
    qk_b = proj(kind="rope128_qk", extras=rope128, tn=COL_BLOCK, n_tiles=2 * B_WIDTH // COL_BLOCK,
                src_block=lambda j: blk[3] + j, name="proj_b_qk")
    v_b = proj(tn=COL_BLOCK, n_tiles=B_WIDTH // COL_BLOCK, src_block=lambda j: blk[5] + j, name="proj_b_v")
    qi = proj(kind="rope64", extras=rope64, out_dtype=F32, tn=COL_BLOCK, n_tiles=IDX_WIDTH // COL_BLOCK,
              src_block=lambda j: blk[6] + j, name="proj_qi")
    kiwi = proj(kind="kiwi", extras=rope64, out_dtype=F32, tn=LANES, n_tiles=1,
                src_block=lambda j: blk[7] * (COL_BLOCK // LANES), name="proj_kiwi")
    yb = _dsa(qi, kiwi, qk_b, v_b)
    return ya, yb


def kernel(x, mem, w_in, w_pa, w_pb, w_o, ln1_g, ln1_b, w_mq, w_mkv, w_mo, ln2_g, ln2_b, w_up, w_down, ln3_g, ln3_b):
    bsz, t, d = x.shape
    depth = w_in.shape[0]
    assert bsz == 1
    alpha = (2 * depth) ** 0.25

    w_in_t = jnp.swapaxes(w_in, 1, 2)
    sizes = (A_WIDTH, A_WIDTH, A_WIDTH, B_WIDTH, B_WIDTH, B_WIDTH, IDX_WIDTH, IDX_DIM, IDX_HEADS, d, d)
    offs = [0]
    for s in sizes:
        offs.append(offs[-1] + s)
    assert offs[-1] == w_in.shape[2] and all(o % COL_BLOCK == 0 for o in offs[:8])
    blk = [o // COL_BLOCK for o in offs[:8]]
    gate_shift = offs[9] - offs[7]
    assert 0 < gate_shift < LANES and (2 * d) % COL_BLOCK == 0
    lnp = [p.reshape(depth, 1, d) for p in (ln1_g, ln1_b, ln2_g, ln2_b, ln3_g, ln3_b)]

    (cos128, sin128), (cos64, sin_lo, sin_hi) = _rope_tables(t)
    tm_rows = _pick(t, (1024, 512, 256))

    xf = x[0]
    xb = _to_bf16(xf)
    memb = _to_bf16(mem[0])

    def row_tab(arr):
        return (arr, (tm_rows, LANES), lambda i, j: (i, 0))

    def tile(arr, tn, joff=0):
        return (arr, (tm_rows, tn), lambda i, j: (i, j + joff))

    rope128 = (row_tab(cos128), row_tab(sin128))
    rope64 = (row_tab(cos64), row_tab(sin_lo), row_tab(sin_hi))
    tn_d = d if d <= 1280 else _pick(d, (1024, 512, 256, 128))
    tn_p = _pick(d, (COL_BLOCK, 256, 128))

    for l in range(depth):
        ya, yb = _mixer_branches(xb, w_in_t, l, blk, rope128, rope64)
        gates = _panel_matmul(xb, w_in_t, l, kind="sigmoid", out_dtype=F32, tn=COL_BLOCK, trans_b=True,
                              n_tiles=2 * d // COL_BLOCK, src_block=lambda j: blk[7] + j, shift=gate_shift,
                              name="proj_gate")
        pa = _matmul(ya, w_pa, l, kind="gate", out_dtype=F32, extras=(tile(gates, tn_d),), name="proj_a")
        merged = _matmul(yb, w_pb, l, kind="gate_add",
                         extras=(tile(pa, tn_d), tile(gates, tn_d, d // tn_d)), name="proj_b_merge")
        y = _panel_matmul(merged, w_o, l, out_dtype=F32, tn=tn_p, n_tiles=d // tn_p, src_block=lambda j: j,
                          name="mixer_out")
        xf, xb = _deepnorm_ln(xf, y, lnp[0], lnp[1], l, alpha)
        qm = _panel_matmul(xb, w_mq, l, tn=MEM_WIDTH, n_tiles=1, src_block=lambda j: j, name="mem_q")
        kvm = _matmul(memb, w_mkv, l, name="mem_kv")
        om = _mem_attention(qm, kvm)
        y = _matmul(om, w_mo, l, out_dtype=F32, name="mem_out")
        xf, xb = _deepnorm_ln(xf, y, lnp[2], lnp[3], l, alpha)
        hdn = _panel_matmul(xb, w_up, l, kind="relu2", tn=tn_p, n_tiles=w_up.shape[2] // tn_p,
                            src_block=lambda j: j, name="mlp_up")
        y = _matmul(hdn, w_down, l, out_dtype=F32, name="mlp_down")
        xf, xb = _deepnorm_ln(xf, y, lnp[4], lnp[5], l, alpha)
    return xf[None]
```

```python
import functools

import jax
import jax.numpy as jnp
from jax import lax
from jax.experimental import pallas as pl
from jax.experimental.pallas import tpu as pltpu

F32 = jnp.float32
BF16 = jnp.bfloat16

HEAD_DIM = 128
DIL_GROUPS = ((128, 1), (512, 4), (2048, 16))
A_HEADS_PER_GROUP = 4
A_GROUPS = len(DIL_GROUPS)
A_WIDTH = A_GROUPS * A_HEADS_PER_GROUP * HEAD_DIM
A_OUT = A_HEADS_PER_GROUP * HEAD_DIM
B_HEADS = 8
B_WIDTH = B_HEADS * HEAD_DIM
IDX_HEADS = 16
IDX_DIM = 64
IDX_WIDTH = IDX_HEADS * IDX_DIM
DSA_TOPK = 256
MEM_HEADS = 4
MEM_WIDTH = MEM_HEADS * HEAD_DIM
ROPE_THETA = 10000.0
LN_EPS = 1e-5
NEG_INF = -1e30
M_FLOOR = -1e29
BAND = 128
LOG2E = 1.4426950408889634

LANES = 128
MXU_COLS = 256
VMEM_LIMIT = 56 * 1024 * 1024
B_TILE_BYTES = 8 * 1024 * 1024
COL_BLOCK = 512

INT_MIN = -(2 ** 31)
INT_MAX = 2 ** 31 - 1
NT_DIMS = (((1,), (1,)), ((), ()))


def _pick(n, prefs):
    for p in prefs:
        if n % p == 0:
            return p
    return n


def _params(sem):
    return pltpu.CompilerParams(dimension_semantics=sem, vmem_limit_bytes=VMEM_LIMIT)


def _rope128(x, cos, sin):
    return x * cos + pltpu.roll(x, 64, 1) * sin


def _rope64(x, cos, sin_lo, sin_hi):
    return x * cos + pltpu.roll(x, 96, 1) * sin_lo + pltpu.roll(x, 32, 1) * sin_hi


def _rope128_tile(acc, cos, sin):
    return [_rope128(acc[:, c * LANES:(c + 1) * LANES], cos, sin) for c in range(acc.shape[1] // LANES)]


def _epilogue(kind, acc, extras, o_ref, stage_ref, dil, j, alpha):
    tn = acc.shape[1]
    if kind == "none":
        o_ref[...] = acc.astype(o_ref.dtype)
    elif kind == "resid":
        o_ref[...] = (alpha * extras[0][...] + acc).astype(o_ref.dtype)
    elif kind == "rope128":
        for c, y in enumerate(_rope128_tile(acc, extras[0][...], extras[1][...])):
            o_ref[:, c * LANES:(c + 1) * LANES] = y.astype(o_ref.dtype)
    elif kind == "rope128_qk":
        qs = jnp.where(j < B_WIDTH // COL_BLOCK, HEAD_DIM ** -0.5 * LOG2E, 1.0)
        for c, y in enumerate(_rope128_tile(acc, extras[0][...], extras[1][...])):
            o_ref[:, c * LANES:(c + 1) * LANES] = (y * qs).astype(o_ref.dtype)
    elif kind == "rope64":
        cos, s_lo, s_hi = extras[0][...], extras[1][...], extras[2][...]
        for c in range(tn // LANES):
            sl = slice(c * LANES, (c + 1) * LANES)
            o_ref[:, sl] = _rope64(acc[:, sl], cos, s_lo, s_hi).astype(o_ref.dtype)
    elif kind == "kiwi":
        cos, s_lo, s_hi = extras[0][...], extras[1][...], extras[2][...]
        o_ref[:, :LANES] = _rope64(acc[:, :LANES], cos, s_lo, s_hi).astype(o_ref.dtype)
        o_ref[:, LANES:] = acc[:, LANES:].astype(o_ref.dtype)
    elif kind == "a_group":
        @pl.when(j < 2)
        def _():
            for c, y in enumerate(_rope128_tile(acc, extras[0][...], extras[1][...])):
                stage_ref[c] = y

        @pl.when(j == 2)
        def _():
            for c in range(tn // LANES):
                stage_ref[c] = acc[:, c * LANES:(c + 1) * LANES]

        rows = acc.shape[0] // dil
        for r in range(dil):
            for c in range(tn // LANES):
                src = stage_ref[c] if dil == 1 else stage_ref[c, pl.ds(r, rows, stride=dil), :]
                o_ref[r, :, c * LANES:(c + 1) * LANES] = src.astype(o_ref.dtype)
    elif kind == "sigmoid":
        o_ref[...] = (1.0 / (1.0 + jnp.exp(-acc))).astype(o_ref.dtype)
    elif kind == "relu2":
        r = jnp.maximum(acc, 0.0)
        o_ref[...] = (r * r).astype(o_ref.dtype)
    else:
        raise ValueError(kind)


def _mm_kernel(*refs, nk, kind, n_extra, dil, alpha):
    a_ref, b_ref = refs[0], refs[1]
    extras = refs[2:2 + n_extra]
    o_ref = refs[2 + n_extra]
    scratch = list(refs[3 + n_extra:])
    acc_ref = scratch.pop(0) if nk > 1 else None
    stage_ref = scratch.pop(0) if kind == "a_group" else None
    j = pl.program_id(1)
    k = pl.program_id(2)
    part = jnp.dot(a_ref[...], b_ref[...].astype(BF16), preferred_element_type=F32)
    if nk == 1:
        _epilogue(kind, part, extras, o_ref, stage_ref, dil, j, alpha)
        return

    @pl.when(k == 0)
    def _():
        acc_ref[...] = part

    @pl.when(k > 0)
    def _():
        acc_ref[...] += part

    @pl.when(k == nk - 1)
    def _():
        _epilogue(kind, acc_ref[...], extras, o_ref, stage_ref, dil, j, alpha)


def _matmul(a, b, layer, *, kind="none", extras=(), out_dtype=BF16, name="mm",
            tn=None, n_tiles=None, src_block=None, dil=1, alpha=None):
    m, k = a.shape
    nb = b.shape[2]
    tm = _pick(m, (1024, 512, 256))
    if tn is None:
        tn = nb if nb <= 1280 else _pick(nb, (1024, 512, 256, 128))
    if n_tiles is None:
        n_tiles = nb // tn
    if src_block is None:
        src_block = lambda j: j
    itemsize = jnp.dtype(b.dtype).itemsize
    tk = k
    for cand in (k, 4096, 2048, 1024, 512):
        if cand <= k and k % cand == 0:
            tk = cand
            if cand * tn * itemsize <= B_TILE_BYTES:
                break
    nk = k // tk
    assert m % tm == 0 and k % tk == 0 and tm % dil == 0
    in_specs = [
        pl.BlockSpec((tm, tk), lambda i, j, kk: (i, kk)),
        pl.BlockSpec((None, tk, tn), lambda i, j, kk: (layer, kk, src_block(j))),
    ]
    ops = [a, b]
    for arr, bshape, imap in extras:
        in_specs.append(pl.BlockSpec(bshape, functools.partial(lambda i, j, kk, f: f(i, j), f=imap)))
        ops.append(arr)
    scratch = [pltpu.VMEM((tm, tn), F32)] if nk > 1 else []
    if kind == "a_group":
        scratch.append(pltpu.VMEM((tn // LANES, tm, LANES), F32))
        out_shape = jax.ShapeDtypeStruct((n_tiles, dil, m // dil, tn), out_dtype)
        out_spec = pl.BlockSpec((None, dil, tm // dil, tn), lambda i, j, kk: (j, 0, i, 0))
    else:
        out_shape = jax.ShapeDtypeStruct((m, n_tiles * tn), out_dtype)
        out_spec = pl.BlockSpec((tm, tn), lambda i, j, kk: (i, j))
    return pl.pallas_call(
        functools.partial(_mm_kernel, nk=nk, kind=kind, n_extra=len(extras), dil=dil, alpha=alpha),
        grid=(m // tm, n_tiles, nk),
        in_specs=in_specs,
        out_specs=out_spec,
        out_shape=out_shape,
        scratch_shapes=scratch,
        compiler_params=_params(("parallel", "parallel", "arbitrary")),
        name=name,
    )(*ops)


def _panel_kernel(*refs, kind, n_extra, dil, shift, trans_b, tn, alpha):
    a_ref, b_ref = refs[0], refs[1]
    n_b = 2 if shift else 1
    extras = refs[1 + n_b:1 + n_b + n_extra]
    o_ref = refs[1 + n_b + n_extra]
    scratch = list(refs[2 + n_b + n_extra:])
    bb_ref = scratch.pop(0)
    stage_ref = scratch.pop(0) if kind == "a_group" else None
    j = pl.program_id(0)
    i = pl.program_id(1)

    @pl.when(i == 0)
    def _():
        w = b_ref[...].astype(BF16)
        if kind == "kiwi":
            q = IDX_DIM
            bb_ref[0:q, :] = w[0:q]
            bb_ref[q:2 * q, :] = w[0:q]
            bb_ref[2 * q:3 * q, :] = w[q:2 * q]
            bb_ref[3 * q:4 * q, :] = w[q:2 * q]
        elif trans_b:
            bb_ref[:b_ref.shape[0], :] = w
            if shift:
                bb_ref[b_ref.shape[0]:, :] = refs[2][...].astype(BF16)
        else:
            bb_ref[...] = w

    if trans_b:
        acc = lax.dot_general(a_ref[...], bb_ref[shift:shift + tn, :], NT_DIMS, preferred_element_type=F32)
    else:
        acc = jnp.dot(a_ref[...], bb_ref[...], preferred_element_type=F32)
    _epilogue(kind, acc, extras, o_ref, stage_ref, dil, j, alpha)


def _panel_matmul(a, b, layer, *, tn, n_tiles, src_block, kind="none", extras=(), out_dtype=BF16,
                  name="mm", dil=1, shift=0, trans_b=False, alpha=None):
    m, k = a.shape
    tm = _pick(m, (1024, 512, 256))
    assert m % tm == 0 and tm % dil == 0 and tn % LANES == 0
    assert shift == 0 or (trans_b and shift % 16 == 0 and shift < LANES)
    out_tn = 2 * tn if kind == "kiwi" else tn
    b_rows = tn + (LANES if shift else 0)
    in_specs = [pl.BlockSpec((tm, k), lambda j, i: (i, 0))]
    ops = [a, b]
    if trans_b:
        in_specs.append(pl.BlockSpec((None, tn, k), lambda j, i: (layer, src_block(j), 0)))
        if shift:
            per = tn // LANES
            in_specs.append(pl.BlockSpec((None, LANES, k), lambda j, i: (layer, (src_block(j) + 1) * per, 0)))
            ops.append(b)
        scratch = [pltpu.VMEM((out_tn if kind == "kiwi" else b_rows, k), BF16)]
    else:
        in_specs.append(pl.BlockSpec((None, k, tn), lambda j, i: (layer, 0, src_block(j))))
        scratch = [pltpu.VMEM((k, tn), BF16)]
    for arr, bshape, imap in extras:
        in_specs.append(pl.BlockSpec(bshape, functools.partial(lambda j, i, f: f(i, j), f=imap)))
        ops.append(arr)
    if kind == "a_group":
        scratch.append(pltpu.VMEM((tn // LANES, tm, LANES), F32))
        out_shape = jax.ShapeDtypeStruct((n_tiles, dil, m // dil, tn), out_dtype)
        out_spec = pl.BlockSpec((None, dil, tm // dil, tn), lambda j, i: (j, 0, i, 0))
    else:
        out_shape = jax.ShapeDtypeStruct((m, n_tiles * out_tn), out_dtype)
        out_spec = pl.BlockSpec((tm, out_tn), lambda j, i: (i, j))
    return pl.pallas_call(
        functools.partial(_panel_kernel, kind=kind, n_extra=len(extras), dil=dil, shift=shift,
                          trans_b=trans_b, tn=out_tn, alpha=alpha),
        grid=(n_tiles, m // tm),
        in_specs=in_specs,
        out_specs=out_spec,
        out_shape=out_shape,
        scratch_shapes=scratch,
        compiler_params=_params(("arbitrary", "arbitrary")),
        name=name,
    )(*ops)


def _ln_kernel(z_ref, g_ref, b_ref, o_ref, ob_ref):
    z = z_ref[...]
    mu = jnp.mean(z, axis=-1, keepdims=True)
    zc = z - mu
    var = jnp.mean(zc * zc, axis=-1, keepdims=True)
    out = (zc * lax.rsqrt(var + LN_EPS)) * g_ref[...] + b_ref[...]
    o_ref[...] = out
    ob_ref[...] = out.astype(BF16)


def _layer_norm(z, g, b, layer):
    t, d = z.shape
    tm = _pick(t, (128,))
    row = pl.BlockSpec((tm, d), lambda i: (i, 0))
    par = pl.BlockSpec((None, 1, d), lambda i: (layer, 0, 0))
    return pl.pallas_call(
        _ln_kernel,
        grid=(t // tm,),
        in_specs=[row, par, par],
        out_specs=[row, row],
        out_shape=[jax.ShapeDtypeStruct((t, d), F32), jax.ShapeDtypeStruct((t, d), BF16)],
        compiler_params=_params(("parallel",)),
        name="layer_norm",
    )(z, g, b)


def _merge_kernel(ya_ref, wa_ref, yb_ref, wb_ref, ga_ref, gb_ref, o_ref):
    pa = jnp.dot(ya_ref[...], wa_ref[...].astype(BF16), preferred_element_type=F32)
    pb = jnp.dot(yb_ref[...], wb_ref[...].astype(BF16), preferred_element_type=F32)
    o_ref[...] = (ga_ref[...].astype(F32) * pa + gb_ref[...].astype(F32) * pb).astype(o_ref.dtype)


def _gated_merge(ya, yb, w_pa, w_pb, gates, layer):
    t = ya.shape[0]
    d = w_pa.shape[2]
    tm = _pick(t, (1024, 512, 256))
    tn = d if d <= 1280 else _pick(d, (1024, 512, 256, 128))
    ka, kb = ya.shape[1], yb.shape[1]
    return pl.pallas_call(
        _merge_kernel,
        grid=(t // tm, d // tn),
        in_specs=[
            pl.BlockSpec((tm, ka), lambda i, j: (i, 0)),
            pl.BlockSpec((None, ka, tn), lambda i, j: (layer, 0, j)),
            pl.BlockSpec((tm, kb), lambda i, j: (i, 0)),
            pl.BlockSpec((None, kb, tn), lambda i, j: (layer, 0, j)),
            pl.BlockSpec((tm, tn), lambda i, j: (i, j)),
            pl.BlockSpec((tm, tn), lambda i, j: (i, j + d // tn)),
        ],
        out_specs=pl.BlockSpec((tm, tn), lambda i, j: (i, j)),
        out_shape=jax.ShapeDtypeStruct((t, d), BF16),
        compiler_params=_params(("parallel", "parallel")),
        name="gated_merge",
    )(ya, w_pa, yb, w_pb, gates, gates)


def _dil_kernel(q_ref, kp_ref, kc_ref, vp_ref, vc_ref, o_ref, lse_ref):
    i = pl.program_id(1)
    blk = q_ref.shape[0]
    qi = lax.broadcasted_iota(jnp.int32, (blk, blk), 0)
    ki = lax.broadcasted_iota(jnp.int32, (blk, blk), 1)
    ok_prev = ki >= qi + jnp.where(i > 0, 0, blk)
    ok_cur = ki <= qi
    scale = HEAD_DIM ** -0.5
    for h in range(A_HEADS_PER_GROUP):
        sl = slice(h * HEAD_DIM, (h + 1) * HEAD_DIM)
        q = q_ref[:, sl]
        sp = lax.dot_general(q, kp_ref[:, sl], NT_DIMS, preferred_element_type=F32) * scale
        sc = lax.dot_general(q, kc_ref[:, sl], NT_DIMS, preferred_element_type=F32) * scale
        sp = jnp.where(ok_prev, sp, NEG_INF)
        sc = jnp.where(ok_cur, sc, NEG_INF)
        m = jnp.maximum(jnp.max(sp, axis=-1, keepdims=True), jnp.max(sc, axis=-1, keepdims=True))
        ep = jnp.exp(sp - m)
        ec = jnp.exp(sc - m)
        l = jnp.sum(ep, axis=-1, keepdims=True) + jnp.sum(ec, axis=-1, keepdims=True)
        lse = m + jnp.log(l)
        inv = 1.0 / l
        pp = (ep * inv).astype(BF16)
        pc = (ec * inv).astype(BF16)
        o = (jnp.dot(pp, vp_ref[:, sl], preferred_element_type=F32)
             + jnp.dot(pc, vc_ref[:, sl], preferred_element_type=F32))
        o_ref[:, sl] = o
        lse_ref[:, sl] = jnp.broadcast_to(lse, (blk, HEAD_DIM))


def _dilated_group(qkv, dil):
    sub = qkv.shape[2]
    assert sub % BAND == 0
    nb = sub // BAND
    prev = lambda i: jnp.maximum(i - 1, 0)

    def spec(which, row):
        return pl.BlockSpec((None, None, BAND, A_OUT), lambda r, i: (which, r, row(i), 0))

    cur = lambda i: i
    out_spec = pl.BlockSpec((None, BAND, A_OUT), lambda r, i: (r, i, 0))
    return pl.pallas_call(
        _dil_kernel,
        grid=(dil, nb),
        in_specs=[spec(0, cur), spec(1, prev), spec(1, cur), spec(2, prev), spec(2, cur)],
        out_specs=[out_spec, out_spec],
        out_shape=[jax.ShapeDtypeStruct((dil, sub, A_OUT), F32)] * 2,
        compiler_params=_params(("parallel", "parallel")),
        name=f"dilated_attn_d{dil}",
    )(qkv, qkv, qkv, qkv, qkv)


def _mix_kernel(*refs, dils):
    n = len(dils)
    o_refs, l_refs, y_ref = refs[:n], refs[n:2 * n], refs[2 * n]
    stage = refs[2 * n + 1:]
    tm = y_ref.shape[0]

    def natural(ref, slot, dil, cs):
        if dil == 1:
            return ref[0, :, cs]
        for r in range(dil):
            slot[pl.ds(r, tm // dil, stride=dil), :] = ref[r, :, cs]
        return slot[...]

    for c in range(A_OUT // LANES):
        cs = slice(c * LANES, (c + 1) * LANES)
        outs = [natural(o_refs[g], stage[2 * g], dils[g], cs) for g in range(n)]
        lses = [natural(l_refs[g], stage[2 * g + 1], dils[g], cs) for g in range(n)]
        m = functools.reduce(jnp.maximum, lses)
        es = [jnp.exp(l - m) for l in lses]
        inv = 1.0 / functools.reduce(lambda a, b: a + b, es)
        y = functools.reduce(lambda a, b: a + b, [(e * inv) * o for e, o in zip(es, outs)])
        y_ref[:, cs] = y.astype(y_ref.dtype)


def _mix_groups(outs, lses, dils, t):
    tm = _pick(t, (512, 256))
    specs = [pl.BlockSpec((d, tm // d, A_OUT), lambda i: (0, i, 0)) for d in dils]
    return pl.pallas_call(
        functools.partial(_mix_kernel, dils=tuple(dils)),
        grid=(t // tm,),
        in_specs=specs + specs,
        out_specs=pl.BlockSpec((tm, A_OUT), lambda i: (i, 0)),
        out_shape=jax.ShapeDtypeStruct((t, A_OUT), BF16),
        scratch_shapes=[pltpu.VMEM((tm, LANES), F32)] * (2 * len(dils)),
        compiler_params=_params(("parallel",)),
        name="mix_groups",
    )(*outs, *lses)


def _sortable(x):
    b = pltpu.bitcast(x, jnp.int32)
    return b ^ ((b >> 31) & jnp.int32(INT_MAX))


def _dsa_kernel(qi_ref, ki_ref, wi_ref, q_ref, k_ref, v_ref, o_ref,
                key_ref, bias_ref, qs_ref, wb_ref, thr_ref, mthr_ref, m_ref, acc_ref,
                *, tq, tkc, topk):
    i = pl.program_id(0)
    c = pl.program_id(1)
    last_c = ((i + 1) * tq - 1) // tkc
    n_ch = last_c + 1
    row_pos = i * tq + lax.broadcasted_iota(jnp.int32, (tq, tkc), 0)
    col_iota = lax.broadcasted_iota(jnp.int32, (tq, tkc), 1)
    sub = MXU_COLS
    row_sub = i * tq + lax.broadcasted_iota(jnp.int32, (tq, sub), 0)
    col_sub = lax.broadcasted_iota(jnp.int32, (tq, sub), 1)

    @pl.when(c == 0)
    def _select():
        lo_half = lax.broadcasted_iota(jnp.int32, (tq, LANES), 1) < IDX_DIM
        for hp in range(IDX_HEADS // 2):
            pair = qi_ref[:, hp * LANES:(hp + 1) * LANES]
            qs_ref[(2 * hp) * tq:(2 * hp + 1) * tq, :] = jnp.where(lo_half, pair, 0.0).astype(BF16)
            qs_ref[(2 * hp + 1) * tq:(2 * hp + 2) * tq, :] = jnp.where(lo_half, 0.0, pair).astype(BF16)
        w = wi_ref[...] * (IDX_DIM ** -0.5 * IDX_HEADS ** -0.5)
        for h in range(IDX_HEADS):
            wb_ref[h] = w[:, h:h + 1] + jnp.zeros((tq, LANES), F32)

        def score_chunk(kc, carry):
            off = pl.multiple_of(kc * tkc, tkc)
            for s in range(tkc // sub):
                kis = ki_ref[pl.ds(off + s * sub, sub), :].astype(BF16)
                r = lax.dot_general(qs_ref[...], kis, NT_DIMS, preferred_element_type=F32)
                halves = []
                for half in range(sub // LANES):
                    cs = slice(half * LANES, (half + 1) * LANES)
                    sc = jnp.zeros((tq, LANES), F32)
                    for h in range(IDX_HEADS):
                        sc = sc + wb_ref[h] * jnp.maximum(r[h * tq:(h + 1) * tq, cs], 0.0)
                    halves.append(sc)
                sc = jnp.concatenate(halves, axis=1)
                cols = slice(s * sub, (s + 1) * sub)
                sc = jnp.where(off + s * sub + col_sub <= row_sub, sc, NEG_INF)
                key_ref[kc, :, cols] = _sortable(sc)
            return carry

        lax.fori_loop(0, n_ch, score_chunk, 0)

        def count(pred):
            def body(kc, acc):
                off = pl.multiple_of(kc * tkc, tkc)
                part = jnp.where(pred(key_ref[kc], off), 1, 0).astype(jnp.int32)
                for s in range(tkc // LANES):
                    acc = acc + part[:, s * LANES:(s + 1) * LANES]
                return acc
            acc = lax.fori_loop(0, n_ch, body, jnp.zeros((tq, LANES), jnp.int32))
            return jnp.sum(acc, axis=1, keepdims=True)

        def unsettled(state):
            it, lo, hi, n_lo = state
            open_rows = jnp.where((n_lo == topk) | (lo == hi), 0, 1)
            return jnp.logical_and(it < 33, jnp.max(open_rows) > 0)

        def bisect(state):
            it, lo, hi, n_lo = state
            mid = (lo | hi) - ((lo ^ hi) >> 1)
            n_mid = count(lambda kk, off: kk >= mid)
            ok = n_mid >= topk
            return (it + 1, jnp.where(ok, mid, lo), jnp.where(ok, hi, mid - 1), jnp.where(ok, n_mid, n_lo))

        lo0 = jnp.full((tq, 1), INT_MIN, jnp.int32)
        hi0 = jnp.full((tq, 1), INT_MAX, jnp.int32)
        n0 = jnp.full((tq, 1), INT_MAX, jnp.int32)
        _, thr, _, n_ge = lax.while_loop(unsettled, bisect, (jnp.int32(0), lo0, hi0, n0))
        thr_ref[...] = thr
        mthr_ref[...] = jnp.full((tq, 1), INT_MAX, jnp.int32)

        @pl.when(jnp.max(n_ge) > topk)
        def _ties():
            need = topk - count(lambda kk, off: kk > thr)

            def bisect_idx(_, lohi):
                lo, hi = lohi
                mid = (lo + hi) >> 1
                ok = count(lambda kk, off: (kk == thr) & (off + col_iota <= mid)) >= need
                return jnp.where(ok, lo, mid + 1), jnp.where(ok, mid, hi)

            lo1 = jnp.zeros((tq, 1), jnp.int32)
            hi1 = jnp.full((tq, 1), 2 ** 30, jnp.int32)
            _, m_idx = lax.fori_loop(0, 31, bisect_idx, (lo1, hi1))
            mthr_ref[...] = m_idx

        m_ref[...] = jnp.full(m_ref.shape, M_FLOOR, F32)
        acc_ref[...] = jnp.zeros(acc_ref.shape, F32)

    @pl.when(c <= last_c)
    def _attend():
        off = pl.multiple_of(c * tkc, tkc)
        kk = key_ref[c]
        thr = thr_ref[...]
        pos = off + col_iota
        chosen = (kk > thr) | ((kk == thr) & (pos <= mthr_ref[...]))
        bias_ref[...] = jnp.where(chosen & (pos <= row_pos), 0.0, NEG_INF)
        ones = jnp.ones((tkc, HEAD_DIM), BF16)
        for h in range(B_HEADS):
            sl = slice(h * HEAD_DIM, (h + 1) * HEAD_DIM)
            s = lax.dot_general(q_ref[:, sl], k_ref[:, sl], NT_DIMS, preferred_element_type=F32) + bias_ref[...]
            m_old = m_ref[h]
            m_new = jnp.maximum(m_old, jnp.max(s, axis=-1, keepdims=True))
            p = jnp.exp2(s - m_new).astype(BF16)
            alpha = jnp.exp2(m_old - m_new)
            v_one = jnp.concatenate([v_ref[:, sl], ones], axis=1)
            acc_ref[h] = alpha * acc_ref[h] + jnp.dot(p, v_one, preferred_element_type=F32)
            m_ref[h] = m_new

    @pl.when(c == last_c)
    def _finish():
        for h in range(B_HEADS):
            a = acc_ref[h]
            o_ref[:, h * HEAD_DIM:(h + 1) * HEAD_DIM] = (a[:, :HEAD_DIM] / a[:, HEAD_DIM:]).astype(o_ref.dtype)


def _dsa(qi, kiwi, qk, v):
    t = qi.shape[0]
    tq = _pick(t, (256, 128))
    tkc = _pick(t, (1024, 512, 256))
    topk = min(DSA_TOPK, t // 4)
    assert tkc >= topk and t % tq == 0 and tkc % MXU_COLS == 0
    nq, nc = t // tq, t // tkc

    def kchunk(i, c):
        return jnp.minimum(c, ((i + 1) * tq - 1) // tkc)

    in_specs = [
        pl.BlockSpec((tq, IDX_WIDTH), lambda i, c: (i, 0)),
        pl.BlockSpec((t, LANES), lambda i, c: (0, 0)),
        pl.BlockSpec((tq, LANES), lambda i, c: (i, 1)),
        pl.BlockSpec((tq, B_WIDTH), lambda i, c: (i, 0)),
        pl.BlockSpec((tkc, B_WIDTH), lambda i, c: (kchunk(i, c), 1)),
        pl.BlockSpec((tkc, B_WIDTH), lambda i, c: (kchunk(i, c), 0)),
    ]
    return pl.pallas_call(
        functools.partial(_dsa_kernel, tq=tq, tkc=tkc, topk=topk),
        grid=(nq, nc),
        in_specs=in_specs,
        out_specs=pl.BlockSpec((tq, B_WIDTH), lambda i, c: (i, 0)),
        out_shape=jax.ShapeDtypeStruct((t, B_WIDTH), BF16),
        scratch_shapes=[
            pltpu.VMEM((nc, tq, tkc), jnp.int32),
            pltpu.VMEM((tq, tkc), F32),
            pltpu.VMEM((IDX_HEADS * tq, LANES), BF16),
            pltpu.VMEM((IDX_HEADS, tq, LANES), F32),
            pltpu.VMEM((tq, 1), jnp.int32),
            pltpu.VMEM((tq, 1), jnp.int32),
            pltpu.VMEM((B_HEADS, tq, 1), F32),
            pltpu.VMEM((B_HEADS, tq, 2 * HEAD_DIM), F32),
        ],
        compiler_params=_params(("parallel", "arbitrary")),
        name="dsa",
    )(qi, kiwi, kiwi, qk, qk, v)


def _mem_kernel(q_ref, kv_ref, o_ref):
    scale = HEAD_DIM ** -0.5
    for h in range(MEM_HEADS):
        sl = slice(h * HEAD_DIM, (h + 1) * HEAD_DIM)
        vsl = slice(MEM_WIDTH + h * HEAD_DIM, MEM_WIDTH + (h + 1) * HEAD_DIM)
        s = lax.dot_general(q_ref[:, sl], kv_ref[:, sl], NT_DIMS, preferred_element_type=F32) * scale
        m = jnp.max(s, axis=-1, keepdims=True)
        e = jnp.exp(s - m)
        p = e / jnp.sum(e, axis=-1, keepdims=True)
        o_ref[:, sl] = jnp.dot(p.astype(BF16), kv_ref[:, vsl], preferred_element_type=F32).astype(o_ref.dtype)


def _mem_attention(q, kv):
    t = q.shape[0]
    mlen = kv.shape[0]
    tq = _pick(t, (512, 256, 128))
    return pl.pallas_call(
        _mem_kernel,
        grid=(t // tq,),
        in_specs=[pl.BlockSpec((tq, MEM_WIDTH), lambda i: (i, 0)),
                  pl.BlockSpec((mlen, 2 * MEM_WIDTH), lambda i: (0, 0))],
        out_specs=pl.BlockSpec((tq, MEM_WIDTH), lambda i: (i, 0)),
        out_shape=jax.ShapeDtypeStruct((t, MEM_WIDTH), BF16),
        compiler_params=_params(("parallel",)),
        name="mem_attention",
    )(q, kv)


def _cast_kernel(x_ref, o_ref):
    o_ref[...] = x_ref[...].astype(o_ref.dtype)


def _to_bf16(x):
    t, d = x.shape
    tm = _pick(t, (256, 128))
    spec = pl.BlockSpec((tm, d), lambda i: (i, 0))
    return pl.pallas_call(
        _cast_kernel, grid=(t // tm,), in_specs=[spec], out_specs=spec,
        out_shape=jax.ShapeDtypeStruct((t, d), BF16),
        compiler_params=_params(("parallel",)), name="cast_bf16",
    )(x)


def _rope_tables(t):
    pos = jnp.arange(t, dtype=jnp.int32).astype(F32)[:, None]
    half = HEAD_DIM // 2
    inv = ROPE_THETA ** (-jnp.arange(half, dtype=F32) / half)
    ang = pos * inv[None, :]
    cos128 = jnp.concatenate([jnp.cos(ang)] * 2, axis=1)
    sin128 = jnp.concatenate([-jnp.sin(ang), jnp.sin(ang)], axis=1)
    half = IDX_DIM // 2
    inv = ROPE_THETA ** (-jnp.arange(half, dtype=F32) / half)
    ang = pos * inv[None, :]
    z = jnp.zeros_like(ang)
    cos64 = jnp.concatenate([jnp.cos(ang)] * 4, axis=1)
    sin_lo = jnp.concatenate([-jnp.sin(ang), z] * 2, axis=1)
    sin_hi = jnp.concatenate([z, jnp.sin(ang)] * 2, axis=1)
    return (cos128, sin128), (cos64, sin_lo, sin_hi)


def _mixer_branches(xb, w_in_t, l, blk, rope128, rope64):
    t = xb.shape[0]
    proj = functools.partial(_panel_matmul, xb, w_in_t, l, trans_b=True)
    outs, lses, dils = [], [], []
    for g, (win, dil) in enumerate(DIL_GROUPS):
        assert win // dil == BAND
        qkv = proj(kind="a_group", extras=rope128, tn=COL_BLOCK, n_tiles=3, dil=dil,
                   src_block=lambda j, g=g: blk[0] + g + j * (A_WIDTH // COL_BLOCK), name=f"proj_a_g{g}")
        o, lse = _dilated_group(qkv, dil)
        outs.append(o)
        lses.append(lse)
        dils.append(dil)
    ya = _mix_groups(outs, lses, dils, t)
    qk_b = proj(kind="rope128_qk", extras=rope128, tn=COL_BLOCK, n_tiles=2 * B_WIDTH // COL_BLOCK,
                src_block=lambda j: blk[3] + j, name="proj_b_qk")
    v_b = proj(tn=COL_BLOCK, n_tiles=B_WIDTH // COL_BLOCK, src_block=lambda j: blk[5] + j, name="proj_b_v")
    qi = proj(kind="rope64", extras=rope64, out_dtype=F32, tn=COL_BLOCK, n_tiles=IDX_WIDTH // COL_BLOCK,
              src_block=lambda j: blk[6] + j, name="proj_qi")
    kiwi = proj(kind="kiwi", extras=rope64, out_dtype=F32, tn=LANES, n_tiles=1,
                src_block=lambda j: blk[7] * (COL_BLOCK // LANES), name="proj_kiwi")
    yb = _dsa(qi, kiwi, qk_b, v_b)
    return ya, yb


def kernel(x, mem, w_in, w_pa, w_pb, w_o, ln1_g, ln1_b, w_mq, w_mkv, w_mo, ln2_g, ln2_b, w_up, w_down, ln3_g, ln3_b):
    bsz, t, d = x.shape
    depth = w_in.shape[0]
    assert bsz == 1
    alpha = (2 * depth) ** 0.25

    w_in_t = jnp.swapaxes(w_in, 1, 2)
    sizes = (A_WIDTH, A_WIDTH, A_WIDTH, B_WIDTH, B_WIDTH, B_WIDTH, IDX_WIDTH, IDX_DIM, IDX_HEADS, d, d)
    offs = [0]
    for s in sizes:
        offs.append(offs[-1] + s)
    assert offs[-1] == w_in.shape[2] and all(o % COL_BLOCK == 0 for o in offs[:8])
    blk = [o // COL_BLOCK for o in offs[:8]]
    gate_shift = offs[9] - offs[7]
    assert 0 < gate_shift < LANES and (2 * d) % COL_BLOCK == 0
    lnp = [p.reshape(depth, 1, d) for p in (ln1_g, ln1_b, ln2_g, ln2_b, ln3_g, ln3_b)]

    (cos128, sin128), (cos64, sin_lo, sin_hi) = _rope_tables(t)
    tm_rows = _pick(t, (1024, 512, 256))

    xf = x[0]
    xb = _to_bf16(xf)
    memb = _to_bf16(mem[0])

    def row_tab(arr):
        return (arr, (tm_rows, LANES), lambda i, j: (i, 0))

    def tile(arr, tn, joff=0):
        return (arr, (tm_rows, tn), lambda i, j: (i, j + joff))

    rope128 = (row_tab(cos128), row_tab(sin128))
    rope64 = (row_tab(cos64), row_tab(sin_lo), row_tab(sin_hi))
    tn_d = d if d <= 1280 else _pick(d, (1024, 512, 256, 128))
    tn_p = _pick(d, (COL_BLOCK, 256, 128))

    for l in range(depth):
        ya, yb = _mixer_branches(xb, w_in_t, l, blk, rope128, rope64)
        gates = _panel_matmul(xb, w_in_t, l, kind="sigmoid", tn=COL_BLOCK, trans_b=True,
                              n_tiles=2 * d // COL_BLOCK, src_block=lambda j: blk[7] + j, shift=gate_shift,
                              name="proj_gate")
        merged = _gated_merge(ya, yb, w_pa, w_pb, gates, l)
        z = _panel_matmul(merged, w_o, l, kind="resid", extras=(tile(xf, tn_p),), alpha=alpha, out_dtype=F32,
                          tn=tn_p, n_tiles=d // tn_p, src_block=lambda j: j, name="mixer_out")
        xf, xb = _layer_norm(z, lnp[0], lnp[1], l)
        qm = _panel_matmul(xb, w_mq, l, tn=MEM_WIDTH, n_tiles=1, src_block=lambda j: j, name="mem_q")
        kvm = _matmul(memb, w_mkv, l, name="mem_kv")
        om = _mem_attention(qm, kvm)
        z = _matmul(om, w_mo, l, kind="resid", extras=(tile(xf, tn_d),), alpha=alpha, out_dtype=F32,
                    name="mem_out")
        xf, xb = _layer_norm(z, lnp[2], lnp[3], l)
        hdn = _panel_matmul(xb, w_up, l, kind="relu2", tn=tn_p, n_tiles=w_up.shape[2] // tn_p,
                            src_block=lambda j: j, name="mlp_up")
        z = _matmul(hdn, w_down, l, kind="resid", extras=(tile(xf, tn_d),), alpha=alpha, out_dtype=F32,
                    name="mlp_down")
        xf, xb = _layer_norm(z, lnp[4], lnp[5], l)
    return xf[None]
```

```python
import functools

import jax
import jax.numpy as jnp
from jax import lax
from jax.experimental import pallas as pl
from jax.experimental.pallas import tpu as pltpu

F32 = jnp.float32
BF16 = jnp.bfloat16

HEAD_DIM = 128
DIL_GROUPS = ((128, 1), (512, 4), (2048, 16))
A_HEADS_PER_GROUP = 4
A_GROUPS = len(DIL_GROUPS)
A_WIDTH = A_GROUPS * A_HEADS_PER_GROUP * HEAD_DIM
A_OUT = A_HEADS_PER_GROUP * HEAD_DIM
B_HEADS = 8
B_WIDTH = B_HEADS * HEAD_DIM
IDX_HEADS = 16
IDX_DIM = 64
IDX_WIDTH = IDX_HEADS * IDX_DIM
DSA_TOPK = 256
MEM_HEADS = 4
MEM_WIDTH = MEM_HEADS * HEAD_DIM
ROPE_THETA = 10000.0
LN_EPS = 1e-5
NEG_INF = -1e30
M_FLOOR = -1e29
BAND = 128
LOG2E = 1.4426950408889634

LANES = 128
MXU_COLS = 256
VMEM_LIMIT = 56 * 1024 * 1024
B_TILE_BYTES = 8 * 1024 * 1024
COL_BLOCK = 512

INT_MIN = -(2 ** 31)
INT_MAX = 2 ** 31 - 1
NT_DIMS = (((1,), (1,)), ((), ()))


def _pick(n, prefs):
    for p in prefs:
        if n % p == 0:
            return p
    return n


def _params(sem):
    return pltpu.CompilerParams(dimension_semantics=sem, vmem_limit_bytes=VMEM_LIMIT)


def _rope128(x, cos, sin):
    return x * cos + pltpu.roll(x, 64, 1) * sin


def _rope64(x, cos, sin_lo, sin_hi):
    return x * cos + pltpu.roll(x, 96, 1) * sin_lo + pltpu.roll(x, 32, 1) * sin_hi


def _rope128_tile(acc, cos, sin):
    return [_rope128(acc[:, c * LANES:(c + 1) * LANES], cos, sin) for c in range(acc.shape[1] // LANES)]


def _epilogue(kind, acc, extras, o_ref, stage_ref, dil, j, alpha):
    tn = acc.shape[1]
    if kind == "none":
        o_ref[...] = acc.astype(o_ref.dtype)
    elif kind == "resid":
        o_ref[...] = (alpha * extras[0][...] + acc).astype(o_ref.dtype)
    elif kind == "rope128":
        for c, y in enumerate(_rope128_tile(acc, extras[0][...], extras[1][...])):
            o_ref[:, c * LANES:(c + 1) * LANES] = y.astype(o_ref.dtype)
    elif kind == "rope128_qk":
        qs = jnp.where(j < B_WIDTH // COL_BLOCK, HEAD_DIM ** -0.5 * LOG2E, 1.0)
        for c, y in enumerate(_rope128_tile(acc, extras[0][...], extras[1][...])):
            o_ref[:, c * LANES:(c + 1) * LANES] = (y * qs).astype(o_ref.dtype)
    elif kind == "rope64":
        cos, s_lo, s_hi = extras[0][...], extras[1][...], extras[2][...]
        for c in range(tn // LANES):
            sl = slice(c * LANES, (c + 1) * LANES)
            o_ref[:, sl] = _rope64(acc[:, sl], cos, s_lo, s_hi).astype(o_ref.dtype)
    elif kind == "kiwi":
        cos, s_lo, s_hi = extras[0][...], extras[1][...], extras[2][...]
        o_ref[:, :LANES] = _rope64(acc[:, :LANES], cos, s_lo, s_hi).astype(o_ref.dtype)
        o_ref[:, LANES:] = acc[:, LANES:].astype(o_ref.dtype)
    elif kind == "a_group":
        @pl.when(j < 2)
        def _():
            for c, y in enumerate(_rope128_tile(acc, extras[0][...], extras[1][...])):
                stage_ref[c] = y

        @pl.when(j == 2)
        def _():
            for c in range(tn // LANES):
                stage_ref[c] = acc[:, c * LANES:(c + 1) * LANES]

        rows = acc.shape[0] // dil
        for r in range(dil):
            for c in range(tn // LANES):
                src = stage_ref[c] if dil == 1 else stage_ref[c, pl.ds(r, rows, stride=dil), :]
                o_ref[r, :, c * LANES:(c + 1) * LANES] = src.astype(o_ref.dtype)
    elif kind == "sigmoid":
        o_ref[...] = (1.0 / (1.0 + jnp.exp(-acc))).astype(o_ref.dtype)
    elif kind == "relu2":
        r = jnp.maximum(acc, 0.0)
        o_ref[...] = (r * r).astype(o_ref.dtype)
    else:
        raise ValueError(kind)


def _mm_kernel(*refs, nk, kind, n_extra, dil, alpha):
    a_ref, b_ref = refs[0], refs[1]
    extras = refs[2:2 + n_extra]
    o_ref = refs[2 + n_extra]
    scratch = list(refs[3 + n_extra:])
    acc_ref = scratch.pop(0) if nk > 1 else None
    stage_ref = scratch.pop(0) if kind == "a_group" else None
    j = pl.program_id(1)
    k = pl.program_id(2)
    part = jnp.dot(a_ref[...], b_ref[...].astype(BF16), preferred_element_type=F32)
    if nk == 1:
        _epilogue(kind, part, extras, o_ref, stage_ref, dil, j, alpha)
        return

    @pl.when(k == 0)
    def _():
        acc_ref[...] = part

    @pl.when(k > 0)
    def _():
        acc_ref[...] += part

    @pl.when(k == nk - 1)
    def _():
        _epilogue(kind, acc_ref[...], extras, o_ref, stage_ref, dil, j, alpha)


def _matmul(a, b, layer, *, kind="none", extras=(), out_dtype=BF16, name="mm",
            tn=None, n_tiles=None, src_block=None, dil=1, alpha=None):
    m, k = a.shape
    nb = b.shape[2]
    tm = _pick(m, (1024, 512, 256))
    if tn is None:
        tn = nb if nb <= 1280 else _pick(nb, (1024, 512, 256, 128))
    if n_tiles is None:
        n_tiles = nb // tn
    if src_block is None:
        src_block = lambda j: j
    itemsize = jnp.dtype(b.dtype).itemsize
    tk = k
    for cand in (k, 4096, 2048, 1024, 512):
        if cand <= k and k % cand == 0:
            tk = cand
            if cand * tn * itemsize <= B_TILE_BYTES:
                break
    nk = k // tk
    assert m % tm == 0 and k % tk == 0 and tm % dil == 0
    in_specs = [
        pl.BlockSpec((tm, tk), lambda i, j, kk: (i, kk)),
        pl.BlockSpec((None, tk, tn), lambda i, j, kk: (layer, kk, src_block(j))),
    ]
    ops = [a, b]
    for arr, bshape, imap in extras:
        in_specs.append(pl.BlockSpec(bshape, functools.partial(lambda i, j, kk, f: f(i, j), f=imap)))
        ops.append(arr)
    scratch = [pltpu.VMEM((tm, tn), F32)] if nk > 1 else []
    if kind == "a_group":
        scratch.append(pltpu.VMEM((tn // LANES, tm, LANES), F32))
        out_shape = jax.ShapeDtypeStruct((n_tiles, dil, m // dil, tn), out_dtype)
        out_spec = pl.BlockSpec((None, dil, tm // dil, tn), lambda i, j, kk: (j, 0, i, 0))
    else:
        out_shape = jax.ShapeDtypeStruct((m, n_tiles * tn), out_dtype)
        out_spec = pl.BlockSpec((tm, tn), lambda i, j, kk: (i, j))
    return pl.pallas_call(
        functools.partial(_mm_kernel, nk=nk, kind=kind, n_extra=len(extras), dil=dil, alpha=alpha),
        grid=(m // tm, n_tiles, nk),
        in_specs=in_specs,
        out_specs=out_spec,
        out_shape=out_shape,
        scratch_shapes=scratch,
        compiler_params=_params(("parallel", "parallel", "arbitrary")),
        name=name,
    )(*ops)


def _panel_kernel(*refs, kind, n_extra, dil, shift, trans_b, tn, alpha):
    a_ref, b_ref = refs[0], refs[1]
    n_b = 2 if shift else 1
    extras = refs[1 + n_b:1 + n_b + n_extra]
    o_ref = refs[1 + n_b + n_extra]
    scratch = list(refs[2 + n_b + n_extra:])
    bb_ref = scratch.pop(0)
    stage_ref = scratch.pop(0) if kind == "a_group" else None
    j = pl.program_id(0)
    i = pl.program_id(1)

    @pl.when(i == 0)
    def _():
        w = b_ref[...].astype(BF16)
        if kind == "kiwi":
            q = IDX_DIM
            bb_ref[0:q, :] = w[0:q]
            bb_ref[q:2 * q, :] = w[0:q]
            bb_ref[2 * q:3 * q, :] = w[q:2 * q]
            bb_ref[3 * q:4 * q, :] = w[q:2 * q]
        elif trans_b:
            bb_ref[:b_ref.shape[0], :] = w
            if shift:
                bb_ref[b_ref.shape[0]:, :] = refs[2][...].astype(BF16)
        else:
            bb_ref[...] = w

    if trans_b:
        acc = lax.dot_general(a_ref[...], bb_ref[shift:shift + tn, :], NT_DIMS, preferred_element_type=F32)
    else:
        acc = jnp.dot(a_ref[...], bb_ref[...], preferred_element_type=F32)
    _epilogue(kind, acc, extras, o_ref, stage_ref, dil, j, alpha)


def _panel_matmul(a, b, layer, *, tn, n_tiles, src_block, kind="none", extras=(), out_dtype=BF16,
                  name="mm", dil=1, shift=0, trans_b=False, alpha=None):
    m, k = a.shape
    tm = _pick(m, (1024, 512, 256))
    assert m % tm == 0 and tm % dil == 0 and tn % LANES == 0
    assert shift == 0 or (trans_b and shift % 16 == 0 and shift < LANES)
    out_tn = 2 * tn if kind == "kiwi" else tn
    b_rows = tn + (LANES if shift else 0)
    in_specs = [pl.BlockSpec((tm, k), lambda j, i: (i, 0))]
    ops = [a, b]
    if trans_b:
        in_specs.append(pl.BlockSpec((None, tn, k), lambda j, i: (layer, src_block(j), 0)))
        if shift:
            per = tn // LANES
            in_specs.append(pl.BlockSpec((None, LANES, k), lambda j, i: (layer, (src_block(j) + 1) * per, 0)))
            ops.append(b)
        scratch = [pltpu.VMEM((out_tn if kind == "kiwi" else b_rows, k), BF16)]
    else:
        in_specs.append(pl.BlockSpec((None, k, tn), lambda j, i: (layer, 0, src_block(j))))
        scratch = [pltpu.VMEM((k, tn), BF16)]
    for arr, bshape, imap in extras:
        in_specs.append(pl.BlockSpec(bshape, functools.partial(lambda j, i, f: f(i, j), f=imap)))
        ops.append(arr)
    if kind == "a_group":
        scratch.append(pltpu.VMEM((tn // LANES, tm, LANES), F32))
        out_shape = jax.ShapeDtypeStruct((n_tiles, dil, m // dil, tn), out_dtype)
        out_spec = pl.BlockSpec((None, dil, tm // dil, tn), lambda j, i: (j, 0, i, 0))
    else:
        out_shape = jax.ShapeDtypeStruct((m, n_tiles * out_tn), out_dtype)
        out_spec = pl.BlockSpec((tm, out_tn), lambda j, i: (i, j))
    return pl.pallas_call(
        functools.partial(_panel_kernel, kind=kind, n_extra=len(extras), dil=dil, shift=shift,
                          trans_b=trans_b, tn=out_tn, alpha=alpha),
        grid=(n_tiles, m // tm),
        in_specs=in_specs,
        out_specs=out_spec,
        out_shape=out_shape,
        scratch_shapes=scratch,
        compiler_params=_params(("arbitrary", "arbitrary")),
        name=name,
    )(*ops)


def _ln_kernel(z_ref, g_ref, b_ref, o_ref, ob_ref):
    z = z_ref[...]
    mu = jnp.mean(z, axis=-1, keepdims=True)
    zc = z - mu
    var = jnp.mean(zc * zc, axis=-1, keepdims=True)
    out = (zc * lax.rsqrt(var + LN_EPS)) * g_ref[...] + b_ref[...]
    o_ref[...] = out
    ob_ref[...] = out.astype(BF16)


def _layer_norm(z, g, b, layer):
    t, d = z.shape
    tm = _pick(t, (512, 256, 128))
    row = pl.BlockSpec((tm, d), lambda i: (i, 0))
    par = pl.BlockSpec((None, 1, d), lambda i: (layer, 0, 0))
    return pl.pallas_call(
        _ln_kernel,
        grid=(t // tm,),
        in_specs=[row, par, par],
        out_specs=[row, row],
        out_shape=[jax.ShapeDtypeStruct((t, d), F32), jax.ShapeDtypeStruct((t, d), BF16)],
        compiler_params=_params(("parallel",)),
        name="layer_norm",
    )(z, g, b)


def _merge_kernel(ya_ref, wa_ref, yb_ref, wb_ref, ga_ref, gb_ref, o_ref):
    pa = jnp.dot(ya_ref[...], wa_ref[...].astype(BF16), preferred_element_type=F32)
    pb = jnp.dot(yb_ref[...], wb_ref[...].astype(BF16), preferred_element_type=F32)
    o_ref[...] = (ga_ref[...].astype(F32) * pa + gb_ref[...].astype(F32) * pb).astype(o_ref.dtype)


def _gated_merge(ya, yb, w_pa, w_pb, gates, layer):
    t = ya.shape[0]
    d = w_pa.shape[2]
    tm = _pick(t, (1024, 512, 256))
    tn = d if d <= 1280 else _pick(d, (1024, 512, 256, 128))
    ka, kb = ya.shape[1], yb.shape[1]
    return pl.pallas_call(
        _merge_kernel,
        grid=(t // tm, d // tn),
        in_specs=[
            pl.BlockSpec((tm, ka), lambda i, j: (i, 0)),
            pl.BlockSpec((None, ka, tn), lambda i, j: (layer, 0, j)),
            pl.BlockSpec((tm, kb), lambda i, j: (i, 0)),
            pl.BlockSpec((None, kb, tn), lambda i, j: (layer, 0, j)),
            pl.BlockSpec((tm, tn), lambda i, j: (i, j)),
            pl.BlockSpec((tm, tn), lambda i, j: (i, j + d // tn)),
        ],
        out_specs=pl.BlockSpec((tm, tn), lambda i, j: (i, j)),
        out_shape=jax.ShapeDtypeStruct((t, d), BF16),
        compiler_params=_params(("parallel", "parallel")),
        name="gated_merge",
    )(ya, w_pa, yb, w_pb, gates, gates)


def _dil_kernel(q_ref, kp_ref, kc_ref, vp_ref, vc_ref, o_ref, lse_ref, *, nsub):
    i = pl.program_id(1)
    blk = BAND
    qi = lax.broadcasted_iota(jnp.int32, (blk, blk), 0)
    ki = lax.broadcasted_iota(jnp.int32, (blk, blk), 1)
    ok_cur = ki <= qi
    scale = HEAD_DIM ** -0.5
    for s in range(nsub):
        rows = slice(s * blk, (s + 1) * blk)
        if s == 0:
            ok_prev = ki >= qi + jnp.where(i > 0, 0, blk)
            k_prev, v_prev = kp_ref, vp_ref
            prows = slice(0, blk)
        else:
            ok_prev = ki >= qi
            k_prev, v_prev = kc_ref, vc_ref
            prows = slice((s - 1) * blk, s * blk)
        for h in range(A_HEADS_PER_GROUP):
            sl = slice(h * HEAD_DIM, (h + 1) * HEAD_DIM)
            q = q_ref[rows, sl]
            sp = lax.dot_general(q, k_prev[prows, sl], NT_DIMS, preferred_element_type=F32) * scale
            sc = lax.dot_general(q, kc_ref[rows, sl], NT_DIMS, preferred_element_type=F32) * scale
            sp = jnp.where(ok_prev, sp, NEG_INF)
            sc = jnp.where(ok_cur, sc, NEG_INF)
            m = jnp.maximum(jnp.max(sp, axis=-1, keepdims=True), jnp.max(sc, axis=-1, keepdims=True))
            ep = jnp.exp(sp - m)
            ec = jnp.exp(sc - m)
            l = jnp.sum(ep, axis=-1, keepdims=True) + jnp.sum(ec, axis=-1, keepdims=True)
            lse = m + jnp.log(l)
            inv = 1.0 / l
            pp = (ep * inv).astype(BF16)
            pc = (ec * inv).astype(BF16)
            o = (jnp.dot(pp, v_prev[prows, sl], preferred_element_type=F32)
                 + jnp.dot(pc, vc_ref[rows, sl], preferred_element_type=F32))
            o_ref[rows, sl] = o
            lse_ref[rows, sl] = jnp.broadcast_to(lse, (blk, HEAD_DIM))


def _dilated_group(qkv, dil):
    sub = qkv.shape[2]
    assert sub % BAND == 0
    nsub = _pick(sub // BAND, (4, 2, 1))
    rows = nsub * BAND

    def spec(which):
        return pl.BlockSpec((None, None, rows, A_OUT), lambda r, i: (which, r, i, 0))

    def prev_spec(which):
        return pl.BlockSpec((None, None, BAND, A_OUT),
                            lambda r, i: (which, r, jnp.maximum(i * nsub - 1, 0), 0))

    out_spec = pl.BlockSpec((None, rows, A_OUT), lambda r, i: (r, i, 0))
    return pl.pallas_call(
        functools.partial(_dil_kernel, nsub=nsub),
        grid=(dil, sub // rows),
        in_specs=[spec(0), prev_spec(1), spec(1), prev_spec(2), spec(2)],
        out_specs=[out_spec, out_spec],
        out_shape=[jax.ShapeDtypeStruct((dil, sub, A_OUT), F32)] * 2,
        compiler_params=_params(("parallel", "parallel")),
        name=f"dilated_attn_d{dil}",
    )(qkv, qkv, qkv, qkv, qkv)


def _mix_kernel(*refs, dils):
    n = len(dils)
    o_refs, l_refs, y_ref = refs[:n], refs[n:2 * n], refs[2 * n]
    stage = refs[2 * n + 1:]
    tm = y_ref.shape[0]

    def natural(ref, slot, dil, cs):
        if dil == 1:
            return ref[0, :, cs]
        for r in range(dil):
            slot[pl.ds(r, tm // dil, stride=dil), :] = ref[r, :, cs]
        return slot[...]

    for c in range(A_OUT // LANES):
        cs = slice(c * LANES, (c + 1) * LANES)
        outs = [natural(o_refs[g], stage[2 * g], dils[g], cs) for g in range(n)]
        lses = [natural(l_refs[g], stage[2 * g + 1], dils[g], cs) for g in range(n)]
        m = functools.reduce(jnp.maximum, lses)
        es = [jnp.exp(l - m) for l in lses]
        inv = 1.0 / functools.reduce(lambda a, b: a + b, es)
        y = functools.reduce(lambda a, b: a + b, [(e * inv) * o for e, o in zip(es, outs)])
        y_ref[:, cs] = y.astype(y_ref.dtype)


def _mix_groups(outs, lses, dils, t):
    tm = _pick(t, (512, 256))
    specs = [pl.BlockSpec((d, tm // d, A_OUT), lambda i: (0, i, 0)) for d in dils]
    return pl.pallas_call(
        functools.partial(_mix_kernel, dils=tuple(dils)),
        grid=(t // tm,),
        in_specs=specs + specs,
        out_specs=pl.BlockSpec((tm, A_OUT), lambda i: (i, 0)),
        out_shape=jax.ShapeDtypeStruct((t, A_OUT), BF16),
        scratch_shapes=[pltpu.VMEM((tm, LANES), F32)] * (2 * len(dils)),
        compiler_params=_params(("parallel",)),
        name="mix_groups",
    )(*outs, *lses)


def _sortable(x):
    b = pltpu.bitcast(x, jnp.int32)
    return b ^ ((b >> 31) & jnp.int32(INT_MAX))


def _dsa_kernel(step_q_ref, step_c_ref, qi_ref, ki_ref, wi_ref, q_ref, k_ref, v_ref, o_ref,
                key_ref, bias_ref, qs_ref, wb_ref, thr_ref, mthr_ref, m_ref, acc_ref,
                *, tq, tkc, topk):
    i = step_q_ref[pl.program_id(0)]
    c = step_c_ref[pl.program_id(0)]
    last_c = ((i + 1) * tq - 1) // tkc
    n_ch = last_c + 1
    row_pos = i * tq + lax.broadcasted_iota(jnp.int32, (tq, tkc), 0)
    col_iota = lax.broadcasted_iota(jnp.int32, (tq, tkc), 1)
    sub = MXU_COLS
    row_sub = i * tq + lax.broadcasted_iota(jnp.int32, (tq, sub), 0)
    col_sub = lax.broadcasted_iota(jnp.int32, (tq, sub), 1)

    @pl.when(c == 0)
    def _select():
        lo_half = lax.broadcasted_iota(jnp.int32, (tq, LANES), 1) < IDX_DIM
        for hp in range(IDX_HEADS // 2):
            pair = qi_ref[:, hp * LANES:(hp + 1) * LANES]
            qs_ref[(2 * hp) * tq:(2 * hp + 1) * tq, :] = jnp.where(lo_half, pair, 0.0).astype(BF16)
            qs_ref[(2 * hp + 1) * tq:(2 * hp + 2) * tq, :] = jnp.where(lo_half, 0.0, pair).astype(BF16)
        w = wi_ref[...] * (IDX_DIM ** -0.5 * IDX_HEADS ** -0.5)
        for h in range(IDX_HEADS):
            wb_ref[h] = w[:, h:h + 1] + jnp.zeros((tq, LANES), F32)

        def score_chunk(kc, carry):
            off = pl.multiple_of(kc * tkc, tkc)
            for s in range(tkc // sub):
                kis = ki_ref[pl.ds(off + s * sub, sub), :].astype(BF16)
                r = lax.dot_general(qs_ref[...], kis, NT_DIMS, preferred_element_type=F32)
                halves = []
                for half in range(sub // LANES):
                    cs = slice(half * LANES, (half + 1) * LANES)
                    sc = jnp.zeros((tq, LANES), F32)
                    for h in range(IDX_HEADS):
                        sc = sc + wb_ref[h] * jnp.maximum(r[h * tq:(h + 1) * tq, cs], 0.0)
                    halves.append(sc)
                sc = jnp.concatenate(halves, axis=1)
                cols = slice(s * sub, (s + 1) * sub)
                sc = jnp.where(off + s * sub + col_sub <= row_sub, sc, NEG_INF)
                key_ref[kc, :, cols] = _sortable(sc)
            return carry

        lax.fori_loop(0, n_ch, score_chunk, 0)

        def count(pred):
            def body(kc, acc):
                off = pl.multiple_of(kc * tkc, tkc)
                part = jnp.where(pred(key_ref[kc], off), 1, 0).astype(jnp.int32)
                for s in range(tkc // LANES):
                    acc = acc + part[:, s * LANES:(s + 1) * LANES]
                return acc
            acc = lax.fori_loop(0, n_ch, body, jnp.zeros((tq, LANES), jnp.int32))
            return jnp.sum(acc, axis=1, keepdims=True)

        def unsettled(state):
            it, lo, hi, n_lo = state
            open_rows = jnp.where((n_lo == topk) | (lo == hi), 0, 1)
            return jnp.logical_and(it < 33, jnp.max(open_rows) > 0)

        def bisect(state):
            it, lo, hi, n_lo = state
            mid = (lo | hi) - ((lo ^ hi) >> 1)
            n_mid = count(lambda kk, off: kk >= mid)
            ok = n_mid >= topk
            return (it + 1, jnp.where(ok, mid, lo), jnp.where(ok, hi, mid - 1), jnp.where(ok, n_mid, n_lo))

        lo0 = jnp.full((tq, 1), INT_MIN, jnp.int32)
        hi0 = jnp.full((tq, 1), INT_MAX, jnp.int32)
        n0 = jnp.full((tq, 1), INT_MAX, jnp.int32)
        _, thr, _, n_ge = lax.while_loop(unsettled, bisect, (jnp.int32(0), lo0, hi0, n0))
        thr_ref[...] = thr
        mthr_ref[...] = jnp.full((tq, 1), INT_MAX, jnp.int32)

        @pl.when(jnp.max(n_ge) > topk)
        def _ties():
            need = topk - count(lambda kk, off: kk > thr)

            def bisect_idx(_, lohi):
                lo, hi = lohi
                mid = (lo + hi) >> 1
                ok = count(lambda kk, off: (kk == thr) & (off + col_iota <= mid)) >= need
                return jnp.where(ok, lo, mid + 1), jnp.where(ok, mid, hi)

            lo1 = jnp.zeros((tq, 1), jnp.int32)
            hi1 = jnp.full((tq, 1), 2 ** 30, jnp.int32)
            _, m_idx = lax.fori_loop(0, 31, bisect_idx, (lo1, hi1))
            mthr_ref[...] = m_idx

        m_ref[...] = jnp.full(m_ref.shape, M_FLOOR, F32)
        acc_ref[...] = jnp.zeros(acc_ref.shape, F32)

    @pl.when(c <= last_c)
    def _attend():
        off = pl.multiple_of(c * tkc, tkc)
        kk = key_ref[c]
        thr = thr_ref[...]
        pos = off + col_iota
        chosen = (kk > thr) | ((kk == thr) & (pos <= mthr_ref[...]))
        bias_ref[...] = jnp.where(chosen & (pos <= row_pos), 0.0, NEG_INF)
        ones = jnp.ones((tkc, HEAD_DIM), BF16)
        for h in range(B_HEADS):
            sl = slice(h * HEAD_DIM, (h + 1) * HEAD_DIM)
            s = lax.dot_general(q_ref[:, sl], k_ref[:, sl], NT_DIMS, preferred_element_type=F32) + bias_ref[...]
            m_old = m_ref[h]
            m_new = jnp.maximum(m_old, jnp.max(s, axis=-1, keepdims=True))
            p = jnp.exp2(s - m_new).astype(BF16)
            alpha = jnp.exp2(m_old - m_new)
            v_one = jnp.concatenate([v_ref[:, sl], ones], axis=1)
            acc_ref[h] = alpha * acc_ref[h] + jnp.dot(p, v_one, preferred_element_type=F32)
            m_ref[h] = m_new

    @pl.when(c == last_c)
    def _finish():
        for h in range(B_HEADS):
            a = acc_ref[h]
            o_ref[:, h * HEAD_DIM:(h + 1) * HEAD_DIM] = (a[:, :HEAD_DIM] / a[:, HEAD_DIM:]).astype(o_ref.dtype)


def _dsa(qi, kiwi, qk, v):
    t = qi.shape[0]
    tq = _pick(t, (256, 128))
    tkc = _pick(t, (1024, 512, 256))
    topk = min(DSA_TOPK, t // 4)
    assert tkc >= topk and t % tq == 0 and tkc % MXU_COLS == 0
    nq, nc = t // tq, t // tkc
    steps = [(i, c) for i in range(nq) for c in range(((i + 1) * tq - 1) // tkc + 1)]
    step_q = jnp.asarray([s[0] for s in steps], jnp.int32)
    step_c = jnp.asarray([s[1] for s in steps], jnp.int32)

    in_specs = [
        pl.BlockSpec((tq, IDX_WIDTH), lambda s, sq, sc: (sq[s], 0)),
        pl.BlockSpec((t, LANES), lambda s, sq, sc: (0, 0)),
        pl.BlockSpec((tq, LANES), lambda s, sq, sc: (sq[s], 1)),
        pl.BlockSpec((tq, B_WIDTH), lambda s, sq, sc: (sq[s], 0)),
        pl.BlockSpec((tkc, B_WIDTH), lambda s, sq, sc: (sc[s], 1)),
        pl.BlockSpec((tkc, B_WIDTH), lambda s, sq, sc: (sc[s], 0)),
    ]
    grid_spec = pltpu.PrefetchScalarGridSpec(
        num_scalar_prefetch=2,
        grid=(len(steps),),
        in_specs=in_specs,
        out_specs=pl.BlockSpec((tq, B_WIDTH), lambda s, sq, sc: (sq[s], 0)),
        scratch_shapes=[
            pltpu.VMEM((nc, tq, tkc), jnp.int32),
            pltpu.VMEM((tq, tkc), F32),
            pltpu.VMEM((IDX_HEADS * tq, LANES), BF16),
            pltpu.VMEM((IDX_HEADS, tq, LANES), F32),
            pltpu.VMEM((tq, 1), jnp.int32),
            pltpu.VMEM((tq, 1), jnp.int32),
            pltpu.VMEM((B_HEADS, tq, 1), F32),
            pltpu.VMEM((B_HEADS, tq, 2 * HEAD_DIM), F32),
        ],
    )
    return pl.pallas_call(
        functools.partial(_dsa_kernel, tq=tq, tkc=tkc, topk=topk),
        grid_spec=grid_spec,
        out_shape=jax.ShapeDtypeStruct((t, B_WIDTH), BF16),
        compiler_params=_params(("arbitrary",)),
        name="dsa",
    )(step_q, step_c, qi, kiwi, kiwi, qk, qk, v)


def _mem_kernel(q_ref, kv_ref, o_ref):
    scale = HEAD_DIM ** -0.5
    for h in range(MEM_HEADS):
        sl = slice(h * HEAD_DIM, (h + 1) * HEAD_DIM)
        vsl = slice(MEM_WIDTH + h * HEAD_DIM, MEM_WIDTH + (h + 1) * HEAD_DIM)
        s = lax.dot_general(q_ref[:, sl], kv_ref[:, sl], NT_DIMS, preferred_element_type=F32) * scale
        m = jnp.max(s, axis=-1, keepdims=True)
        e = jnp.exp(s - m)
        p = e / jnp.sum(e, axis=-1, keepdims=True)
        o_ref[:, sl] = jnp.dot(p.astype(BF16), kv_ref[:, vsl], preferred_element_type=F32).astype(o_ref.dtype)


def _mem_attention(q, kv):
    t = q.shape[0]
    mlen = kv.shape[0]
    tq = _pick(t, (512, 256, 128))
    return pl.pallas_call(
        _mem_kernel,
        grid=(t // tq,),
        in_specs=[pl.BlockSpec((tq, MEM_WIDTH), lambda i: (i, 0)),
                  pl.BlockSpec((mlen, 2 * MEM_WIDTH), lambda i: (0, 0))],
        out_specs=pl.BlockSpec((tq, MEM_WIDTH), lambda i: (i, 0)),
        out_shape=jax.ShapeDtypeStruct((t, MEM_WIDTH), BF16),
        compiler_params=_params(("parallel",)),
        name="mem_attention",
    )(q, kv)


def _cast_kernel(x_ref, o_ref):
    o_ref[...] = x_ref[...].astype(o_ref.dtype)


def _to_bf16(x):
    t, d = x.shape
    tm = _pick(t, (256, 128))
    spec = pl.BlockSpec((tm, d), lambda i: (i, 0))
    return pl.pallas_call(
        _cast_kernel, grid=(t // tm,), in_specs=[spec], out_specs=spec,
        out_shape=jax.ShapeDtypeStruct((t, d), BF16),
        compiler_params=_params(("parallel",)), name="cast_bf16",
    )(x)


def _rope_tables(t):
    pos = jnp.arange(t, dtype=jnp.int32).astype(F32)[:, None]
    half = HEAD_DIM // 2
    inv = ROPE_THETA ** (-jnp.arange(half, dtype=F32) / half)
    ang = pos * inv[None, :]
    cos128 = jnp.concatenate([jnp.cos(ang)] * 2, axis=1)
    sin128 = jnp.concatenate([-jnp.sin(ang), jnp.sin(ang)], axis=1)
    half = IDX_DIM // 2
    inv = ROPE_THETA ** (-jnp.arange(half, dtype=F32) / half)
    ang = pos * inv[None, :]
    z = jnp.zeros_like(ang)
    cos64 = jnp.concatenate([jnp.cos(ang)] * 4, axis=1)
    sin_lo = jnp.concatenate([-jnp.sin(ang), z] * 2, axis=1)
    sin_hi = jnp.concatenate([z, jnp.sin(ang)] * 2, axis=1)
    return (cos128, sin128), (cos64, sin_lo, sin_hi)


def _mixer_branches(xb, w_in_t, l, blk, rope128, rope64):
    t = xb.shape[0]
    proj = functools.partial(_panel_matmul, xb, w_in_t, l, trans_b=True)
    outs, lses, dils = [], [], []
    for g, (win, dil) in enumerate(DIL_GROUPS):
        assert win // dil == BAND
        qkv = proj(kind="a_group", extras=rope128, tn=COL_BLOCK, n_tiles=3, dil=dil,
                   src_block=lambda j, g=g: blk[0] + g + j * (A_WIDTH // COL_BLOCK), name=f"proj_a_g{g}")
        o, lse = _dilated_group(qkv, dil)
        outs.append(o)
        lses.append(lse)
        dils.append(dil)
    ya = _mix_groups(outs, lses, dils, t)
    qk_b = proj(kind="rope128_qk", extras=rope128, tn=COL_BLOCK, n_tiles=2 * B_WIDTH // COL_BLOCK,
                src_block=lambda j: blk[3] + j, name="proj_b_qk")
    v_b = proj(tn=COL_BLOCK, n_tiles=B_WIDTH // COL_BLOCK, src_block=lambda j: blk[5] + j, name="proj_b_v")
    qi = proj(kind="rope64", extras=rope64, out_dtype=F32, tn=COL_BLOCK, n_tiles=IDX_WIDTH // COL_BLOCK,
              src_block=lambda j: blk[6] + j, name="proj_qi")
    kiwi = proj(kind="kiwi", extras=rope64, out_dtype=F32, tn=LANES, n_tiles=1,
                src_block=lambda j: blk[7] * (COL_BLOCK // LANES), name="proj_kiwi")
    yb = _dsa(qi, kiwi, qk_b, v_b)
    return ya, yb


def kernel(x, mem, w_in, w_pa, w_pb, w_o, ln1_g, ln1_b, w_mq, w_mkv, w_mo, ln2_g, ln2_b, w_up, w_down, ln3_g, ln3_b):
    bsz, t, d = x.shape
    depth = w_in.shape[0]
    assert bsz == 1
    alpha = (2 * depth) ** 0.25

    w_in_t = jnp.swapaxes(w_in, 1, 2)
    sizes = (A_WIDTH, A_WIDTH, A_WIDTH, B_WIDTH, B_WIDTH, B_WIDTH, IDX_WIDTH, IDX_DIM, IDX_HEADS, d, d)
    offs = [0]
    for s in sizes:
        offs.append(offs[-1] + s)
    assert offs[-1] == w_in.shape[2] and all(o % COL_BLOCK == 0 for o in offs[:8])
    blk = [o // COL_BLOCK for o in offs[:8]]
    gate_shift = offs[9] - offs[7]
    assert 0 < gate_shift < LANES and (2 * d) % COL_BLOCK == 0
    lnp = [p.reshape(depth, 1, d) for p in (ln1_g, ln1_b, ln2_g, ln2_b, ln3_g, ln3_b)]

    (cos128, sin128), (cos64, sin_lo, sin_hi) = _rope_tables(t)
    tm_rows = _pick(t, (1024, 512, 256))

    xf = x[0]
    xb = _to_bf16(xf)
    memb = _to_bf16(mem[0])

    def row_tab(arr):
        return (arr, (tm_rows, LANES), lambda i, j: (i, 0))

    def tile(arr, tn, joff=0):
        return (arr, (tm_rows, tn), lambda i, j: (i, j + joff))

    rope128 = (row_tab(cos128), row_tab(sin128))
    rope64 = (row_tab(cos64), row_tab(sin_lo), row_tab(sin_hi))
    tn_d = d if d <= 1280 else _pick(d, (1024, 512, 256, 128))
    tn_p = _pick(d, (COL_BLOCK, 256, 128))

    for l in range(depth):
        ya, yb = _mixer_branches(xb, w_in_t, l, blk, rope128, rope64)
        gates = _panel_matmul(xb, w_in_t, l, kind="sigmoid", tn=COL_BLOCK, trans_b=True,
                              n_tiles=2 * d // COL_BLOCK, src_block=lambda j: blk[7] + j, shift=gate_shift,
                              name="proj_gate")
        merged = _gated_merge(ya, yb, w_pa, w_pb, gates, l)
        z = _panel_matmul(merged, w_o, l, kind="resid", extras=(tile(xf, tn_p),), alpha=alpha, out_dtype=F32,
                          tn=tn_p, n_tiles=d // tn_p, src_block=lambda j: j, name="mixer_out")
        xf, xb = _layer_norm(z, lnp[0], lnp[1], l)
        qm = _panel_matmul(xb, w_mq, l, tn=MEM_WIDTH, n_tiles=1, src_block=lambda j: j, name="mem_q")
        kvm = _matmul(memb, w_mkv, l, name="mem_kv")
        om = _mem_attention(qm, kvm)
        z = _matmul(om, w_mo, l, kind="resid", extras=(tile(xf, tn_d),), alpha=alpha, out_dtype=F32,
                    name="mem_out")
        xf, xb = _layer_norm(z, lnp[2], lnp[3], l)
        hdn = _panel_matmul(xb, w_up, l, kind="relu2", tn=tn_p, n_tiles=w_up.shape[2] // tn_p,
                            src_block=lambda j: j, name="mlp_up")
        z = _matmul(hdn, w_down, l, kind="resid", extras=(tile(xf, tn_d),), alpha=alpha, out_dtype=F32,
                    name="mlp_down")
        xf, xb = _layer_norm(z, lnp[4], lnp[5], l)
    return xf[None]
```

```python
import functools

import jax
import jax.numpy as jnp
from jax import lax
from jax.experimental import pallas as pl
from jax.experimental.pallas import tpu as pltpu

F32 = jnp.float32
BF16 = jnp.bfloat16

HEAD_DIM = 128
DIL_GROUPS = ((128, 1), (512, 4), (2048, 16))
A_HEADS_PER_GROUP = 4
A_GROUPS = len(DIL_GROUPS)
A_WIDTH = A_GROUPS * A_HEADS_PER_GROUP * HEAD_DIM
A_OUT = A_HEADS_PER_GROUP * HEAD_DIM
B_HEADS = 8
B_WIDTH = B_HEADS * HEAD_DIM
IDX_HEADS = 16
IDX_DIM = 64
IDX_WIDTH = IDX_HEADS * IDX_DIM
DSA_TOPK = 256
MEM_HEADS = 4
MEM_WIDTH = MEM_HEADS * HEAD_DIM
ROPE_THETA = 10000.0
LN_EPS = 1e-5
NEG_INF = -1e30
M_FLOOR = -1e29
BAND = 128
LOG2E = 1.4426950408889634

LANES = 128
MXU_COLS = 256
VMEM_LIMIT = 56 * 1024 * 1024
B_TILE_BYTES = 8 * 1024 * 1024
COL_BLOCK = 512

INT_MIN = -(2 ** 31)
INT_MAX = 2 ** 31 - 1
NT_DIMS = (((1,), (1,)), ((), ()))


def _pick(n, prefs):
    for p in prefs:
        if n % p == 0:
            return p
    return n


def _params(sem):
    return pltpu.CompilerParams(dimension_semantics=sem, vmem_limit_bytes=VMEM_LIMIT)


def _rope128(x, cos, sin):
    return x * cos + pltpu.roll(x, 64, 1) * sin


def _rope64(x, cos, sin_lo, sin_hi):
    return x * cos + pltpu.roll(x, 96, 1) * sin_lo + pltpu.roll(x, 32, 1) * sin_hi


def _rope128_tile(acc, cos, sin):
    return [_rope128(acc[:, c * LANES:(c + 1) * LANES], cos, sin) for c in range(acc.shape[1] // LANES)]


def _epilogue(kind, acc, extras, o_ref, stage_ref, dil, j, alpha):
    tn = acc.shape[1]
    if kind == "none":
        o_ref[...] = acc.astype(o_ref.dtype)
    elif kind == "resid":
        o_ref[...] = (alpha * extras[0][...] + acc).astype(o_ref.dtype)
    elif kind == "rope128":
        for c, y in enumerate(_rope128_tile(acc, extras[0][...], extras[1][...])):
            o_ref[:, c * LANES:(c + 1) * LANES] = y.astype(o_ref.dtype)
    elif kind == "rope128_qk":
        qs = jnp.where(j < B_WIDTH // COL_BLOCK, HEAD_DIM ** -0.5 * LOG2E, 1.0)
        for c, y in enumerate(_rope128_tile(acc, extras[0][...], extras[1][...])):
            o_ref[:, c * LANES:(c + 1) * LANES] = (y * qs).astype(o_ref.dtype)
    elif kind == "rope64":
        cos, s_lo, s_hi = extras[0][...], extras[1][...], extras[2][...]
        for c in range(tn // LANES):
            sl = slice(c * LANES, (c + 1) * LANES)
            o_ref[:, sl] = _rope64(acc[:, sl], cos, s_lo, s_hi).astype(o_ref.dtype)
    elif kind == "kiwi":
        cos, s_lo, s_hi = extras[0][...], extras[1][...], extras[2][...]
        o_ref[:, :LANES] = _rope64(acc[:, :LANES], cos, s_lo, s_hi).astype(o_ref.dtype)
        o_ref[:, LANES:] = acc[:, LANES:].astype(o_ref.dtype)
    elif kind == "a_group":
        @pl.when(j < 2)
        def _():
            for c, y in enumerate(_rope128_tile(acc, extras[0][...], extras[1][...])):
                stage_ref[c] = y

        @pl.when(j == 2)
        def _():
            for c in range(tn // LANES):
                stage_ref[c] = acc[:, c * LANES:(c + 1) * LANES]

        rows = acc.shape[0] // dil
        for r in range(dil):
            for c in range(tn // LANES):
                src = stage_ref[c] if dil == 1 else stage_ref[c, pl.ds(r, rows, stride=dil), :]
                o_ref[r, :, c * LANES:(c + 1) * LANES] = src.astype(o_ref.dtype)
    elif kind == "sigmoid":
        o_ref[...] = (1.0 / (1.0 + jnp.exp(-acc))).astype(o_ref.dtype)
    elif kind == "relu2":
        r = jnp.maximum(acc, 0.0)
        o_ref[...] = (r * r).astype(o_ref.dtype)
    else:
        raise ValueError(kind)


def _mm_kernel(*refs, nk, kind, n_extra, dil, alpha):
    a_ref, b_ref = refs[0], refs[1]
    extras = refs[2:2 + n_extra]
    o_ref = refs[2 + n_extra]
    scratch = list(refs[3 + n_extra:])
    acc_ref = scratch.pop(0) if nk > 1 else None
    stage_ref = scratch.pop(0) if kind == "a_group" else None
    j = pl.program_id(1)
    k = pl.program_id(2)
    part = jnp.dot(a_ref[...], b_ref[...].astype(BF16), preferred_element_type=F32)
    if nk == 1:
        _epilogue(kind, part, extras, o_ref, stage_ref, dil, j, alpha)
        return

    @pl.when(k == 0)
    def _():
        acc_ref[...] = part

    @pl.when(k > 0)
    def _():
        acc_ref[...] += part

    @pl.when(k == nk - 1)
    def _():
        _epilogue(kind, acc_ref[...], extras, o_ref, stage_ref, dil, j, alpha)


def _matmul(a, b, layer, *, kind="none", extras=(), out_dtype=BF16, name="mm",
            tn=None, n_tiles=None, src_block=None, dil=1, alpha=None):
    m, k = a.shape
    nb = b.shape[2]
    tm = _pick(m, (1024, 512, 256))
    if tn is None:
        tn = nb if nb <= 1280 else _pick(nb, (1024, 512, 256, 128))
    if n_tiles is None:
        n_tiles = nb // tn
    if src_block is None:
        src_block = lambda j: j
    itemsize = jnp.dtype(b.dtype).itemsize
    tk = k
    for cand in (k, 4096, 2048, 1024, 512):
        if cand <= k and k % cand == 0:
            tk = cand
            if cand * tn * itemsize <= B_TILE_BYTES:
                break
    nk = k // tk
    assert m % tm == 0 and k % tk == 0 and tm % dil == 0
    in_specs = [
        pl.BlockSpec((tm, tk), lambda i, j, kk: (i, kk)),
        pl.BlockSpec((None, tk, tn), lambda i, j, kk: (layer, kk, src_block(j))),
    ]
    ops = [a, b]
    for arr, bshape, imap in extras:
        in_specs.append(pl.BlockSpec(bshape, functools.partial(lambda i, j, kk, f: f(i, j), f=imap)))
        ops.append(arr)
    scratch = [pltpu.VMEM((tm, tn), F32)] if nk > 1 else []
    if kind == "a_group":
        scratch.append(pltpu.VMEM((tn // LANES, tm, LANES), F32))
        out_shape = jax.ShapeDtypeStruct((n_tiles, dil, m // dil, tn), out_dtype)
        out_spec = pl.BlockSpec((None, dil, tm // dil, tn), lambda i, j, kk: (j, 0, i, 0))
    else:
        out_shape = jax.ShapeDtypeStruct((m, n_tiles * tn), out_dtype)
        out_spec = pl.BlockSpec((tm, tn), lambda i, j, kk: (i, j))
    return pl.pallas_call(
        functools.partial(_mm_kernel, nk=nk, kind=kind, n_extra=len(extras), dil=dil, alpha=alpha),
        grid=(m // tm, n_tiles, nk),
        in_specs=in_specs,
        out_specs=out_spec,
        out_shape=out_shape,
        scratch_shapes=scratch,
        compiler_params=_params(("parallel", "parallel", "arbitrary")),
        name=name,
    )(*ops)


def _panel_kernel(*refs, kind, n_extra, dil, shift, trans_b, tn, alpha):
    a_ref, b_ref = refs[0], refs[1]
    n_b = 2 if shift else 1
    extras = refs[1 + n_b:1 + n_b + n_extra]
    o_ref = refs[1 + n_b + n_extra]
    scratch = list(refs[2 + n_b + n_extra:])
    bb_ref = scratch.pop(0)
    stage_ref = scratch.pop(0) if kind == "a_group" else None
    j = pl.program_id(0)
    i = pl.program_id(1)

    @pl.when(i == 0)
    def _():
        w = b_ref[...].astype(BF16)
        if kind == "kiwi":
            q = IDX_DIM
            bb_ref[0:q, :] = w[0:q]
            bb_ref[q:2 * q, :] = w[0:q]
            bb_ref[2 * q:3 * q, :] = w[q:2 * q]
            bb_ref[3 * q:4 * q, :] = w[q:2 * q]
        elif trans_b:
            bb_ref[:b_ref.shape[0], :] = w
            if shift:
                bb_ref[b_ref.shape[0]:, :] = refs[2][...].astype(BF16)
        else:
            bb_ref[...] = w

    if trans_b:
        acc = lax.dot_general(a_ref[...], bb_ref[shift:shift + tn, :], NT_DIMS, preferred_element_type=F32)
    else:
        acc = jnp.dot(a_ref[...], bb_ref[...], preferred_element_type=F32)
    _epilogue(kind, acc, extras, o_ref, stage_ref, dil, j, alpha)


def _panel_matmul(a, b, layer, *, tn, n_tiles, src_block, kind="none", extras=(), out_dtype=BF16,
                  name="mm", dil=1, shift=0, trans_b=False, alpha=None):
    m, k = a.shape
    tm = _pick(m, (1024, 512, 256))
    assert m % tm == 0 and tm % dil == 0 and tn % LANES == 0
    assert shift == 0 or (trans_b and shift % 16 == 0 and shift < LANES)
    out_tn = 2 * tn if kind == "kiwi" else tn
    b_rows = tn + (LANES if shift else 0)
    in_specs = [pl.BlockSpec((tm, k), lambda j, i: (i, 0))]
    ops = [a, b]
    if trans_b:
        in_specs.append(pl.BlockSpec((None, tn, k), lambda j, i: (layer, src_block(j), 0)))
        if shift:
            per = tn // LANES
            in_specs.append(pl.BlockSpec((None, LANES, k), lambda j, i: (layer, (src_block(j) + 1) * per, 0)))
            ops.append(b)
        scratch = [pltpu.VMEM((out_tn if kind == "kiwi" else b_rows, k), BF16)]
    else:
        in_specs.append(pl.BlockSpec((None, k, tn), lambda j, i: (layer, 0, src_block(j))))
        scratch = [pltpu.VMEM((k, tn), BF16)]
    for arr, bshape, imap in extras:
        in_specs.append(pl.BlockSpec(bshape, functools.partial(lambda j, i, f: f(i, j), f=imap)))
        ops.append(arr)
    if kind == "a_group":
        scratch.append(pltpu.VMEM((tn // LANES, tm, LANES), F32))
        out_shape = jax.ShapeDtypeStruct((n_tiles, dil, m // dil, tn), out_dtype)
        out_spec = pl.BlockSpec((None, dil, tm // dil, tn), lambda j, i: (j, 0, i, 0))
    else:
        out_shape = jax.ShapeDtypeStruct((m, n_tiles * out_tn), out_dtype)
        out_spec = pl.BlockSpec((tm, out_tn), lambda j, i: (i, j))
    return pl.pallas_call(
        functools.partial(_panel_kernel, kind=kind, n_extra=len(extras), dil=dil, shift=shift,
                          trans_b=trans_b, tn=out_tn, alpha=alpha),
        grid=(n_tiles, m // tm),
        in_specs=in_specs,
        out_specs=out_spec,
        out_shape=out_shape,
        scratch_shapes=scratch,
        compiler_params=_params(("arbitrary", "arbitrary")),
        name=name,
    )(*ops)


def _ln_kernel(z_ref, g_ref, b_ref, o_ref, ob_ref):
    z = z_ref[...]
    mu = jnp.mean(z, axis=-1, keepdims=True)
    zc = z - mu
    var = jnp.mean(zc * zc, axis=-1, keepdims=True)
    out = (zc * lax.rsqrt(var + LN_EPS)) * g_ref[...] + b_ref[...]
    o_ref[...] = out
    ob_ref[...] = out.astype(BF16)


def _layer_norm(z, g, b, layer):
    t, d = z.shape
    tm = _pick(t, (512, 256, 128))
    row = pl.BlockSpec((tm, d), lambda i: (i, 0))
    par = pl.BlockSpec((None, 1, d), lambda i: (layer, 0, 0))
    return pl.pallas_call(
        _ln_kernel,
        grid=(t // tm,),
        in_specs=[row, par, par],
        out_specs=[row, row],
        out_shape=[jax.ShapeDtypeStruct((t, d), F32), jax.ShapeDtypeStruct((t, d), BF16)],
        compiler_params=_params(("parallel",)),
        name="layer_norm",
    )(z, g, b)


def _merge_kernel(ya_ref, wa_ref, yb_ref, wb_ref, ga_ref, gb_ref, o_ref):
    pa = jnp.dot(ya_ref[...], wa_ref[...].astype(BF16), preferred_element_type=F32)
    pb = jnp.dot(yb_ref[...], wb_ref[...].astype(BF16), preferred_element_type=F32)
    o_ref[...] = (ga_ref[...].astype(F32) * pa + gb_ref[...].astype(F32) * pb).astype(o_ref.dtype)


def _gated_merge(ya, yb, w_pa, w_pb, gates, layer):
    t = ya.shape[0]
    d = w_pa.shape[2]
    tm = _pick(t, (1024, 512, 256))
    tn = d if d <= 1280 else _pick(d, (1024, 512, 256, 128))
    ka, kb = ya.shape[1], yb.shape[1]
    return pl.pallas_call(
        _merge_kernel,
        grid=(t // tm, d // tn),
        in_specs=[
            pl.BlockSpec((tm, ka), lambda i, j: (i, 0)),
            pl.BlockSpec((None, ka, tn), lambda i, j: (layer, 0, j)),
            pl.BlockSpec((tm, kb), lambda i, j: (i, 0)),
            pl.BlockSpec((None, kb, tn), lambda i, j: (layer, 0, j)),
            pl.BlockSpec((tm, tn), lambda i, j: (i, j)),
            pl.BlockSpec((tm, tn), lambda i, j: (i, j + d // tn)),
        ],
        out_specs=pl.BlockSpec((tm, tn), lambda i, j: (i, j)),
        out_shape=jax.ShapeDtypeStruct((t, d), BF16),
        compiler_params=_params(("parallel", "parallel")),
        name="gated_merge",
    )(ya, w_pa, yb, w_pb, gates, gates)


def _dil_kernel(q_ref, kp_ref, kc_ref, vp_ref, vc_ref, o_ref, lse_ref, *, nsub):
    i = pl.program_id(1)
    qrows = min(nsub, 2) * BAND
    qq = lax.broadcasted_iota(jnp.int32, (qrows, BAND + qrows), 0)
    kk = lax.broadcasted_iota(jnp.int32, (qrows, BAND + qrows), 1)
    in_band = (kk >= qq) & (kk <= qq + BAND)
    bias = jnp.where(in_band, 0.0, NEG_INF)
    bias_first = jnp.where(in_band & (kk >= jnp.where(i > 0, 0, BAND)), 0.0, NEG_INF)
    scale = HEAD_DIM ** -0.5
    chains = [(g, h) for g in range(nsub * BAND // qrows) for h in range(A_HEADS_PER_GROUP)]
    scores, v_bands = [], []
    for g, h in chains:
        rows = slice(g * qrows, (g + 1) * qrows)
        band = slice(g * qrows - BAND, (g + 1) * qrows)
        sl = slice(h * HEAD_DIM, (h + 1) * HEAD_DIM)
        if g == 0:
            k_band = jnp.concatenate([kp_ref[:, sl], kc_ref[rows, sl]], axis=0)
            v_bands.append(jnp.concatenate([vp_ref[:, sl], vc_ref[rows, sl]], axis=0))
        else:
            k_band = kc_ref[band, sl]
            v_bands.append(vc_ref[band, sl])
        s = lax.dot_general(q_ref[rows, sl], k_band, NT_DIMS, preferred_element_type=F32) * scale
        scores.append(s + (bias_first if g == 0 else bias))
    maxes = [jnp.max(s, axis=-1, keepdims=True) for s in scores]
    exps = [jnp.exp(s - m) for s, m in zip(scores, maxes)]
    unnorm = [jnp.dot(e.astype(BF16), v, preferred_element_type=F32) for e, v in zip(exps, v_bands)]
    sums = [jnp.sum(e, axis=-1, keepdims=True) for e in exps]
    outs = [o * (1.0 / l) for o, l in zip(unnorm, sums)]
    lses = [jnp.broadcast_to(m + jnp.log(l), (qrows, HEAD_DIM)) for m, l in zip(maxes, sums)]
    nh = A_HEADS_PER_GROUP
    ngrp = len(outs) // nh
    o_ref[...] = jnp.concatenate(
        [jnp.concatenate(outs[g * nh:(g + 1) * nh], axis=1) for g in range(ngrp)], axis=0)
    lse_ref[...] = jnp.concatenate(
        [jnp.concatenate(lses[g * nh:(g + 1) * nh], axis=1) for g in range(ngrp)], axis=0)


def _dilated_group(qkv, dil):
    sub = qkv.shape[2]
    assert sub % BAND == 0
    nsub = _pick(sub // BAND, (4, 2, 1))
    rows = nsub * BAND

    def spec(which):
        return pl.BlockSpec((None, None, rows, A_OUT), lambda r, i: (which, r, i, 0))

    def prev_spec(which):
        return pl.BlockSpec((None, None, BAND, A_OUT),
                            lambda r, i: (which, r, jnp.maximum(i * nsub - 1, 0), 0))

    out_spec = pl.BlockSpec((None, rows, A_OUT), lambda r, i: (r, i, 0))
    return pl.pallas_call(
        functools.partial(_dil_kernel, nsub=nsub),
        grid=(dil, sub // rows),
        in_specs=[spec(0), prev_spec(1), spec(1), prev_spec(2), spec(2)],
        out_specs=[out_spec, out_spec],
        out_shape=[jax.ShapeDtypeStruct((dil, sub, A_OUT), F32)] * 2,
        compiler_params=_params(("parallel", "parallel")),
        name=f"dilated_attn_d{dil}",
    )(qkv, qkv, qkv, qkv, qkv)


def _mix_kernel(*refs, dils):
    n = len(dils)
    o_refs, l_refs, y_ref = refs[:n], refs[n:2 * n], refs[2 * n]
    stage = refs[2 * n + 1:]
    tm = y_ref.shape[0]

    def natural(ref, slot, dil, cs):
        if dil == 1:
            return ref[0, :, cs]
        for r in range(dil):
            slot[pl.ds(r, tm // dil, stride=dil), :] = ref[r, :, cs]
        return slot[...]

    for c in range(A_OUT // LANES):
        cs = slice(c * LANES, (c + 1) * LANES)
        outs = [natural(o_refs[g], stage[2 * g], dils[g], cs) for g in range(n)]
        lses = [natural(l_refs[g], stage[2 * g + 1], dils[g], cs) for g in range(n)]
        m = functools.reduce(jnp.maximum, lses)
        es = [jnp.exp(l - m) for l in lses]
        inv = 1.0 / functools.reduce(lambda a, b: a + b, es)
        y = functools.reduce(lambda a, b: a + b, [(e * inv) * o for e, o in zip(es, outs)])
        y_ref[:, cs] = y.astype(y_ref.dtype)


def _mix_groups(outs, lses, dils, t):
    tm = _pick(t, (512, 256))
    specs = [pl.BlockSpec((d, tm // d, A_OUT), lambda i: (0, i, 0)) for d in dils]
    return pl.pallas_call(
        functools.partial(_mix_kernel, dils=tuple(dils)),
        grid=(t // tm,),
        in_specs=specs + specs,
        out_specs=pl.BlockSpec((tm, A_OUT), lambda i: (i, 0)),
        out_shape=jax.ShapeDtypeStruct((t, A_OUT), BF16),
        scratch_shapes=[pltpu.VMEM((tm, LANES), F32)] * (2 * len(dils)),
        compiler_params=_params(("parallel",)),
        name="mix_groups",
    )(*outs, *lses)


def _sortable(x):
    b = pltpu.bitcast(x, jnp.int32)
    return b ^ ((b >> 31) & jnp.int32(INT_MAX))


def _dsa_kernel(step_q_ref, step_c_ref, qi_ref, ki_ref, wi_ref, q_ref, k_ref, v_ref, o_ref,
                key_ref, bias_ref, qs_ref, wb_ref, thr_ref, mthr_ref, m_ref, acc_ref,
                *, tq, tkc, topk):
    i = step_q_ref[pl.program_id(0)]
    c = step_c_ref[pl.program_id(0)]
    last_c = ((i + 1) * tq - 1) // tkc
    n_ch = last_c + 1
    row_pos = i * tq + lax.broadcasted_iota(jnp.int32, (tq, tkc), 0)
    col_iota = lax.broadcasted_iota(jnp.int32, (tq, tkc), 1)
    sub = MXU_COLS
    row_sub = i * tq + lax.broadcasted_iota(jnp.int32, (tq, sub), 0)
    col_sub = lax.broadcasted_iota(jnp.int32, (tq, sub), 1)

    @pl.when(c == 0)
    def _select():
        lo_half = lax.broadcasted_iota(jnp.int32, (tq, LANES), 1) < IDX_DIM
        for hp in range(IDX_HEADS // 2):
            pair = qi_ref[:, hp * LANES:(hp + 1) * LANES]
            qs_ref[(2 * hp) * tq:(2 * hp + 1) * tq, :] = jnp.where(lo_half, pair, 0.0).astype(BF16)
            qs_ref[(2 * hp + 1) * tq:(2 * hp + 2) * tq, :] = jnp.where(lo_half, 0.0, pair).astype(BF16)
        w = wi_ref[...] * (IDX_DIM ** -0.5 * IDX_HEADS ** -0.5)
        for h in range(IDX_HEADS):
            wb_ref[h] = w[:, h:h + 1] + jnp.zeros((tq, LANES), F32)

        def score_chunk(kc, carry):
            off = pl.multiple_of(kc * tkc, tkc)
            for s in range(tkc // sub):
                kis = ki_ref[pl.ds(off + s * sub, sub), :].astype(BF16)
                r = lax.dot_general(qs_ref[...], kis, NT_DIMS, preferred_element_type=F32)
                halves = []
                for half in range(sub // LANES):
                    cs = slice(half * LANES, (half + 1) * LANES)
                    sc = jnp.zeros((tq, LANES), F32)
                    for h in range(IDX_HEADS):
                        sc = sc + wb_ref[h] * jnp.maximum(r[h * tq:(h + 1) * tq, cs], 0.0)
                    halves.append(sc)
                sc = jnp.concatenate(halves, axis=1)
                cols = slice(s * sub, (s + 1) * sub)
                sc = jnp.where(off + s * sub + col_sub <= row_sub, sc, NEG_INF)
                key_ref[kc, :, cols] = _sortable(sc)
            return carry

        lax.fori_loop(0, n_ch, score_chunk, 0)

        def count(pred):
            def body(kc, acc):
                off = pl.multiple_of(kc * tkc, tkc)
                part = jnp.where(pred(key_ref[kc], off), 1, 0).astype(jnp.int32)
                for s in range(tkc // LANES):
                    acc = acc + part[:, s * LANES:(s + 1) * LANES]
                return acc
            acc = lax.fori_loop(0, n_ch, body, jnp.zeros((tq, LANES), jnp.int32))
            return jnp.sum(acc, axis=1, keepdims=True)

        def unsettled(state):
            it, lo, hi, n_lo = state
            open_rows = jnp.where((n_lo == topk) | (lo == hi), 0, 1)
            return jnp.logical_and(it < 33, jnp.max(open_rows) > 0)

        def bisect(state):
            it, lo, hi, n_lo = state
            mid = (lo | hi) - ((lo ^ hi) >> 1)
            n_mid = count(lambda kk, off: kk >= mid)
            ok = n_mid >= topk
            return (it + 1, jnp.where(ok, mid, lo), jnp.where(ok, hi, mid - 1), jnp.where(ok, n_mid, n_lo))

        lo0 = jnp.full((tq, 1), INT_MIN, jnp.int32)
        hi0 = jnp.full((tq, 1), INT_MAX, jnp.int32)
        n0 = jnp.full((tq, 1), INT_MAX, jnp.int32)
        _, thr, _, n_ge = lax.while_loop(unsettled, bisect, (jnp.int32(0), lo0, hi0, n0))
        thr_ref[...] = thr
        mthr_ref[...] = jnp.full((tq, 1), INT_MAX, jnp.int32)

        @pl.when(jnp.max(n_ge) > topk)
        def _ties():
            need = topk - count(lambda kk, off: kk > thr)

            def bisect_idx(_, lohi):
                lo, hi = lohi
                mid = (lo + hi) >> 1
                ok = count(lambda kk, off: (kk == thr) & (off + col_iota <= mid)) >= need
                return jnp.where(ok, lo, mid + 1), jnp.where(ok, mid, hi)

            lo1 = jnp.zeros((tq, 1), jnp.int32)
            hi1 = jnp.full((tq, 1), 2 ** 30, jnp.int32)
            _, m_idx = lax.fori_loop(0, 31, bisect_idx, (lo1, hi1))
            mthr_ref[...] = m_idx

        m_ref[...] = jnp.full(m_ref.shape, M_FLOOR, F32)
        acc_ref[...] = jnp.zeros(acc_ref.shape, F32)

    @pl.when(c <= last_c)
    def _attend():
        off = pl.multiple_of(c * tkc, tkc)
        kk = key_ref[c]
        thr = thr_ref[...]
        pos = off + col_iota
        chosen = (kk > thr) | ((kk == thr) & (pos <= mthr_ref[...]))
        bias_ref[...] = jnp.where(chosen & (pos <= row_pos), 0.0, NEG_INF)
        ones = jnp.ones((tkc, HEAD_DIM), BF16)
        for h in range(B_HEADS):
            sl = slice(h * HEAD_DIM, (h + 1) * HEAD_DIM)
            s = lax.dot_general(q_ref[:, sl], k_ref[:, sl], NT_DIMS, preferred_element_type=F32) + bias_ref[...]
            m_old = m_ref[h]
            m_new = jnp.maximum(m_old, jnp.max(s, axis=-1, keepdims=True))
            p = jnp.exp2(s - m_new).astype(BF16)
            alpha = jnp.exp2(m_old - m_new)
            v_one = jnp.concatenate([v_ref[:, sl], ones], axis=1)
            acc_ref[h] = alpha * acc_ref[h] + jnp.dot(p, v_one, preferred_element_type=F32)
            m_ref[h] = m_new

    @pl.when(c == last_c)
    def _finish():
        for h in range(B_HEADS):
            a = acc_ref[h]
            o_ref[:, h * HEAD_DIM:(h + 1) * HEAD_DIM] = (a[:, :HEAD_DIM] / a[:, HEAD_DIM:]).astype(o_ref.dtype)


def _dsa(qi, kiwi, qk, v):
    t = qi.shape[0]
    tq = _pick(t, (256, 128))
    tkc = _pick(t, (1024, 512, 256))
    topk = min(DSA_TOPK, t // 4)
    assert tkc >= topk and t % tq == 0 and tkc % MXU_COLS == 0
    nq, nc = t // tq, t // tkc
    steps = [(i, c) for i in range(nq) for c in range(((i + 1) * tq - 1) // tkc + 1)]
    step_q = jnp.asarray([s[0] for s in steps], jnp.int32)
    step_c = jnp.asarray([s[1] for s in steps], jnp.int32)

    in_specs = [
        pl.BlockSpec((tq, IDX_WIDTH), lambda s, sq, sc: (sq[s], 0)),
        pl.BlockSpec((t, LANES), lambda s, sq, sc: (0, 0)),
        pl.BlockSpec((tq, LANES), lambda s, sq, sc: (sq[s], 1)),
        pl.BlockSpec((tq, B_WIDTH), lambda s, sq, sc: (sq[s], 0)),
        pl.BlockSpec((tkc, B_WIDTH), lambda s, sq, sc: (sc[s], 1)),
        pl.BlockSpec((tkc, B_WIDTH), lambda s, sq, sc: (sc[s], 0)),
    ]
    grid_spec = pltpu.PrefetchScalarGridSpec(
        num_scalar_prefetch=2,
        grid=(len(steps),),
        in_specs=in_specs,
        out_specs=pl.BlockSpec((tq, B_WIDTH), lambda s, sq, sc: (sq[s], 0)),
        scratch_shapes=[
            pltpu.VMEM((nc, tq, tkc), jnp.int32),
            pltpu.VMEM((tq, tkc), F32),
            pltpu.VMEM((IDX_HEADS * tq, LANES), BF16),
            pltpu.VMEM((IDX_HEADS, tq, LANES), F32),
            pltpu.VMEM((tq, 1), jnp.int32),
            pltpu.VMEM((tq, 1), jnp.int32),
            pltpu.VMEM((B_HEADS, tq, 1), F32),
            pltpu.VMEM((B_HEADS, tq, 2 * HEAD_DIM), F32),
        ],
    )
    return pl.pallas_call(
        functools.partial(_dsa_kernel, tq=tq, tkc=tkc, topk=topk),
        grid_spec=grid_spec,
        out_shape=jax.ShapeDtypeStruct((t, B_WIDTH), BF16),
        compiler_params=_params(("arbitrary",)),
        name="dsa",
    )(step_q, step_c, qi, kiwi, kiwi, qk, qk, v)


def _mem_kernel(q_ref, kv_ref, o_ref):
    scale = HEAD_DIM ** -0.5
    for h in range(MEM_HEADS):
        sl = slice(h * HEAD_DIM, (h + 1) * HEAD_DIM)
        vsl = slice(MEM_WIDTH + h * HEAD_DIM, MEM_WIDTH + (h + 1) * HEAD_DIM)
        s = lax.dot_general(q_ref[:, sl], kv_ref[:, sl], NT_DIMS, preferred_element_type=F32) * scale
        m = jnp.max(s, axis=-1, keepdims=True)
        e = jnp.exp(s - m)
        p = e / jnp.sum(e, axis=-1, keepdims=True)
        o_ref[:, sl] = jnp.dot(p.astype(BF16), kv_ref[:, vsl], preferred_element_type=F32).astype(o_ref.dtype)


def _mem_attention(q, kv):
    t = q.shape[0]
    mlen = kv.shape[0]
    tq = _pick(t, (512, 256, 128))
    return pl.pallas_call(
        _mem_kernel,
        grid=(t // tq,),
        in_specs=[pl.BlockSpec((tq, MEM_WIDTH), lambda i: (i, 0)),
                  pl.BlockSpec((mlen, 2 * MEM_WIDTH), lambda i: (0, 0))],
        out_specs=pl.BlockSpec((tq, MEM_WIDTH), lambda i: (i, 0)),
        out_shape=jax.ShapeDtypeStruct((t, MEM_WIDTH), BF16),
        compiler_params=_params(("parallel",)),
        name="mem_attention",
    )(q, kv)


def _cast_kernel(x_ref, o_ref):
    o_ref[...] = x_ref[...].astype(o_ref.dtype)


def _to_bf16(x):
    t, d = x.shape
    tm = _pick(t, (256, 128))
    spec = pl.BlockSpec((tm, d), lambda i: (i, 0))
    return pl.pallas_call(
        _cast_kernel, grid=(t // tm,), in_specs=[spec], out_specs=spec,
        out_shape=jax.ShapeDtypeStruct((t, d), BF16),
        compiler_params=_params(("parallel",)), name="cast_bf16",
    )(x)


def _rope_tables(t):
    pos = jnp.arange(t, dtype=jnp.int32).astype(F32)[:, None]
    half = HEAD_DIM // 2
    inv = ROPE_THETA ** (-jnp.arange(half, dtype=F32) / half)
    ang = pos * inv[None, :]
    cos128 = jnp.concatenate([jnp.cos(ang)] * 2, axis=1)
    sin128 = jnp.concatenate([-jnp.sin(ang), jnp.sin(ang)], axis=1)
    half = IDX_DIM // 2
    inv = ROPE_THETA ** (-jnp.arange(half, dtype=F32) / half)
    ang = pos * inv[None, :]
    z = jnp.zeros_like(ang)
    cos64 = jnp.concatenate([jnp.cos(ang)] * 4, axis=1)
    sin_lo = jnp.concatenate([-jnp.sin(ang), z] * 2, axis=1)
    sin_hi = jnp.concatenate([z, jnp.sin(ang)] * 2, axis=1)
    return (cos128, sin128), (cos64, sin_lo, sin_hi)


def _mixer_branches(xb, w_in_t, l, blk, rope128, rope64):
    t = xb.shape[0]
    proj = functools.partial(_panel_matmul, xb, w_in_t, l, trans_b=True)
    outs, lses, dils = [], [], []
    for g, (win, dil) in enumerate(DIL_GROUPS):
        assert win // dil == BAND
        qkv = proj(kind="a_group", extras=rope128, tn=COL_BLOCK, n_tiles=3, dil=dil,
                   src_block=lambda j, g=g: blk[0] + g + j * (A_WIDTH // COL_BLOCK), name=f"proj_a_g{g}")
        o, lse = _dilated_group(qkv, dil)
        outs.append(o)
        lses.append(lse)
        dils.append(dil)
    ya = _mix_groups(outs, lses, dils, t)
    qk_b = proj(kind="rope128_qk", extras=rope128, tn=COL_BLOCK, n_tiles=2 * B_WIDTH // COL_BLOCK,
                src_block=lambda j: blk[3] + j, name="proj_b_qk")
    v_b = proj(tn=COL_BLOCK, n_tiles=B_WIDTH // COL_BLOCK, src_block=lambda j: blk[5] + j, name="proj_b_v")
    qi = proj(kind="rope64", extras=rope64, out_dtype=F32, tn=COL_BLOCK, n_tiles=IDX_WIDTH // COL_BLOCK,
              src_block=lambda j: blk[6] + j, name="proj_qi")
    kiwi = proj(kind="kiwi", extras=rope64, out_dtype=F32, tn=LANES, n_tiles=1,
                src_block=lambda j: blk[7] * (COL_BLOCK // LANES), name="proj_kiwi")
    yb = _dsa(qi, kiwi, qk_b, v_b)
    return ya, yb


def kernel(x, mem, w_in, w_pa, w_pb, w_o, ln1_g, ln1_b, w_mq, w_mkv, w_mo, ln2_g, ln2_b, w_up, w_down, ln3_g, ln3_b):
    bsz, t, d = x.shape
    depth = w_in.shape[0]
    assert bsz == 1
    alpha = (2 * depth) ** 0.25

    w_in_t = jnp.swapaxes(w_in, 1, 2)
    sizes = (A_WIDTH, A_WIDTH, A_WIDTH, B_WIDTH, B_WIDTH, B_WIDTH, IDX_WIDTH, IDX_DIM, IDX_HEADS, d, d)
    offs = [0]
    for s in sizes:
        offs.append(offs[-1] + s)
    assert offs[-1] == w_in.shape[2] and all(o % COL_BLOCK == 0 for o in offs[:8])
    blk = [o // COL_BLOCK for o in offs[:8]]
    gate_shift = offs[9] - offs[7]
    assert 0 < gate_shift < LANES and (2 * d) % COL_BLOCK == 0
    lnp = [p.reshape(depth, 1, d) for p in (ln1_g, ln1_b, ln2_g, ln2_b, ln3_g, ln3_b)]

    (cos128, sin128), (cos64, sin_lo, sin_hi) = _rope_tables(t)
    tm_rows = _pick(t, (1024, 512, 256))

    xf = x[0]
    xb = _to_bf16(xf)
    memb = _to_bf16(mem[0])

    def row_tab(arr):
        return (arr, (tm_rows, LANES), lambda i, j: (i, 0))

    def tile(arr, tn, joff=0):
        return (arr, (tm_rows, tn), lambda i, j: (i, j + joff))

    rope128 = (row_tab(cos128), row_tab(sin128))
    rope64 = (row_tab(cos64), row_tab(sin_lo), row_tab(sin_hi))
    tn_d = d if d <= 1280 else _pick(d, (1024, 512, 256, 128))
    tn_p = _pick(d, (COL_BLOCK, 256, 128))

    for l in range(depth):
        ya, yb = _mixer_branches(xb, w_in_t, l, blk, rope128, rope64)
        gates = _panel_matmul(xb, w_in_t, l, kind="sigmoid", tn=COL_BLOCK, trans_b=True,
                              n_tiles=2 * d // COL_BLOCK, src_block=lambda j: blk[7] + j, shift=gate_shift,
                              name="proj_gate")
        merged = _gated_merge(ya, yb, w_pa, w_pb, gates, l)
        z = _panel_matmul(merged, w_o, l, kind="resid", extras=(tile(xf, tn_p),), alpha=alpha, out_dtype=F32,
                          tn=tn_p, n_tiles=d // tn_p, src_block=lambda j: j, name="mixer_out")
        xf, xb = _layer_norm(z, lnp[0], lnp[1], l)
        qm = _panel_matmul(xb, w_mq, l, tn=MEM_WIDTH, n_tiles=1, src_block=lambda j: j, name="mem_q")
        kvm = _matmul(memb, w_mkv, l, name="mem_kv")
        om = _mem_attention(qm, kvm)
        z = _matmul(om, w_mo, l, kind="resid", extras=(tile(xf, tn_d),), alpha=alpha, out_dtype=F32,
                    name="mem_out")
        xf, xb = _layer_norm(z, lnp[2], lnp[3], l)
        hdn = _panel_matmul(xb, w_up, l, kind="relu2", tn=tn_p, n_tiles=w_up.shape[2] // tn_p,
                            src_block=lambda j: j, name="mlp_up")
        z = _matmul(hdn, w_down, l, kind="resid", extras=(tile(xf, tn_d),), alpha=alpha, out_dtype=F32,
                    name="mlp_down")
        xf, xb = _layer_norm(z, lnp[4], lnp[5], l)
    return xf[None]
```

```python
import functools

import jax
import jax.numpy as jnp
from jax import lax
from jax.experimental import pallas as pl
from jax.experimental.pallas import tpu as pltpu

F32 = jnp.float32
BF16 = jnp.bfloat16

HEAD_DIM = 128
DIL_GROUPS = ((128, 1), (512, 4), (2048, 16))
A_HEADS_PER_GROUP = 4
A_GROUPS = len(DIL_GROUPS)
A_WIDTH = A_GROUPS * A_HEADS_PER_GROUP * HEAD_DIM
A_OUT = A_HEADS_PER_GROUP * HEAD_DIM
B_HEADS = 8
B_WIDTH = B_HEADS * HEAD_DIM
IDX_HEADS = 16
IDX_DIM = 64
IDX_WIDTH = IDX_HEADS * IDX_DIM
DSA_TOPK = 256
MEM_HEADS = 4
MEM_WIDTH = MEM_HEADS * HEAD_DIM
ROPE_THETA = 10000.0
LN_EPS = 1e-5
NEG_INF = -1e30
M_FLOOR = -1e29
BAND = 128
LOG2E = 1.4426950408889634

LANES = 128
MXU_COLS = 256
VMEM_LIMIT = 56 * 1024 * 1024
B_TILE_BYTES = 8 * 1024 * 1024
COL_BLOCK = 512

INT_MIN = -(2 ** 31)
INT_MAX = 2 ** 31 - 1
NT_DIMS = (((1,), (1,)), ((), ()))


def _pick(n, prefs):
    for p in prefs:
        if n % p == 0:
            return p
    return n


def _params(sem):
    return pltpu.CompilerParams(dimension_semantics=sem, vmem_limit_bytes=VMEM_LIMIT)


def _rope128(x, cos, sin):
    return x * cos + pltpu.roll(x, 64, 1) * sin


def _rope64(x, cos, sin_lo, sin_hi):
    return x * cos + pltpu.roll(x, 96, 1) * sin_lo + pltpu.roll(x, 32, 1) * sin_hi


def _rope128_tile(acc, cos, sin):
    return [_rope128(acc[:, c * LANES:(c + 1) * LANES], cos, sin) for c in range(acc.shape[1] // LANES)]


def _epilogue(kind, acc, extras, o_ref, stage_ref, dil, j, alpha):
    tn = acc.shape[1]
    if kind == "none":
        o_ref[...] = acc.astype(o_ref.dtype)
    elif kind == "resid":
        o_ref[...] = (alpha * extras[0][...] + acc).astype(o_ref.dtype)
    elif kind == "rope128":
        for c, y in enumerate(_rope128_tile(acc, extras[0][...], extras[1][...])):
            o_ref[:, c * LANES:(c + 1) * LANES] = y.astype(o_ref.dtype)
    elif kind == "rope128_qk":
        qs = jnp.where(j < B_WIDTH // COL_BLOCK, HEAD_DIM ** -0.5 * LOG2E, 1.0)
        for c, y in enumerate(_rope128_tile(acc, extras[0][...], extras[1][...])):
            o_ref[:, c * LANES:(c + 1) * LANES] = (y * qs).astype(o_ref.dtype)
    elif kind == "rope64":
        cos, s_lo, s_hi = extras[0][...], extras[1][...], extras[2][...]
        for c in range(tn // LANES):
            sl = slice(c * LANES, (c + 1) * LANES)
            o_ref[:, sl] = _rope64(acc[:, sl], cos, s_lo, s_hi).astype(o_ref.dtype)
    elif kind == "kiwi":
        cos, s_lo, s_hi = extras[0][...], extras[1][...], extras[2][...]
        o_ref[:, :LANES] = _rope64(acc[:, :LANES], cos, s_lo, s_hi).astype(o_ref.dtype)
        o_ref[:, LANES:] = acc[:, LANES:].astype(o_ref.dtype)
    elif kind == "a_group":
        cos = jnp.where(j < 2, extras[0][...], 1.0)
        sin = jnp.where(j < 2, extras[1][...], 0.0)
        for c, y in enumerate(_rope128_tile(acc, cos, sin)):
            stage_ref[c] = y

        rows = acc.shape[0] // dil
        for r in range(dil):
            for c in range(tn // LANES):
                src = stage_ref[c] if dil == 1 else stage_ref[c, pl.ds(r, rows, stride=dil), :]
                o_ref[r, :, c * LANES:(c + 1) * LANES] = src.astype(o_ref.dtype)
    elif kind == "sigmoid":
        o_ref[...] = (1.0 / (1.0 + jnp.exp(-acc))).astype(o_ref.dtype)
    elif kind == "relu2":
        r = jnp.maximum(acc, 0.0)
        o_ref[...] = (r * r).astype(o_ref.dtype)
    else:
        raise ValueError(kind)


def _mm_kernel(*refs, nk, kind, n_extra, dil, alpha):
    a_ref, b_ref = refs[0], refs[1]
    extras = refs[2:2 + n_extra]
    o_ref = refs[2 + n_extra]
    scratch = list(refs[3 + n_extra:])
    acc_ref = scratch.pop(0) if nk > 1 else None
    stage_ref = scratch.pop(0) if kind == "a_group" else None
    j = pl.program_id(1)
    k = pl.program_id(2)
    if nk == 1:
        part = jnp.dot(a_ref[...], b_ref[...].astype(BF16), preferred_element_type=F32)
        _epilogue(kind, part, extras, o_ref, stage_ref, dil, j, alpha)
        return

    @pl.when(k == 0)
    def _():
        acc_ref[...] = jnp.zeros(acc_ref.shape, F32)

    acc_ref[...] += jnp.dot(a_ref[...], b_ref[...].astype(BF16), preferred_element_type=F32)

    @pl.when(k == nk - 1)
    def _():
        _epilogue(kind, acc_ref[...], extras, o_ref, stage_ref, dil, j, alpha)


def _matmul(a, b, layer, *, kind="none", extras=(), out_dtype=BF16, name="mm",
            tn=None, n_tiles=None, src_block=None, dil=1, alpha=None):
    m, k = a.shape
    nb = b.shape[2]
    tm = _pick(m, (1024, 512, 256))
    if tn is None:
        tn = nb if nb <= 1280 else _pick(nb, (1024, 512, 256, 128))
    if n_tiles is None:
        n_tiles = nb // tn
    if src_block is None:
        src_block = lambda j: j
    itemsize = jnp.dtype(b.dtype).itemsize
    tk = k
    for cand in (k, 4096, 2048, 1024, 512):
        if cand <= k and k % cand == 0:
            tk = cand
            if cand * tn * itemsize <= B_TILE_BYTES:
                break
    nk = k // tk
    assert m % tm == 0 and k % tk == 0 and tm % dil == 0
    in_specs = [
        pl.BlockSpec((tm, tk), lambda i, j, kk: (i, kk)),
        pl.BlockSpec((None, tk, tn), lambda i, j, kk: (layer, kk, src_block(j))),
    ]
    ops = [a, b]
    for arr, bshape, imap in extras:
        in_specs.append(pl.BlockSpec(bshape, functools.partial(lambda i, j, kk, f: f(i, j), f=imap)))
        ops.append(arr)
    scratch = [pltpu.VMEM((tm, tn), F32)] if nk > 1 else []
    if kind == "a_group":
        scratch.append(pltpu.VMEM((tn // LANES, tm, LANES), F32))
        out_shape = jax.ShapeDtypeStruct((n_tiles, dil, m // dil, tn), out_dtype)
        out_spec = pl.BlockSpec((None, dil, tm // dil, tn), lambda i, j, kk: (j, 0, i, 0))
    else:
        out_shape = jax.ShapeDtypeStruct((m, n_tiles * tn), out_dtype)
        out_spec = pl.BlockSpec((tm, tn), lambda i, j, kk: (i, j))
    return pl.pallas_call(
        functools.partial(_mm_kernel, nk=nk, kind=kind, n_extra=len(extras), dil=dil, alpha=alpha),
        grid=(m // tm, n_tiles, nk),
        in_specs=in_specs,
        out_specs=out_spec,
        out_shape=out_shape,
        scratch_shapes=scratch,
        compiler_params=_params(("parallel", "parallel", "arbitrary")),
        name=name,
    )(*ops)


def _panel_kernel(*refs, kind, n_extra, dil, shift, trans_b, tn, alpha):
    a_ref, b_ref = refs[0], refs[1]
    n_b = 2 if shift else 1
    extras = refs[1 + n_b:1 + n_b + n_extra]
    o_ref = refs[1 + n_b + n_extra]
    scratch = list(refs[2 + n_b + n_extra:])
    bb_ref = scratch.pop(0)
    stage_ref = scratch.pop(0) if kind == "a_group" else None
    j = pl.program_id(0)
    i = pl.program_id(1)

    @pl.when(i == 0)
    def _():
        w = b_ref[...].astype(BF16)
        if kind == "kiwi":
            q = IDX_DIM
            bb_ref[0:q, :] = w[0:q]
            bb_ref[q:2 * q, :] = w[0:q]
            bb_ref[2 * q:3 * q, :] = w[q:2 * q]
            bb_ref[3 * q:4 * q, :] = w[q:2 * q]
        elif trans_b:
            bb_ref[:b_ref.shape[0], :] = w
            if shift:
                bb_ref[b_ref.shape[0]:, :] = refs[2][...].astype(BF16)
        else:
            bb_ref[...] = w

    if trans_b:
        acc = lax.dot_general(a_ref[...], bb_ref[shift:shift + tn, :], NT_DIMS, preferred_element_type=F32)
    else:
        acc = jnp.dot(a_ref[...], bb_ref[...], preferred_element_type=F32)
    _epilogue(kind, acc, extras, o_ref, stage_ref, dil, j, alpha)


def _panel_matmul(a, b, layer, *, tn, n_tiles, src_block, kind="none", extras=(), out_dtype=BF16,
                  name="mm", dil=1, shift=0, trans_b=False, alpha=None):
    m, k = a.shape
    tm = _pick(m, (1024, 512, 256))
    assert m % tm == 0 and tm % dil == 0 and tn % LANES == 0
    assert shift == 0 or (trans_b and shift % 16 == 0 and shift < LANES)
    out_tn = 2 * tn if kind == "kiwi" else tn
    b_rows = tn + (LANES if shift else 0)
    in_specs = [pl.BlockSpec((tm, k), lambda j, i: (i, 0))]
    ops = [a, b]
    if trans_b:
        in_specs.append(pl.BlockSpec((None, tn, k), lambda j, i: (layer, src_block(j), 0)))
        if shift:
            per = tn // LANES
            in_specs.append(pl.BlockSpec((None, LANES, k), lambda j, i: (layer, (src_block(j) + 1) * per, 0)))
            ops.append(b)
        scratch = [pltpu.VMEM((out_tn if kind == "kiwi" else b_rows, k), BF16)]
    else:
        in_specs.append(pl.BlockSpec((None, k, tn), lambda j, i: (layer, 0, src_block(j))))
        scratch = [pltpu.VMEM((k, tn), BF16)]
    for arr, bshape, imap in extras:
        in_specs.append(pl.BlockSpec(bshape, functools.partial(lambda j, i, f: f(i, j), f=imap)))
        ops.append(arr)
    if kind == "a_group":
        scratch.append(pltpu.VMEM((tn // LANES, tm, LANES), F32))
        out_shape = jax.ShapeDtypeStruct((n_tiles, dil, m // dil, tn), out_dtype)
        out_spec = pl.BlockSpec((None, dil, tm // dil, tn), lambda j, i: (j, 0, i, 0))
    else:
        out_shape = jax.ShapeDtypeStruct((m, n_tiles * out_tn), out_dtype)
        out_spec = pl.BlockSpec((tm, out_tn), lambda j, i: (i, j))
    return pl.pallas_call(
        functools.partial(_panel_kernel, kind=kind, n_extra=len(extras), dil=dil, shift=shift,
                          trans_b=trans_b, tn=out_tn, alpha=alpha),
        grid=(n_tiles, m // tm),
        in_specs=in_specs,
        out_specs=out_spec,
        out_shape=out_shape,
        scratch_shapes=scratch,
        compiler_params=_params(("arbitrary", "arbitrary")),
        name=name,
    )(*ops)


def _ln_kernel(z_ref, g_ref, b_ref, o_ref, ob_ref):
    z = z_ref[...]
    mu = jnp.mean(z, axis=-1, keepdims=True)
    zc = z - mu
    var = jnp.mean(zc * zc, axis=-1, keepdims=True)
    out = (zc * lax.rsqrt(var + LN_EPS)) * g_ref[...] + b_ref[...]
    o_ref[...] = out
    ob_ref[...] = out.astype(BF16)


def _layer_norm(z, g, b, layer):
    t, d = z.shape
    tm = _pick(t, (512, 256, 128))
    row = pl.BlockSpec((tm, d), lambda i: (i, 0))
    par = pl.BlockSpec((None, 1, d), lambda i: (layer, 0, 0))
    return pl.pallas_call(
        _ln_kernel,
        grid=(t // tm,),
        in_specs=[row, par, par],
        out_specs=[row, row],
        out_shape=[jax.ShapeDtypeStruct((t, d), F32), jax.ShapeDtypeStruct((t, d), BF16)],
        compiler_params=_params(("parallel",)),
        name="layer_norm",
    )(z, g, b)


def _merge_kernel(ya_ref, wa_ref, yb_ref, wb_ref, ga_ref, gb_ref, o_ref):
    pa = jnp.dot(ya_ref[...], wa_ref[...].astype(BF16), preferred_element_type=F32)
    pb = jnp.dot(yb_ref[...], wb_ref[...].astype(BF16), preferred_element_type=F32)
    o_ref[...] = (ga_ref[...].astype(F32) * pa + gb_ref[...].astype(F32) * pb).astype(o_ref.dtype)


def _gated_merge(ya, yb, w_pa, w_pb, gates, layer):
    t = ya.shape[0]
    d = w_pa.shape[2]
    tm = _pick(t, (1024, 512, 256))
    tn = d if d <= 1280 else _pick(d, (1024, 512, 256, 128))
    ka, kb = ya.shape[1], yb.shape[1]
    return pl.pallas_call(
        _merge_kernel,
        grid=(t // tm, d // tn),
        in_specs=[
            pl.BlockSpec((tm, ka), lambda i, j: (i, 0)),
            pl.BlockSpec((None, ka, tn), lambda i, j: (layer, 0, j)),
            pl.BlockSpec((tm, kb), lambda i, j: (i, 0)),
            pl.BlockSpec((None, kb, tn), lambda i, j: (layer, 0, j)),
            pl.BlockSpec((tm, tn), lambda i, j: (i, j)),
            pl.BlockSpec((tm, tn), lambda i, j: (i, j + d // tn)),
        ],
        out_specs=pl.BlockSpec((tm, tn), lambda i, j: (i, j)),
        out_shape=jax.ShapeDtypeStruct((t, d), BF16),
        compiler_params=_params(("parallel", "parallel")),
        name="gated_merge",
    )(ya, w_pa, yb, w_pb, gates, gates)


def _dil_kernel(q_ref, kp_ref, kc_ref, vp_ref, vc_ref, o_ref, lse_ref, *, nsub):
    i = pl.program_id(1)
    qrows = min(nsub, 2) * BAND
    qq = lax.broadcasted_iota(jnp.int32, (qrows, BAND + qrows), 0)
    kk = lax.broadcasted_iota(jnp.int32, (qrows, BAND + qrows), 1)
    in_band = (kk >= qq) & (kk <= qq + BAND)
    bias = jnp.where(in_band, 0.0, NEG_INF)
    bias_first = jnp.where(in_band & (kk >= jnp.where(i > 0, 0, BAND)), 0.0, NEG_INF)
    scale = HEAD_DIM ** -0.5
    chains = [(g, h) for g in range(nsub * BAND // qrows) for h in range(A_HEADS_PER_GROUP)]
    scores, v_bands = [], []
    for g, h in chains:
        rows = slice(g * qrows, (g + 1) * qrows)
        band = slice(g * qrows - BAND, (g + 1) * qrows)
        sl = slice(h * HEAD_DIM, (h + 1) * HEAD_DIM)
        if g == 0:
            k_band = jnp.concatenate([kp_ref[:, sl], kc_ref[rows, sl]], axis=0)
            v_bands.append(jnp.concatenate([vp_ref[:, sl], vc_ref[rows, sl]], axis=0))
        else:
            k_band = kc_ref[band, sl]
            v_bands.append(vc_ref[band, sl])
        s = lax.dot_general(q_ref[rows, sl], k_band, NT_DIMS, preferred_element_type=F32) * scale
        scores.append(s + (bias_first if g == 0 else bias))
    maxes = [jnp.max(s, axis=-1, keepdims=True) for s in scores]
    exps = [jnp.exp(s - m) for s, m in zip(scores, maxes)]
    unnorm = [jnp.dot(e.astype(BF16), v, preferred_element_type=F32) for e, v in zip(exps, v_bands)]
    sums = [jnp.sum(e, axis=-1, keepdims=True) for e in exps]
    outs = [o * (1.0 / l) for o, l in zip(unnorm, sums)]
    lses = [jnp.broadcast_to(m + jnp.log(l), (qrows, HEAD_DIM)) for m, l in zip(maxes, sums)]
    nh = A_HEADS_PER_GROUP
    ngrp = len(outs) // nh
    o_ref[...] = jnp.concatenate(
        [jnp.concatenate(outs[g * nh:(g + 1) * nh], axis=1) for g in range(ngrp)], axis=0)
    lse_ref[...] = jnp.concatenate(
        [jnp.concatenate(lses[g * nh:(g + 1) * nh], axis=1) for g in range(ngrp)], axis=0)


def _dilated_group(qkv, dil):
    sub = qkv.shape[2]
    assert sub % BAND == 0
    nsub = _pick(sub // BAND, (4, 2, 1))
    rows = nsub * BAND

    def spec(which):
        return pl.BlockSpec((None, None, rows, A_OUT), lambda r, i: (which, r, i, 0))

    def prev_spec(which):
        return pl.BlockSpec((None, None, BAND, A_OUT),
                            lambda r, i: (which, r, jnp.maximum(i * nsub - 1, 0), 0))

    out_spec = pl.BlockSpec((None, rows, A_OUT), lambda r, i: (r, i, 0))
    return pl.pallas_call(
        functools.partial(_dil_kernel, nsub=nsub),
        grid=(dil, sub // rows),
        in_specs=[spec(0), prev_spec(1), spec(1), prev_spec(2), spec(2)],
        out_specs=[out_spec, out_spec],
        out_shape=[jax.ShapeDtypeStruct((dil, sub, A_OUT), F32)] * 2,
        compiler_params=_params(("parallel", "parallel")),
        name=f"dilated_attn_d{dil}",
    )(qkv, qkv, qkv, qkv, qkv)


def _mix_kernel(*refs, dils):
    n = len(dils)
    o_refs, l_refs, y_ref = refs[:n], refs[n:2 * n], refs[2 * n]
    stage = refs[2 * n + 1:]
    tm = y_ref.shape[0]

    def natural(ref, slot, dil, cs):
        if dil == 1:
            return ref[0, :, cs]
        for r in range(dil):
            slot[pl.ds(r, tm // dil, stride=dil), :] = ref[r, :, cs]
        return slot[...]

    for c in range(A_OUT // LANES):
        cs = slice(c * LANES, (c + 1) * LANES)
        outs = [natural(o_refs[g], stage[2 * g], dils[g], cs) for g in range(n)]
        lses = [natural(l_refs[g], stage[2 * g + 1], dils[g], cs) for g in range(n)]
        m = functools.reduce(jnp.maximum, lses)
        es = [jnp.exp(l - m) for l in lses]
        inv = 1.0 / functools.reduce(lambda a, b: a + b, es)
        y = functools.reduce(lambda a, b: a + b, [(e * inv) * o for e, o in zip(es, outs)])
        y_ref[:, cs] = y.astype(y_ref.dtype)


def _mix_groups(outs, lses, dils, t):
    tm = _pick(t, (512, 256))
    specs = [pl.BlockSpec((d, tm // d, A_OUT), lambda i: (0, i, 0)) for d in dils]
    return pl.pallas_call(
        functools.partial(_mix_kernel, dils=tuple(dils)),
        grid=(t // tm,),
        in_specs=specs + specs,
        out_specs=pl.BlockSpec((tm, A_OUT), lambda i: (i, 0)),
        out_shape=jax.ShapeDtypeStruct((t, A_OUT), BF16),
        scratch_shapes=[pltpu.VMEM((tm, LANES), F32)] * (2 * len(dils)),
        compiler_params=_params(("parallel",)),
        name="mix_groups",
    )(*outs, *lses)


def _sortable(x):
    b = pltpu.bitcast(x, jnp.int32)
    return b ^ ((b >> 31) & jnp.int32(INT_MAX))


def _dsa_kernel(step_q_ref, step_c_ref, qi_ref, ki_ref, wi_ref, q_ref, k_ref, v_ref, o_ref,
                key_ref, bias_ref, qs_ref, wb_ref, thr_ref, mthr_ref, m_ref, acc_ref,
                *, tq, tkc, topk):
    i = step_q_ref[pl.program_id(0)]
    c = step_c_ref[pl.program_id(0)]
    last_c = ((i + 1) * tq - 1) // tkc
    n_ch = last_c + 1
    row_pos = i * tq + lax.broadcasted_iota(jnp.int32, (tq, tkc), 0)
    col_iota = lax.broadcasted_iota(jnp.int32, (tq, tkc), 1)
    sub = MXU_COLS
    row_sub = i * tq + lax.broadcasted_iota(jnp.int32, (tq, sub), 0)
    col_sub = lax.broadcasted_iota(jnp.int32, (tq, sub), 1)

    @pl.when(c == 0)
    def _select():
        lo_half = lax.broadcasted_iota(jnp.int32, (tq, LANES), 1) < IDX_DIM
        for hp in range(IDX_HEADS // 2):
            pair = qi_ref[:, hp * LANES:(hp + 1) * LANES]
            qs_ref[(2 * hp) * tq:(2 * hp + 1) * tq, :] = jnp.where(lo_half, pair, 0.0).astype(BF16)
            qs_ref[(2 * hp + 1) * tq:(2 * hp + 2) * tq, :] = jnp.where(lo_half, 0.0, pair).astype(BF16)
        w = wi_ref[...] * (IDX_DIM ** -0.5 * IDX_HEADS ** -0.5)
        for h in range(IDX_HEADS):
            wb_ref[h] = w[:, h:h + 1] + jnp.zeros((tq, LANES), F32)

        def score_chunk(kc, carry):
            off = pl.multiple_of(kc * tkc, tkc)
            for s in range(tkc // sub):
                kis = ki_ref[pl.ds(off + s * sub, sub), :].astype(BF16)
                r = lax.dot_general(qs_ref[...], kis, NT_DIMS, preferred_element_type=F32)
                halves = []
                for half in range(sub // LANES):
                    cs = slice(half * LANES, (half + 1) * LANES)
                    sc = jnp.zeros((tq, LANES), F32)
                    for h in range(IDX_HEADS):
                        sc = sc + wb_ref[h] * jnp.maximum(r[h * tq:(h + 1) * tq, cs], 0.0)
                    halves.append(sc)
                sc = jnp.concatenate(halves, axis=1)
                cols = slice(s * sub, (s + 1) * sub)
                sc = jnp.where(off + s * sub + col_sub <= row_sub, sc, NEG_INF)
                key_ref[kc, :, cols] = _sortable(sc)
            return carry

        lax.fori_loop(0, n_ch, score_chunk, 0)

        def count(pred):
            def body(kc, acc):
                off = pl.multiple_of(kc * tkc, tkc)
                part = jnp.where(pred(key_ref[kc], off), 1, 0).astype(jnp.int32)
                for s in range(tkc // LANES):
                    acc = acc + part[:, s * LANES:(s + 1) * LANES]
                return acc
            acc = lax.fori_loop(0, n_ch, body, jnp.zeros((tq, LANES), jnp.int32))
            return jnp.sum(acc, axis=1, keepdims=True)

        def unsettled(state):
            it, lo, hi, n_lo = state
            open_rows = jnp.where((n_lo == topk) | (lo == hi), 0, 1)
            return jnp.logical_and(it < 33, jnp.max(open_rows) > 0)

        def bisect(state):
            it, lo, hi, n_lo = state
            mid = (lo | hi) - ((lo ^ hi) >> 1)
            n_mid = count(lambda kk, off: kk >= mid)
            ok = n_mid >= topk
            return (it + 1, jnp.where(ok, mid, lo), jnp.where(ok, hi, mid - 1), jnp.where(ok, n_mid, n_lo))

        lo0 = jnp.full((tq, 1), INT_MIN, jnp.int32)
        hi0 = jnp.full((tq, 1), INT_MAX, jnp.int32)
        n0 = jnp.full((tq, 1), INT_MAX, jnp.int32)
        _, thr, _, n_ge = lax.while_loop(unsettled, bisect, (jnp.int32(0), lo0, hi0, n0))
        thr_ref[...] = thr
        mthr_ref[...] = jnp.full((tq, 1), INT_MAX, jnp.int32)

        @pl.when(jnp.max(n_ge) > topk)
        def _ties():
            need = topk - count(lambda kk, off: kk > thr)

            def bisect_idx(_, lohi):
                lo, hi = lohi
                mid = (lo + hi) >> 1
                ok = count(lambda kk, off: (kk == thr) & (off + col_iota <= mid)) >= need
                return jnp.where(ok, lo, mid + 1), jnp.where(ok, mid, hi)

            lo1 = jnp.zeros((tq, 1), jnp.int32)
            hi1 = jnp.full((tq, 1), 2 ** 30, jnp.int32)
            _, m_idx = lax.fori_loop(0, 31, bisect_idx, (lo1, hi1))
            mthr_ref[...] = m_idx

        m_ref[...] = jnp.full(m_ref.shape, M_FLOOR, F32)
        acc_ref[...] = jnp.zeros(acc_ref.shape, F32)

    @pl.when(c <= last_c)
    def _attend():
        off = pl.multiple_of(c * tkc, tkc)
        kk = key_ref[c]
        thr = thr_ref[...]
        pos = off + col_iota
        chosen = (kk > thr) | ((kk == thr) & (pos <= mthr_ref[...]))
        bias_ref[...] = jnp.where(chosen & (pos <= row_pos), 0.0, NEG_INF)
        ones = jnp.ones((tkc, HEAD_DIM), BF16)
        for h in range(B_HEADS):
            sl = slice(h * HEAD_DIM, (h + 1) * HEAD_DIM)
            s = lax.dot_general(q_ref[:, sl], k_ref[:, sl], NT_DIMS, preferred_element_type=F32) + bias_ref[...]
            m_old = m_ref[h]
            m_new = jnp.maximum(m_old, jnp.max(s, axis=-1, keepdims=True))
            p = jnp.exp2(s - m_new).astype(BF16)
            alpha = jnp.exp2(m_old - m_new)
            v_one = jnp.concatenate([v_ref[:, sl], ones], axis=1)
            acc_ref[h] = alpha * acc_ref[h] + jnp.dot(p, v_one, preferred_element_type=F32)
            m_ref[h] = m_new

    @pl.when(c == last_c)
    def _finish():
        for h in range(B_HEADS):
            a = acc_ref[h]
            o_ref[:, h * HEAD_DIM:(h + 1) * HEAD_DIM] = (a[:, :HEAD_DIM] / a[:, HEAD_DIM:]).astype(o_ref.dtype)


def _dsa(qi, kiwi, qk, v):
    t = qi.shape[0]
    tq = _pick(t, (256, 128))
    tkc = _pick(t, (1024, 512, 256))
    topk = min(DSA_TOPK, t // 4)
    assert tkc >= topk and t % tq == 0 and tkc % MXU_COLS == 0
    nq, nc = t // tq, t // tkc
    steps = [(i, c) for i in range(nq) for c in range(((i + 1) * tq - 1) // tkc + 1)]
    step_q = jnp.asarray([s[0] for s in steps], jnp.int32)
    step_c = jnp.asarray([s[1] for s in steps], jnp.int32)

    in_specs = [
        pl.BlockSpec((tq, IDX_WIDTH), lambda s, sq, sc: (sq[s], 0)),
        pl.BlockSpec((t, LANES), lambda s, sq, sc: (0, 0)),
        pl.BlockSpec((tq, LANES), lambda s, sq, sc: (sq[s], 1)),
        pl.BlockSpec((tq, B_WIDTH), lambda s, sq, sc: (sq[s], 0)),
        pl.BlockSpec((tkc, B_WIDTH), lambda s, sq, sc: (sc[s], 1)),
        pl.BlockSpec((tkc, B_WIDTH), lambda s, sq, sc: (sc[s], 0)),
    ]
    grid_spec = pltpu.PrefetchScalarGridSpec(
        num_scalar_prefetch=2,
        grid=(len(steps),),
        in_specs=in_specs,
        out_specs=pl.BlockSpec((tq, B_WIDTH), lambda s, sq, sc: (sq[s], 0)),
        scratch_shapes=[
            pltpu.VMEM((nc, tq, tkc), jnp.int32),
            pltpu.VMEM((tq, tkc), F32),
            pltpu.VMEM((IDX_HEADS * tq, LANES), BF16),
            pltpu.VMEM((IDX_HEADS, tq, LANES), F32),
            pltpu.VMEM((tq, 1), jnp.int32),
            pltpu.VMEM((tq, 1), jnp.int32),
            pltpu.VMEM((B_HEADS, tq, 1), F32),
            pltpu.VMEM((B_HEADS, tq, 2 * HEAD_DIM), F32),
        ],
    )
    return pl.pallas_call(
        functools.partial(_dsa_kernel, tq=tq, tkc=tkc, topk=topk),
        grid_spec=grid_spec,
        out_shape=jax.ShapeDtypeStruct((t, B_WIDTH), BF16),
        compiler_params=_params(("arbitrary",)),
        name="dsa",
    )(step_q, step_c, qi, kiwi, kiwi, qk, qk, v)


def _mem_kernel(q_ref, kv_ref, o_ref):
    scale = HEAD_DIM ** -0.5
    for h in range(MEM_HEADS):
        sl = slice(h * HEAD_DIM, (h + 1) * HEAD_DIM)
        vsl = slice(MEM_WIDTH + h * HEAD_DIM, MEM_WIDTH + (h + 1) * HEAD_DIM)
        s = lax.dot_general(q_ref[:, sl], kv_ref[:, sl], NT_DIMS, preferred_element_type=F32) * scale
        m = jnp.max(s, axis=-1, keepdims=True)
        e = jnp.exp(s - m)
        p = e / jnp.sum(e, axis=-1, keepdims=True)
        o_ref[:, sl] = jnp.dot(p.astype(BF16), kv_ref[:, vsl], preferred_element_type=F32).astype(o_ref.dtype)


def _mem_attention(q, kv):
    t = q.shape[0]
    mlen = kv.shape[0]
    tq = _pick(t, (512, 256, 128))
    return pl.pallas_call(
        _mem_kernel,
        grid=(t // tq,),
        in_specs=[pl.BlockSpec((tq, MEM_WIDTH), lambda i: (i, 0)),
                  pl.BlockSpec((mlen, 2 * MEM_WIDTH), lambda i: (0, 0))],
        out_specs=pl.BlockSpec((tq, MEM_WIDTH), lambda i: (i, 0)),
        out_shape=jax.ShapeDtypeStruct((t, MEM_WIDTH), BF16),
        compiler_params=_params(("parallel",)),
        name="mem_attention",
    )(q, kv)


def _cast_kernel(x_ref, o_ref):
    o_ref[...] = x_ref[...].astype(o_ref.dtype)


def _to_bf16(x):
    t, d = x.shape
    tm = _pick(t, (256, 128))
    spec = pl.BlockSpec((tm, d), lambda i: (i, 0))
    return pl.pallas_call(
        _cast_kernel, grid=(t // tm,), in_specs=[spec], out_specs=spec,
        out_shape=jax.ShapeDtypeStruct((t, d), BF16),
        compiler_params=_params(("parallel",)), name="cast_bf16",
    )(x)


def _rope_tables(t):
    pos = jnp.arange(t, dtype=jnp.int32).astype(F32)[:, None]
    half = HEAD_DIM // 2
    inv = ROPE_THETA ** (-jnp.arange(half, dtype=F32) / half)
    ang = pos * inv[None, :]
    cos128 = jnp.concatenate([jnp.cos(ang)] * 2, axis=1)
    sin128 = jnp.concatenate([-jnp.sin(ang), jnp.sin(ang)], axis=1)
    half = IDX_DIM // 2
    inv = ROPE_THETA ** (-jnp.arange(half, dtype=F32) / half)
    ang = pos * inv[None, :]
    z = jnp.zeros_like(ang)
    cos64 = jnp.concatenate([jnp.cos(ang)] * 4, axis=1)
    sin_lo = jnp.concatenate([-jnp.sin(ang), z] * 2, axis=1)
    sin_hi = jnp.concatenate([z, jnp.sin(ang)] * 2, axis=1)
    return (cos128, sin128), (cos64, sin_lo, sin_hi)


def _mixer_branches(xb, w_in_t, l, blk, rope128, rope64):
    t = xb.shape[0]
    proj = functools.partial(_panel_matmul, xb, w_in_t, l, trans_b=True)
    outs, lses, dils = [], [], []
    for g, (win, dil) in enumerate(DIL_GROUPS):
        assert win // dil == BAND
        qkv = proj(kind="a_group", extras=rope128, tn=COL_BLOCK, n_tiles=3, dil=dil,
                   src_block=lambda j, g=g: blk[0] + g + j * (A_WIDTH // COL_BLOCK), name=f"proj_a_g{g}")
        o, lse = _dilated_group(qkv, dil)
        outs.append(o)
        lses.append(lse)
        dils.append(dil)
    ya = _mix_groups(outs, lses, dils, t)
    qk_b = proj(kind="rope128_qk", extras=rope128, tn=COL_BLOCK, n_tiles=2 * B_WIDTH // COL_BLOCK,
                src_block=lambda j: blk[3] + j, name="proj_b_qk")
    v_b = proj(tn=COL_BLOCK, n_tiles=B_WIDTH // COL_BLOCK, src_block=lambda j: blk[5] + j, name="proj_b_v")
    qi = proj(kind="rope64", extras=rope64, out_dtype=F32, tn=COL_BLOCK, n_tiles=IDX_WIDTH // COL_BLOCK,
              src_block=lambda j: blk[6] + j, name="proj_qi")
    kiwi = proj(kind="kiwi", extras=rope64, out_dtype=F32, tn=LANES, n_tiles=1,
                src_block=lambda j: blk[7] * (COL_BLOCK // LANES), name="proj_kiwi")
    yb = _dsa(qi, kiwi, qk_b, v_b)
    return ya, yb


def kernel(x, mem, w_in, w_pa, w_pb, w_o, ln1_g, ln1_b, w_mq, w_mkv, w_mo, ln2_g, ln2_b, w_up, w_down, ln3_g, ln3_b):
    bsz, t, d = x.shape
    depth = w_in.shape[0]
    assert bsz == 1
    alpha = (2 * depth) ** 0.25

    w_in_t = jnp.swapaxes(w_in, 1, 2)
    sizes = (A_WIDTH, A_WIDTH, A_WIDTH, B_WIDTH, B_WIDTH, B_WIDTH, IDX_WIDTH, IDX_DIM, IDX_HEADS, d, d)
    offs = [0]
    for s in sizes:
        offs.append(offs[-1] + s)
    assert offs[-1] == w_in.shape[2] and all(o % COL_BLOCK == 0 for o in offs[:8])
    blk = [o // COL_BLOCK for o in offs[:8]]
    gate_shift = offs[9] - offs[7]
    assert 0 < gate_shift < LANES and (2 * d) % COL_BLOCK == 0
    lnp = [p.reshape(depth, 1, d) for p in (ln1_g, ln1_b, ln2_g, ln2_b, ln3_g, ln3_b)]

    (cos128, sin128), (cos64, sin_lo, sin_hi) = _rope_tables(t)
    tm_rows = _pick(t, (1024, 512, 256))

    xf = x[0]
    xb = _to_bf16(xf)
    memb = _to_bf16(mem[0])

    def row_tab(arr):
        return (arr, (tm_rows, LANES), lambda i, j: (i, 0))

    def tile(arr, tn, joff=0):
        return (arr, (tm_rows, tn), lambda i, j: (i, j + joff))

    rope128 = (row_tab(cos128), row_tab(sin128))
    rope64 = (row_tab(cos64), row_tab(sin_lo), row_tab(sin_hi))
    tn_d = d if d <= 1280 else _pick(d, (1024, 512, 256, 128))
    tn_p = _pick(d, (COL_BLOCK, 256, 128))

    for l in range(depth):
        ya, yb = _mixer_branches(xb, w_in_t, l, blk, rope128, rope64)
        gates = _panel_matmul(xb, w_in_t, l, kind="sigmoid", tn=COL_BLOCK, trans_b=True,
                              n_tiles=2 * d // COL_BLOCK, src_block=lambda j: blk[7] + j, shift=gate_shift,
                              name="proj_gate")
        merged = _gated_merge(ya, yb, w_pa, w_pb, gates, l)
        z = _panel_matmul(merged, w_o, l, kind="resid", extras=(tile(xf, tn_p),), alpha=alpha, out_dtype=F32,
                          tn=tn_p, n_tiles=d // tn_p, src_block=lambda j: j, name="mixer_out")
        xf, xb = _layer_norm(z, lnp[0], lnp[1], l)
        qm = _panel_matmul(xb, w_mq, l, tn=MEM_WIDTH, n_tiles=1, src_block=lambda j: j, name="mem_q")
        kvm = _matmul(memb, w_mkv, l, name="mem_kv")
        om = _mem_attention(qm, kvm)
        z = _matmul(om, w_mo, l, kind="resid", extras=(tile(xf, tn_d),), alpha=alpha, out_dtype=F32,
                    name="mem_out")
        xf, xb = _layer_norm(z, lnp[2], lnp[3], l)
        hdn = _panel_matmul(xb, w_up, l, kind="relu2", tn=tn_p, n_tiles=w_up.shape[2] // tn_p,
                            src_block=lambda j: j, name="mlp_up")
        z = _matmul(hdn, w_down, l, kind="resid", extras=(tile(xf, tn_d),), alpha=alpha, out_dtype=F32,
                    name="mlp_down")
        xf, xb = _layer_norm(z, lnp[4], lnp[5], l)
    return xf[None]
```

```python
import functools

import jax
import jax.numpy as jnp
from jax import lax
from jax.experimental import pallas as pl
from jax.experimental.pallas import tpu as pltpu

F32 = jnp.float32
BF16 = jnp.bfloat16

HEAD_DIM = 128
DIL_GROUPS = ((128, 1), (512, 4), (2048, 16))
A_HEADS_PER_GROUP = 4
A_GROUPS = len(DIL_GROUPS)
A_WIDTH = A_GROUPS * A_HEADS_PER_GROUP * HEAD_DIM
A_OUT = A_HEADS_PER_GROUP * HEAD_DIM
B_HEADS = 8
B_WIDTH = B_HEADS * HEAD_DIM
IDX_HEADS = 16
IDX_DIM = 64
IDX_WIDTH = IDX_HEADS * IDX_DIM
DSA_TOPK = 256
MEM_HEADS = 4
MEM_WIDTH = MEM_HEADS * HEAD_DIM
ROPE_THETA = 10000.0
LN_EPS = 1e-5
NEG_INF = -1e30
M_FLOOR = -1e29
BAND = 128
LOG2E = 1.4426950408889634

LANES = 128
MXU_COLS = 256
VMEM_LIMIT = 56 * 1024 * 1024
B_TILE_BYTES = 8 * 1024 * 1024
COL_BLOCK = 512

INT_MIN = -(2 ** 31)
INT_MAX = 2 ** 31 - 1
NT_DIMS = (((1,), (1,)), ((), ()))


def _pick(n, prefs):
    for p in prefs:
        if n % p == 0:
            return p
    return n


def _params(sem):
    return pltpu.CompilerParams(dimension_semantics=sem, vmem_limit_bytes=VMEM_LIMIT)


def _rope128(x, cos, sin):
    return x * cos + pltpu.roll(x, 64, 1) * sin


def _rope64(x, cos, sin_lo, sin_hi):
    return x * cos + pltpu.roll(x, 96, 1) * sin_lo + pltpu.roll(x, 32, 1) * sin_hi


def _rope128_tile(acc, cos, sin):
    return [_rope128(acc[:, c * LANES:(c + 1) * LANES], cos, sin) for c in range(acc.shape[1] // LANES)]


def _epilogue(kind, acc, extras, o_ref, stage_ref, dil, j, alpha):
    tn = acc.shape[1]
    if kind == "none":
        o_ref[...] = acc.astype(o_ref.dtype)
    elif kind == "resid":
        o_ref[...] = (alpha * extras[0][...] + acc).astype(o_ref.dtype)
    elif kind == "rope128":
        for c, y in enumerate(_rope128_tile(acc, extras[0][...], extras[1][...])):
            o_ref[:, c * LANES:(c + 1) * LANES] = y.astype(o_ref.dtype)
    elif kind == "rope128_qk":
        qs = jnp.where(j < B_WIDTH // COL_BLOCK, HEAD_DIM ** -0.5 * LOG2E, 1.0)
        for c, y in enumerate(_rope128_tile(acc, extras[0][...], extras[1][...])):
            o_ref[:, c * LANES:(c + 1) * LANES] = (y * qs).astype(o_ref.dtype)
    elif kind == "rope64":
        cos, s_lo, s_hi = extras[0][...], extras[1][...], extras[2][...]
        for c in range(tn // LANES):
            sl = slice(c * LANES, (c + 1) * LANES)
            o_ref[:, sl] = _rope64(acc[:, sl], cos, s_lo, s_hi).astype(o_ref.dtype)
    elif kind == "kiwi":
        cos, s_lo, s_hi = extras[0][...], extras[1][...], extras[2][...]
        o_ref[:, :LANES] = _rope64(acc[:, :LANES], cos, s_lo, s_hi).astype(o_ref.dtype)
        o_ref[:, LANES:] = acc[:, LANES:].astype(o_ref.dtype)
    elif kind == "a_group":
        cos = jnp.where(j < 2, extras[0][...], 1.0)
        sin = jnp.where(j < 2, extras[1][...], 0.0)
        for c, y in enumerate(_rope128_tile(acc, cos, sin)):
            stage_ref[c] = y

        rows = acc.shape[0] // dil
        for r in range(dil):
            for c in range(tn // LANES):
                src = stage_ref[c] if dil == 1 else stage_ref[c, pl.ds(r, rows, stride=dil), :]
                o_ref[r, :, c * LANES:(c + 1) * LANES] = src.astype(o_ref.dtype)
    elif kind == "sigmoid":
        o_ref[...] = (1.0 / (1.0 + jnp.exp(-acc))).astype(o_ref.dtype)
    elif kind == "relu2":
        r = jnp.maximum(acc, 0.0)
        o_ref[...] = (r * r).astype(o_ref.dtype)
    else:
        raise ValueError(kind)


def _mm_kernel(*refs, nk, kind, n_extra, dil, alpha):
    a_ref, b_ref = refs[0], refs[1]
    extras = refs[2:2 + n_extra]
    o_ref = refs[2 + n_extra]
    scratch = list(refs[3 + n_extra:])
    acc_ref = scratch.pop(0) if nk > 1 else None
    stage_ref = scratch.pop(0) if kind == "a_group" else None
    j = pl.program_id(1)
    k = pl.program_id(2)
    if nk == 1:
        part = jnp.dot(a_ref[...], b_ref[...].astype(BF16), preferred_element_type=F32)
        _epilogue(kind, part, extras, o_ref, stage_ref, dil, j, alpha)
        return

    @pl.when(k == 0)
    def _():
        acc_ref[...] = jnp.zeros(acc_ref.shape, F32)

    acc_ref[...] += jnp.dot(a_ref[...], b_ref[...].astype(BF16), preferred_element_type=F32)

    @pl.when(k == nk - 1)
    def _():
        _epilogue(kind, acc_ref[...], extras, o_ref, stage_ref, dil, j, alpha)


def _matmul(a, b, layer, *, kind="none", extras=(), out_dtype=BF16, name="mm",
            tn=None, n_tiles=None, src_block=None, dil=1, alpha=None):
    m, k = a.shape
    nb = b.shape[2]
    tm = _pick(m, (1024, 512, 256))
    if tn is None:
        tn = nb if nb <= 1280 else _pick(nb, (1024, 512, 256, 128))
    if n_tiles is None:
        n_tiles = nb // tn
    if src_block is None:
        src_block = lambda j: j
    itemsize = jnp.dtype(b.dtype).itemsize
    tk = k
    for cand in (k, 4096, 2048, 1024, 512):
        if cand <= k and k % cand == 0:
            tk = cand
            if cand * tn * itemsize <= B_TILE_BYTES:
                break
    nk = k // tk
    assert m % tm == 0 and k % tk == 0 and tm % dil == 0
    in_specs = [
        pl.BlockSpec((tm, tk), lambda i, j, kk: (i, kk)),
        pl.BlockSpec((None, tk, tn), lambda i, j, kk: (layer, kk, src_block(j))),
    ]
    ops = [a, b]
    for arr, bshape, imap in extras:
        in_specs.append(pl.BlockSpec(bshape, functools.partial(lambda i, j, kk, f: f(i, j), f=imap)))
        ops.append(arr)
    scratch = [pltpu.VMEM((tm, tn), F32)] if nk > 1 else []
    if kind == "a_group":
        scratch.append(pltpu.VMEM((tn // LANES, tm, LANES), F32))
        out_shape = jax.ShapeDtypeStruct((n_tiles, dil, m // dil, tn), out_dtype)
        out_spec = pl.BlockSpec((None, dil, tm // dil, tn), lambda i, j, kk: (j, 0, i, 0))
    else:
        out_shape = jax.ShapeDtypeStruct((m, n_tiles * tn), out_dtype)
        out_spec = pl.BlockSpec((tm, tn), lambda i, j, kk: (i, j))
    return pl.pallas_call(
        functools.partial(_mm_kernel, nk=nk, kind=kind, n_extra=len(extras), dil=dil, alpha=alpha),
        grid=(m // tm, n_tiles, nk),
        in_specs=in_specs,
        out_specs=out_spec,
        out_shape=out_shape,
        scratch_shapes=scratch,
        compiler_params=_params(("parallel", "parallel", "arbitrary")),
        name=name,
    )(*ops)


def _panel_kernel(*refs, kind, n_extra, dil, shift, trans_b, tn, alpha):
    a_ref, b_ref = refs[0], refs[1]
    n_b = 2 if shift else 1
    extras = refs[1 + n_b:1 + n_b + n_extra]
    o_ref = refs[1 + n_b + n_extra]
    scratch = list(refs[2 + n_b + n_extra:])
    bb_ref = scratch.pop(0)
    stage_ref = scratch.pop(0) if kind == "a_group" else None
    j = pl.program_id(0)
    i = pl.program_id(1)

    @pl.when(i == 0)
    def _():
        w = b_ref[...].astype(BF16)
        if kind == "kiwi":
            q = IDX_DIM
            bb_ref[0:q, :] = w[0:q]
            bb_ref[q:2 * q, :] = w[0:q]
            bb_ref[2 * q:3 * q, :] = w[q:2 * q]
            bb_ref[3 * q:4 * q, :] = w[q:2 * q]
        elif trans_b:
            bb_ref[:b_ref.shape[0], :] = w
            if shift:
                bb_ref[b_ref.shape[0]:, :] = refs[2][...].astype(BF16)
        else:
            bb_ref[...] = w

    if trans_b:
        acc = lax.dot_general(a_ref[...], bb_ref[shift:shift + tn, :], NT_DIMS, preferred_element_type=F32)
    else:
        acc = jnp.dot(a_ref[...], bb_ref[...], preferred_element_type=F32)
    _epilogue(kind, acc, extras, o_ref, stage_ref, dil, j, alpha)


def _panel_matmul(a, b, layer, *, tn, n_tiles, src_block, kind="none", extras=(), out_dtype=BF16,
                  name="mm", dil=1, shift=0, trans_b=False, alpha=None):
    m, k = a.shape
    tm = _pick(m, (1024, 512, 256))
    assert m % tm == 0 and tm % dil == 0 and tn % LANES == 0
    assert shift == 0 or (trans_b and shift % 16 == 0 and shift < LANES)
    out_tn = 2 * tn if kind == "kiwi" else tn
    b_rows = tn + (LANES if shift else 0)
    in_specs = [pl.BlockSpec((tm, k), lambda j, i: (i, 0))]
    ops = [a, b]
    if trans_b:
        in_specs.append(pl.BlockSpec((None, tn, k), lambda j, i: (layer, src_block(j), 0)))
        if shift:
            per = tn // LANES
            in_specs.append(pl.BlockSpec((None, LANES, k), lambda j, i: (layer, (src_block(j) + 1) * per, 0)))
            ops.append(b)
        scratch = [pltpu.VMEM((out_tn if kind == "kiwi" else b_rows, k), BF16)]
    else:
        in_specs.append(pl.BlockSpec((None, k, tn), lambda j, i: (layer, 0, src_block(j))))
        scratch = [pltpu.VMEM((k, tn), BF16)]
    for arr, bshape, imap in extras:
        in_specs.append(pl.BlockSpec(bshape, functools.partial(lambda j, i, f: f(i, j), f=imap)))
        ops.append(arr)
    if kind == "a_group":
        scratch.append(pltpu.VMEM((tn // LANES, tm, LANES), F32))
        out_shape = jax.ShapeDtypeStruct((n_tiles, dil, m // dil, tn), out_dtype)
        out_spec = pl.BlockSpec((None, dil, tm // dil, tn), lambda j, i: (j, 0, i, 0))
    else:
        out_shape = jax.ShapeDtypeStruct((m, n_tiles * out_tn), out_dtype)
        out_spec = pl.BlockSpec((tm, out_tn), lambda j, i: (i, j))
    return pl.pallas_call(
        functools.partial(_panel_kernel, kind=kind, n_extra=len(extras), dil=dil, shift=shift,
                          trans_b=trans_b, tn=out_tn, alpha=alpha),
        grid=(n_tiles, m // tm),
        in_specs=in_specs,
        out_specs=out_spec,
        out_shape=out_shape,
        scratch_shapes=scratch,
        compiler_params=_params(("arbitrary", "arbitrary")),
        name=name,
    )(*ops)


def _ln_kernel(z_ref, g_ref, b_ref, o_ref, ob_ref):
    z = z_ref[...]
    mu = jnp.mean(z, axis=-1, keepdims=True)
    zc = z - mu
    var = jnp.mean(zc * zc, axis=-1, keepdims=True)
    out = (zc * lax.rsqrt(var + LN_EPS)) * g_ref[...] + b_ref[...]
    o_ref[...] = out
    ob_ref[...] = out.astype(BF16)


def _layer_norm(z, g, b, layer):
    t, d = z.shape
    tm = _pick(t, (512, 256, 128))
    row = pl.BlockSpec((tm, d), lambda i: (i, 0))
    par = pl.BlockSpec((None, 1, d), lambda i: (layer, 0, 0))
    return pl.pallas_call(
        _ln_kernel,
        grid=(t // tm,),
        in_specs=[row, par, par],
        out_specs=[row, row],
        out_shape=[jax.ShapeDtypeStruct((t, d), F32), jax.ShapeDtypeStruct((t, d), BF16)],
        compiler_params=_params(("parallel",)),
        name="layer_norm",
    )(z, g, b)


def _merge_kernel(ya_ref, wa_ref, yb_ref, wb_ref, ga_ref, gb_ref, o_ref):
    pa = jnp.dot(ya_ref[...], wa_ref[...].astype(BF16), preferred_element_type=F32)
    pb = jnp.dot(yb_ref[...], wb_ref[...].astype(BF16), preferred_element_type=F32)
    o_ref[...] = (ga_ref[...].astype(F32) * pa + gb_ref[...].astype(F32) * pb).astype(o_ref.dtype)


def _gated_merge(ya, yb, w_pa, w_pb, gates, layer):
    t = ya.shape[0]
    d = w_pa.shape[2]
    tm = _pick(t, (1024, 512, 256))
    tn = d if d <= 1280 else _pick(d, (1024, 512, 256, 128))
    ka, kb = ya.shape[1], yb.shape[1]
    return pl.pallas_call(
        _merge_kernel,
        grid=(t // tm, d // tn),
        in_specs=[
            pl.BlockSpec((tm, ka), lambda i, j: (i, 0)),
            pl.BlockSpec((None, ka, tn), lambda i, j: (layer, 0, j)),
            pl.BlockSpec((tm, kb), lambda i, j: (i, 0)),
            pl.BlockSpec((None, kb, tn), lambda i, j: (layer, 0, j)),
            pl.BlockSpec((tm, tn), lambda i, j: (i, j)),
            pl.BlockSpec((tm, tn), lambda i, j: (i, j + d // tn)),
        ],
        out_specs=pl.BlockSpec((tm, tn), lambda i, j: (i, j)),
        out_shape=jax.ShapeDtypeStruct((t, d), BF16),
        compiler_params=_params(("parallel", "parallel")),
        name="gated_merge",
    )(ya, w_pa, yb, w_pb, gates, gates)


def _dil_kernel(q_ref, kp_ref, kc_ref, vp_ref, vc_ref, o_ref, lse_ref, *, nsub):
    i = pl.program_id(1)
    qrows = min(nsub, 2) * BAND
    qq = lax.broadcasted_iota(jnp.int32, (qrows, BAND + qrows), 0)
    kk = lax.broadcasted_iota(jnp.int32, (qrows, BAND + qrows), 1)
    in_band = (kk >= qq) & (kk <= qq + BAND)
    bias = jnp.where(in_band, 0.0, NEG_INF)
    bias_first = jnp.where(in_band & (kk >= jnp.where(i > 0, 0, BAND)), 0.0, NEG_INF)
    scale = HEAD_DIM ** -0.5
    chains = [(g, h) for g in range(nsub * BAND // qrows) for h in range(A_HEADS_PER_GROUP)]
    scores, v_bands = [], []
    for g, h in chains:
        rows = slice(g * qrows, (g + 1) * qrows)
        band = slice(g * qrows - BAND, (g + 1) * qrows)
        sl = slice(h * HEAD_DIM, (h + 1) * HEAD_DIM)
        if g == 0:
            k_band = jnp.concatenate([kp_ref[:, sl], kc_ref[rows, sl]], axis=0)
            v_bands.append(jnp.concatenate([vp_ref[:, sl], vc_ref[rows, sl]], axis=0))
        else:
            k_band = kc_ref[band, sl]
            v_bands.append(vc_ref[band, sl])
        s = lax.dot_general(q_ref[rows, sl], k_band, NT_DIMS, preferred_element_type=F32) * scale
        scores.append(s + (bias_first if g == 0 else bias))
    maxes = [jnp.max(s, axis=-1, keepdims=True) for s in scores]
    exps = [jnp.exp(s - m) for s, m in zip(scores, maxes)]
    unnorm = [jnp.dot(e.astype(BF16), v, preferred_element_type=F32) for e, v in zip(exps, v_bands)]
    sums = [jnp.sum(e, axis=-1, keepdims=True) for e in exps]
    outs = [o * (1.0 / l) for o, l in zip(unnorm, sums)]
    lses = [jnp.broadcast_to(m + jnp.log(l), (qrows, HEAD_DIM)) for m, l in zip(maxes, sums)]
    nh = A_HEADS_PER_GROUP
    ngrp = len(outs) // nh
    o_ref[...] = jnp.concatenate(
        [jnp.concatenate(outs[g * nh:(g + 1) * nh], axis=1) for g in range(ngrp)], axis=0)
    lse_ref[...] = jnp.concatenate(
        [jnp.concatenate(lses[g * nh:(g + 1) * nh], axis=1) for g in range(ngrp)], axis=0)


def _dilated_group(qkv, dil):
    sub = qkv.shape[2]
    assert sub % BAND == 0
    nsub = _pick(sub // BAND, (4, 2, 1))
    rows = nsub * BAND

    def spec(which):
        return pl.BlockSpec((None, None, rows, A_OUT), lambda r, i: (which, r, i, 0))

    def prev_spec(which):
        return pl.BlockSpec((None, None, BAND, A_OUT),
                            lambda r, i: (which, r, jnp.maximum(i * nsub - 1, 0), 0))

    out_spec = pl.BlockSpec((None, rows, A_OUT), lambda r, i: (r, i, 0))
    return pl.pallas_call(
        functools.partial(_dil_kernel, nsub=nsub),
        grid=(dil, sub // rows),
        in_specs=[spec(0), prev_spec(1), spec(1), prev_spec(2), spec(2)],
        out_specs=[out_spec, out_spec],
        out_shape=[jax.ShapeDtypeStruct((dil, sub, A_OUT), F32)] * 2,
        compiler_params=_params(("parallel", "parallel")),
        name=f"dilated_attn_d{dil}",
    )(qkv, qkv, qkv, qkv, qkv)


def _mix_kernel(*refs, dils):
    n = len(dils)
    o_refs, l_refs, y_ref = refs[:n], refs[n:2 * n], refs[2 * n]
    stage = refs[2 * n + 1:]
    tm = y_ref.shape[0]

    def natural(ref, slot, dil, cs):
        if dil == 1:
            return ref[0, :, cs]
        for r in range(dil):
            slot[pl.ds(r, tm // dil, stride=dil), :] = ref[r, :, cs]
        return slot[...]

    for c in range(A_OUT // LANES):
        cs = slice(c * LANES, (c + 1) * LANES)
        outs = [natural(o_refs[g], stage[2 * g], dils[g], cs) for g in range(n)]
        lses = [natural(l_refs[g], stage[2 * g + 1], dils[g], cs) for g in range(n)]
        m = functools.reduce(jnp.maximum, lses)
        es = [jnp.exp(l - m) for l in lses]
        inv = 1.0 / functools.reduce(lambda a, b: a + b, es)
        y = functools.reduce(lambda a, b: a + b, [(e * inv) * o for e, o in zip(es, outs)])
        y_ref[:, cs] = y.astype(y_ref.dtype)


def _mix_groups(outs, lses, dils, t):
    tm = _pick(t, (512, 256))
    specs = [pl.BlockSpec((d, tm // d, A_OUT), lambda i: (0, i, 0)) for d in dils]
    return pl.pallas_call(
        functools.partial(_mix_kernel, dils=tuple(dils)),
        grid=(t // tm,),
        in_specs=specs + specs,
        out_specs=pl.BlockSpec((tm, A_OUT), lambda i: (i, 0)),
        out_shape=jax.ShapeDtypeStruct((t, A_OUT), BF16),
        scratch_shapes=[pltpu.VMEM((tm, LANES), F32)] * (2 * len(dils)),
        compiler_params=_params(("parallel",)),
        name="mix_groups",
    )(*outs, *lses)


def _sortable(x):
    b = pltpu.bitcast(x, jnp.int32)
    return b ^ ((b >> 31) & jnp.int32(INT_MAX))


def _dsa_kernel(step_q_ref, step_c_ref, qi_ref, ki_ref, wi_ref, q_ref, k_ref, v_ref, o_ref,
                key_ref, bias_ref, qs_ref, wb_ref, thr_ref, mthr_ref, m_ref, acc_ref,
                *, tq, tkc, topk):
    i = step_q_ref[pl.program_id(0)]
    c = step_c_ref[pl.program_id(0)]
    last_c = ((i + 1) * tq - 1) // tkc
    n_ch = last_c + 1
    row_pos = i * tq + lax.broadcasted_iota(jnp.int32, (tq, tkc), 0)
    col_iota = lax.broadcasted_iota(jnp.int32, (tq, tkc), 1)
    sub = MXU_COLS
    row_sub = i * tq + lax.broadcasted_iota(jnp.int32, (tq, sub), 0)
    col_sub = lax.broadcasted_iota(jnp.int32, (tq, sub), 1)

    @pl.when(c == 0)
    def _select():
        lo_half = lax.broadcasted_iota(jnp.int32, (tq, LANES), 1) < IDX_DIM
        for hp in range(IDX_HEADS // 2):
            pair = qi_ref[:, hp * LANES:(hp + 1) * LANES]
            qs_ref[(2 * hp) * tq:(2 * hp + 1) * tq, :] = jnp.where(lo_half, pair, 0.0).astype(BF16)
            qs_ref[(2 * hp + 1) * tq:(2 * hp + 2) * tq, :] = jnp.where(lo_half, 0.0, pair).astype(BF16)
        w = wi_ref[...] * (IDX_DIM ** -0.5 * IDX_HEADS ** -0.5)
        for h in range(IDX_HEADS):
            wb_ref[h] = w[:, h:h + 1] + jnp.zeros((tq, LANES), F32)

        def score_chunk(kc, carry):
            off = pl.multiple_of(kc * tkc, tkc)
            for s in range(tkc // sub):
                kis = ki_ref[pl.ds(off + s * sub, sub), :].astype(BF16)
                r = lax.dot_general(qs_ref[...], kis, NT_DIMS, preferred_element_type=F32)
                halves = []
                for half in range(sub // LANES):
                    cs = slice(half * LANES, (half + 1) * LANES)
                    sc = jnp.zeros((tq, LANES), F32)
                    for h in range(IDX_HEADS):
                        sc = sc + wb_ref[h] * jnp.maximum(r[h * tq:(h + 1) * tq, cs], 0.0)
                    halves.append(sc)
                sc = jnp.concatenate(halves, axis=1)
                cols = slice(s * sub, (s + 1) * sub)
                sc = jnp.where(off + s * sub + col_sub <= row_sub, sc, NEG_INF)
                key_ref[kc, :, cols] = _sortable(sc)
            return carry

        lax.fori_loop(0, n_ch, score_chunk, 0)

        def count(pred):
            def body(kc, acc):
                off = pl.multiple_of(kc * tkc, tkc)
                part = jnp.where(pred(key_ref[kc], off), 1, 0).astype(jnp.int32)
                for s in range(tkc // LANES):
                    acc = acc + part[:, s * LANES:(s + 1) * LANES]
                return acc
            acc = lax.fori_loop(0, n_ch, body, jnp.zeros((tq, LANES), jnp.int32))
            return jnp.sum(acc, axis=1, keepdims=True)

        def unsettled(state):
            it, lo, hi, n_lo = state
            open_rows = jnp.where((n_lo == topk) | (lo == hi), 0, 1)
            return jnp.logical_and(it < 33, jnp.max(open_rows) > 0)

        def bisect(state):
            it, lo, hi, n_lo = state
            mid = (lo | hi) - ((lo ^ hi) >> 1)
            n_mid = count(lambda kk, off: kk >= mid)
            ok = n_mid >= topk
            return (it + 1, jnp.where(ok, mid, lo), jnp.where(ok, hi, mid - 1), jnp.where(ok, n_mid, n_lo))

        lo0 = jnp.full((tq, 1), INT_MIN, jnp.int32)
        hi0 = jnp.full((tq, 1), INT_MAX, jnp.int32)
        n0 = jnp.full((tq, 1), INT_MAX, jnp.int32)
        _, thr, _, n_ge = lax.while_loop(unsettled, bisect, (jnp.int32(0), lo0, hi0, n0))
        thr_ref[...] = thr
        mthr_ref[...] = jnp.full((tq, 1), INT_MAX, jnp.int32)

        @pl.when(jnp.max(n_ge) > topk)
        def _ties():
            need = topk - count(lambda kk, off: kk > thr)

            def bisect_idx(_, lohi):
                lo, hi = lohi
                mid = (lo + hi) >> 1
                ok = count(lambda kk, off: (kk == thr) & (off + col_iota <= mid)) >= need
                return jnp.where(ok, lo, mid + 1), jnp.where(ok, mid, hi)

            lo1 = jnp.zeros((tq, 1), jnp.int32)
            hi1 = jnp.full((tq, 1), 2 ** 30, jnp.int32)
            _, m_idx = lax.fori_loop(0, 31, bisect_idx, (lo1, hi1))
            mthr_ref[...] = m_idx

        m_ref[...] = jnp.full(m_ref.shape, M_FLOOR, F32)
        acc_ref[...] = jnp.zeros(acc_ref.shape, F32)

    @pl.when(c <= last_c)
    def _attend():
        off = pl.multiple_of(c * tkc, tkc)
        kk = key_ref[c]
        thr = thr_ref[...]
        pos = off + col_iota
        chosen = (kk > thr) | ((kk == thr) & (pos <= mthr_ref[...]))
        bias_ref[...] = jnp.where(chosen & (pos <= row_pos), 0.0, NEG_INF)
        ones = jnp.ones((tkc, HEAD_DIM), BF16)
        for h in range(B_HEADS):
            sl = slice(h * HEAD_DIM, (h + 1) * HEAD_DIM)
            s = lax.dot_general(q_ref[:, sl], k_ref[:, sl], NT_DIMS, preferred_element_type=F32) + bias_ref[...]
            m_old = m_ref[h]
            m_new = jnp.maximum(m_old, jnp.max(s, axis=-1, keepdims=True))
            p = jnp.exp2(s - m_new).astype(BF16)
            alpha = jnp.exp2(m_old - m_new)
            v_one = jnp.concatenate([v_ref[:, sl], ones], axis=1)
            acc_ref[h] = alpha * acc_ref[h] + jnp.dot(p, v_one, preferred_element_type=F32)
            m_ref[h] = m_new

    @pl.when(c == last_c)
    def _finish():
        for h in range(B_HEADS):
            a = acc_ref[h]
            o_ref[:, h * HEAD_DIM:(h + 1) * HEAD_DIM] = (a[:, :HEAD_DIM] / a[:, HEAD_DIM:]).astype(o_ref.dtype)


def _dsa(qi, kiwi, qk, v):
    t = qi.shape[0]
    tq = _pick(t, (256, 128))
    tkc = _pick(t, (1024, 512, 256))
    topk = min(DSA_TOPK, t // 4)
    assert tkc >= topk and t % tq == 0 and tkc % MXU_COLS == 0
    nq, nc = t // tq, t // tkc
    steps = [(i, c) for i in range(nq) for c in range(((i + 1) * tq - 1) // tkc + 1)]
    step_q = jnp.asarray([s[0] for s in steps], jnp.int32)
    step_c = jnp.asarray([s[1] for s in steps], jnp.int32)

    in_specs = [
        pl.BlockSpec((tq, IDX_WIDTH), lambda s, sq, sc: (sq[s], 0)),
        pl.BlockSpec((t, LANES), lambda s, sq, sc: (0, 0)),
        pl.BlockSpec((tq, LANES), lambda s, sq, sc: (sq[s], 1)),
        pl.BlockSpec((tq, B_WIDTH), lambda s, sq, sc: (sq[s], 0)),
        pl.BlockSpec((tkc, B_WIDTH), lambda s, sq, sc: (sc[s], 1)),
        pl.BlockSpec((tkc, B_WIDTH), lambda s, sq, sc: (sc[s], 0)),
    ]
    grid_spec = pltpu.PrefetchScalarGridSpec(
        num_scalar_prefetch=2,
        grid=(len(steps),),
        in_specs=in_specs,
        out_specs=pl.BlockSpec((tq, B_WIDTH), lambda s, sq, sc: (sq[s], 0)),
        scratch_shapes=[
            pltpu.VMEM((nc, tq, tkc), jnp.int32),
            pltpu.VMEM((tq, tkc), F32),
            pltpu.VMEM((IDX_HEADS * tq, LANES), BF16),
            pltpu.VMEM((IDX_HEADS, tq, LANES), F32),
            pltpu.VMEM((tq, 1), jnp.int32),
            pltpu.VMEM((tq, 1), jnp.int32),
            pltpu.VMEM((B_HEADS, tq, 1), F32),
            pltpu.VMEM((B_HEADS, tq, 2 * HEAD_DIM), F32),
        ],
    )
    return pl.pallas_call(
        functools.partial(_dsa_kernel, tq=tq, tkc=tkc, topk=topk),
        grid_spec=grid_spec,
        out_shape=jax.ShapeDtypeStruct((t, B_WIDTH), BF16),
        compiler_params=_params(("arbitrary",)),
        name="dsa",
    )(step_q, step_c, qi, kiwi, kiwi, qk, qk, v)


def _mem_layer_kernel(xb_ref, x_ref, wq_ref, kv_ref, wo_ref, g_ref, b_ref, o_ref, ob_ref, *, alpha):
    scale = HEAD_DIM ** -0.5
    q = jnp.dot(xb_ref[...], wq_ref[...], preferred_element_type=F32).astype(BF16)
    heads = []
    for h in range(MEM_HEADS):
        sl = slice(h * HEAD_DIM, (h + 1) * HEAD_DIM)
        vsl = slice(MEM_WIDTH + h * HEAD_DIM, MEM_WIDTH + (h + 1) * HEAD_DIM)
        s = lax.dot_general(q[:, sl], kv_ref[:, sl], NT_DIMS, preferred_element_type=F32) * scale
        m = jnp.max(s, axis=-1, keepdims=True)
        e = jnp.exp(s - m)
        p = e / jnp.sum(e, axis=-1, keepdims=True)
        heads.append(jnp.dot(p.astype(BF16), kv_ref[:, vsl], preferred_element_type=F32))
    om = jnp.concatenate(heads, axis=1).astype(BF16)
    z = alpha * x_ref[...] + jnp.dot(om, wo_ref[...], preferred_element_type=F32)
    mu = jnp.mean(z, axis=-1, keepdims=True)
    zc = z - mu
    var = jnp.mean(zc * zc, axis=-1, keepdims=True)
    out = (zc * lax.rsqrt(var + LN_EPS)) * g_ref[...] + b_ref[...]
    o_ref[...] = out
    ob_ref[...] = out.astype(BF16)


def _mem_layer(xb, x, wq, kv, wo, g, b, layer, alpha):
    t, d = x.shape
    mlen = kv.shape[0]
    tm = _pick(t, (256, 128))
    row = lambda w: pl.BlockSpec((tm, w), lambda i: (i, 0))
    whole = lambda a: pl.BlockSpec(a.shape, lambda i: (0, 0))
    par = pl.BlockSpec((None, 1, d), lambda i: (layer, 0, 0))
    return pl.pallas_call(
        functools.partial(_mem_layer_kernel, alpha=alpha),
        grid=(t // tm,),
        in_specs=[row(d), row(d), whole(wq), whole(kv), whole(wo), par, par],
        out_specs=[row(d), row(d)],
        out_shape=[jax.ShapeDtypeStruct((t, d), F32), jax.ShapeDtypeStruct((t, d), BF16)],
        compiler_params=_params(("parallel",)),
        name="mem_layer",
    )(xb, x, wq, kv, wo, g, b)


def _cast_layer_kernel(w_ref, o_ref):
    o_ref[...] = w_ref[...].astype(o_ref.dtype)


def _layer_to_bf16(w, layer):
    _, r, c = w.shape
    tr = _pick(r, (512, 256, 128))
    return pl.pallas_call(
        _cast_layer_kernel,
        grid=(r // tr,),
        in_specs=[pl.BlockSpec((None, tr, c), lambda i: (layer, i, 0))],
        out_specs=pl.BlockSpec((tr, c), lambda i: (i, 0)),
        out_shape=jax.ShapeDtypeStruct((r, c), BF16),
        compiler_params=_params(("parallel",)),
        name="cast_layer",
    )(w)


def _cast_kernel(x_ref, o_ref):
    o_ref[...] = x_ref[...].astype(o_ref.dtype)


def _to_bf16(x):
    t, d = x.shape
    tm = _pick(t, (256, 128))
    spec = pl.BlockSpec((tm, d), lambda i: (i, 0))
    return pl.pallas_call(
        _cast_kernel, grid=(t // tm,), in_specs=[spec], out_specs=spec,
        out_shape=jax.ShapeDtypeStruct((t, d), BF16),
        compiler_params=_params(("parallel",)), name="cast_bf16",
    )(x)


def _rope_tables(t):
    pos = jnp.arange(t, dtype=jnp.int32).astype(F32)[:, None]
    half = HEAD_DIM // 2
    inv = ROPE_THETA ** (-jnp.arange(half, dtype=F32) / half)
    ang = pos * inv[None, :]
    cos128 = jnp.concatenate([jnp.cos(ang)] * 2, axis=1)
    sin128 = jnp.concatenate([-jnp.sin(ang), jnp.sin(ang)], axis=1)
    half = IDX_DIM // 2
    inv = ROPE_THETA ** (-jnp.arange(half, dtype=F32) / half)
    ang = pos * inv[None, :]
    z = jnp.zeros_like(ang)
    cos64 = jnp.concatenate([jnp.cos(ang)] * 4, axis=1)
    sin_lo = jnp.concatenate([-jnp.sin(ang), z] * 2, axis=1)
    sin_hi = jnp.concatenate([z, jnp.sin(ang)] * 2, axis=1)
    return (cos128, sin128), (cos64, sin_lo, sin_hi)


def _mixer_branches(xb, w_in_t, l, blk, rope128, rope64):
    t = xb.shape[0]
    proj = functools.partial(_panel_matmul, xb, w_in_t, l, trans_b=True)
    outs, lses, dils = [], [], []
    for g, (win, dil) in enumerate(DIL_GROUPS):
        assert win // dil == BAND
        qkv = proj(kind="a_group", extras=rope128, tn=COL_BLOCK, n_tiles=3, dil=dil,
                   src_block=lambda j, g=g: blk[0] + g + j * (A_WIDTH // COL_BLOCK), name=f"proj_a_g{g}")
        o, lse = _dilated_group(qkv, dil)
        outs.append(o)
        lses.append(lse)
        dils.append(dil)
    ya = _mix_groups(outs, lses, dils, t)
    qk_b = proj(kind="rope128_qk", extras=rope128, tn=COL_BLOCK, n_tiles=2 * B_WIDTH // COL_BLOCK,
                src_block=lambda j: blk[3] + j, name="proj_b_qk")
    v_b = proj(tn=COL_BLOCK, n_tiles=B_WIDTH // COL_BLOCK, src_block=lambda j: blk[5] + j, name="proj_b_v")
    qi = proj(kind="rope64", extras=rope64, out_dtype=F32, tn=COL_BLOCK, n_tiles=IDX_WIDTH // COL_BLOCK,
              src_block=lambda j: blk[6] + j, name="proj_qi")
    kiwi = proj(kind="kiwi", extras=rope64, out_dtype=F32, tn=LANES, n_tiles=1,
                src_block=lambda j: blk[7] * (COL_BLOCK // LANES), name="proj_kiwi")
    yb = _dsa(qi, kiwi, qk_b, v_b)
    return ya, yb


def kernel(x, mem, w_in, w_pa, w_pb, w_o, ln1_g, ln1_b, w_mq, w_mkv, w_mo, ln2_g, ln2_b, w_up, w_down, ln3_g, ln3_b):
    bsz, t, d = x.shape
    depth = w_in.shape[0]
    assert bsz == 1
    alpha = (2 * depth) ** 0.25

    w_in_t = jnp.swapaxes(w_in, 1, 2)
    sizes = (A_WIDTH, A_WIDTH, A_WIDTH, B_WIDTH, B_WIDTH, B_WIDTH, IDX_WIDTH, IDX_DIM, IDX_HEADS, d, d)
    offs = [0]
    for s in sizes:
        offs.append(offs[-1] + s)
    assert offs[-1] == w_in.shape[2] and all(o % COL_BLOCK == 0 for o in offs[:8])
    blk = [o // COL_BLOCK for o in offs[:8]]
    gate_shift = offs[9] - offs[7]
    assert 0 < gate_shift < LANES and (2 * d) % COL_BLOCK == 0
    lnp = [p.reshape(depth, 1, d) for p in (ln1_g, ln1_b, ln2_g, ln2_b, ln3_g, ln3_b)]

    (cos128, sin128), (cos64, sin_lo, sin_hi) = _rope_tables(t)
    tm_rows = _pick(t, (1024, 512, 256))

    xf = x[0]
    xb = _to_bf16(xf)
    memb = _to_bf16(mem[0])

    def row_tab(arr):
        return (arr, (tm_rows, LANES), lambda i, j: (i, 0))

    def tile(arr, tn, joff=0):
        return (arr, (tm_rows, tn), lambda i, j: (i, j + joff))

    rope128 = (row_tab(cos128), row_tab(sin128))
    rope64 = (row_tab(cos64), row_tab(sin_lo), row_tab(sin_hi))
    tn_d = d if d <= 1280 else _pick(d, (1024, 512, 256, 128))
    tn_p = _pick(d, (COL_BLOCK, 256, 128))

    for l in range(depth):
        ya, yb = _mixer_branches(xb, w_in_t, l, blk, rope128, rope64)
        gates = _panel_matmul(xb, w_in_t, l, kind="sigmoid", tn=COL_BLOCK, trans_b=True,
                              n_tiles=2 * d // COL_BLOCK, src_block=lambda j: blk[7] + j, shift=gate_shift,
                              name="proj_gate")
        merged = _gated_merge(ya, yb, w_pa, w_pb, gates, l)
        z = _panel_matmul(merged, w_o, l, kind="resid", extras=(tile(xf, tn_p),), alpha=alpha, out_dtype=F32,
                          tn=tn_p, n_tiles=d // tn_p, src_block=lambda j: j, name="mixer_out")
        xf, xb = _layer_norm(z, lnp[0], lnp[1], l)
        kvm = _matmul(memb, w_mkv, l, name="mem_kv")
        xf, xb = _mem_layer(xb, xf, _layer_to_bf16(w_mq, l), kvm, _layer_to_bf16(w_mo, l),
                            lnp[2], lnp[3], l, alpha)
        hdn = _panel_matmul(xb, w_up, l, kind="relu2", tn=tn_p, n_tiles=w_up.shape[2] // tn_p,
                            src_block=lambda j: j, name="mlp_up")
        z = _matmul(hdn, w_down, l, kind="resid", extras=(tile(xf, tn_d),), alpha=alpha, out_dtype=F32,
                    name="mlp_down")
        xf, xb = _layer_norm(z, lnp[4], lnp[5], l)
    return xf[None]
```

```python
import functools

import jax
import jax.numpy as jnp
from jax import lax
from jax.experimental import pallas as pl
from jax.experimental.pallas import tpu as pltpu

F32 = jnp.float32
BF16 = jnp.bfloat16

HEAD_DIM = 128
DIL_GROUPS = ((128, 1), (512, 4), (2048, 16))
A_HEADS_PER_GROUP = 4
A_GROUPS = len(DIL_GROUPS)
A_WIDTH = A_GROUPS * A_HEADS_PER_GROUP * HEAD_DIM
A_OUT = A_HEADS_PER_GROUP * HEAD_DIM
B_HEADS = 8
B_WIDTH = B_HEADS * HEAD_DIM
IDX_HEADS = 16
IDX_DIM = 64
IDX_WIDTH = IDX_HEADS * IDX_DIM
DSA_TOPK = 256
MEM_HEADS = 4
MEM_WIDTH = MEM_HEADS * HEAD_DIM
ROPE_THETA = 10000.0
LN_EPS = 1e-5
NEG_INF = -1e30
M_FLOOR = -1e29
BAND = 128
LOG2E = 1.4426950408889634

LANES = 128
MXU_COLS = 256
VMEM_LIMIT = 56 * 1024 * 1024
B_TILE_BYTES = 8 * 1024 * 1024
COL_BLOCK = 512

INT_MIN = -(2 ** 31)
INT_MAX = 2 ** 31 - 1
NT_DIMS = (((1,), (1,)), ((), ()))


def _pick(n, prefs):
    for p in prefs:
        if n % p == 0:
            return p
    return n


def _params(sem):
    return pltpu.CompilerParams(dimension_semantics=sem, vmem_limit_bytes=VMEM_LIMIT)


def _rope128(x, cos, sin):
    return x * cos + pltpu.roll(x, 64, 1) * sin


def _rope64(x, cos, sin_lo, sin_hi):
    return x * cos + pltpu.roll(x, 96, 1) * sin_lo + pltpu.roll(x, 32, 1) * sin_hi


def _rope128_tile(acc, cos, sin):
    return [_rope128(acc[:, c * LANES:(c + 1) * LANES], cos, sin) for c in range(acc.shape[1] // LANES)]


def _epilogue(kind, acc, extras, o_ref, stage_ref, dil, j, alpha):
    tn = acc.shape[1]
    if kind == "none":
        o_ref[...] = acc.astype(o_ref.dtype)
    elif kind == "resid":
        o_ref[...] = (alpha * extras[0][...] + acc).astype(o_ref.dtype)
    elif kind == "rope128":
        for c, y in enumerate(_rope128_tile(acc, extras[0][...], extras[1][...])):
            o_ref[:, c * LANES:(c + 1) * LANES] = y.astype(o_ref.dtype)
    elif kind == "rope128_qk":
        qs = jnp.where(j < B_WIDTH // COL_BLOCK, HEAD_DIM ** -0.5 * LOG2E, 1.0)
        for c, y in enumerate(_rope128_tile(acc, extras[0][...], extras[1][...])):
            o_ref[:, c * LANES:(c + 1) * LANES] = (y * qs).astype(o_ref.dtype)
    elif kind == "rope64":
        cos, s_lo, s_hi = extras[0][...], extras[1][...], extras[2][...]
        for c in range(tn // LANES):
            sl = slice(c * LANES, (c + 1) * LANES)
            o_ref[:, sl] = _rope64(acc[:, sl], cos, s_lo, s_hi).astype(o_ref.dtype)
    elif kind == "kiwi":
        cos, s_lo, s_hi = extras[0][...], extras[1][...], extras[2][...]
        o_ref[:, :LANES] = _rope64(acc[:, :LANES], cos, s_lo, s_hi).astype(o_ref.dtype)
        o_ref[:, LANES:] = acc[:, LANES:].astype(o_ref.dtype)
    elif kind == "a_group":
        cos = jnp.where(j < 2, extras[0][...], 1.0)
        sin = jnp.where(j < 2, extras[1][...], 0.0)
        for c, y in enumerate(_rope128_tile(acc, cos, sin)):
            stage_ref[c] = y

        rows = acc.shape[0] // dil
        for r in range(dil):
            for c in range(tn // LANES):
                src = stage_ref[c] if dil == 1 else stage_ref[c, pl.ds(r, rows, stride=dil), :]
                o_ref[r, :, c * LANES:(c + 1) * LANES] = src.astype(o_ref.dtype)
    elif kind == "sigmoid":
        o_ref[...] = (1.0 / (1.0 + jnp.exp(-acc))).astype(o_ref.dtype)
    elif kind == "relu2":
        r = jnp.maximum(acc, 0.0)
        o_ref[...] = (r * r).astype(o_ref.dtype)
    else:
        raise ValueError(kind)


def _mm_kernel(*refs, nk, kind, n_extra, dil, alpha):
    a_ref, b_ref = refs[0], refs[1]
    extras = refs[2:2 + n_extra]
    o_ref = refs[2 + n_extra]
    scratch = list(refs[3 + n_extra:])
    acc_ref = scratch.pop(0) if nk > 1 else None
    stage_ref = scratch.pop(0) if kind == "a_group" else None
    j = pl.program_id(1)
    k = pl.program_id(2)
    if nk == 1:
        part = jnp.dot(a_ref[...], b_ref[...].astype(BF16), preferred_element_type=F32)
        _epilogue(kind, part, extras, o_ref, stage_ref, dil, j, alpha)
        return

    @pl.when(k == 0)
    def _():
        acc_ref[...] = jnp.zeros(acc_ref.shape, F32)

    acc_ref[...] += jnp.dot(a_ref[...], b_ref[...].astype(BF16), preferred_element_type=F32)

    @pl.when(k == nk - 1)
    def _():
        _epilogue(kind, acc_ref[...], extras, o_ref, stage_ref, dil, j, alpha)


def _matmul(a, b, layer, *, kind="none", extras=(), out_dtype=BF16, name="mm",
            tn=None, n_tiles=None, src_block=None, dil=1, alpha=None):
    m, k = a.shape
    nb = b.shape[2]
    tm = _pick(m, (1024, 512, 256))
    if tn is None:
        tn = nb if nb <= 1280 else _pick(nb, (1024, 512, 256, 128))
    if n_tiles is None:
        n_tiles = nb // tn
    if src_block is None:
        src_block = lambda j: j
    itemsize = jnp.dtype(b.dtype).itemsize
    tk = k
    for cand in (k, 4096, 2048, 1024, 512):
        if cand <= k and k % cand == 0:
            tk = cand
            if cand * tn * itemsize <= B_TILE_BYTES:
                break
    nk = k // tk
    assert m % tm == 0 and k % tk == 0 and tm % dil == 0
    in_specs = [
        pl.BlockSpec((tm, tk), lambda i, j, kk: (i, kk)),
        pl.BlockSpec((None, tk, tn), lambda i, j, kk: (layer, kk, src_block(j))),
    ]
    ops = [a, b]
    for arr, bshape, imap in extras:
        in_specs.append(pl.BlockSpec(bshape, functools.partial(lambda i, j, kk, f: f(i, j), f=imap)))
        ops.append(arr)
    scratch = [pltpu.VMEM((tm, tn), F32)] if nk > 1 else []
    if kind == "a_group":
        scratch.append(pltpu.VMEM((tn // LANES, tm, LANES), F32))
        out_shape = jax.ShapeDtypeStruct((n_tiles, dil, m // dil, tn), out_dtype)
        out_spec = pl.BlockSpec((None, dil, tm // dil, tn), lambda i, j, kk: (j, 0, i, 0))
    else:
        out_shape = jax.ShapeDtypeStruct((m, n_tiles * tn), out_dtype)
        out_spec = pl.BlockSpec((tm, tn), lambda i, j, kk: (i, j))
    return pl.pallas_call(
        functools.partial(_mm_kernel, nk=nk, kind=kind, n_extra=len(extras), dil=dil, alpha=alpha),
        grid=(m // tm, n_tiles, nk),
        in_specs=in_specs,
        out_specs=out_spec,
        out_shape=out_shape,
        scratch_shapes=scratch,
        compiler_params=_params(("parallel", "parallel", "arbitrary")),
        name=name,
    )(*ops)


def _panel_kernel(*refs, kind, n_extra, dil, shift, trans_b, tn, alpha):
    a_ref, b_ref = refs[0], refs[1]
    n_b = 2 if shift else 1
    extras = refs[1 + n_b:1 + n_b + n_extra]
    o_ref = refs[1 + n_b + n_extra]
    scratch = list(refs[2 + n_b + n_extra:])
    bb_ref = scratch.pop(0)
    stage_ref = scratch.pop(0) if kind == "a_group" else None
    j = pl.program_id(0)
    i = pl.program_id(1)

    @pl.when(i == 0)
    def _():
        w = b_ref[...].astype(BF16)
        if kind == "kiwi":
            q = IDX_DIM
            bb_ref[0:q, :] = w[0:q]
            bb_ref[q:2 * q, :] = w[0:q]
            bb_ref[2 * q:3 * q, :] = w[q:2 * q]
            bb_ref[3 * q:4 * q, :] = w[q:2 * q]
        elif trans_b:
            bb_ref[:b_ref.shape[0], :] = w
            if shift:
                bb_ref[b_ref.shape[0]:, :] = refs[2][...].astype(BF16)
        else:
            bb_ref[...] = w

    if trans_b:
        acc = lax.dot_general(a_ref[...], bb_ref[shift:shift + tn, :], NT_DIMS, preferred_element_type=F32)
    else:
        acc = jnp.dot(a_ref[...], bb_ref[...], preferred_element_type=F32)
    _epilogue(kind, acc, extras, o_ref, stage_ref, dil, j, alpha)


def _panel_matmul(a, b, layer, *, tn, n_tiles, src_block, kind="none", extras=(), out_dtype=BF16,
                  name="mm", dil=1, shift=0, trans_b=False, alpha=None):
    m, k = a.shape
    tm = _pick(m, (1024, 512, 256))
    assert m % tm == 0 and tm % dil == 0 and tn % LANES == 0
    assert shift == 0 or (trans_b and shift % 16 == 0 and shift < LANES)
    out_tn = 2 * tn if kind == "kiwi" else tn
    b_rows = tn + (LANES if shift else 0)
    in_specs = [pl.BlockSpec((tm, k), lambda j, i: (i, 0))]
    ops = [a, b]
    if trans_b:
        in_specs.append(pl.BlockSpec((None, tn, k), lambda j, i: (layer, src_block(j), 0)))
        if shift:
            per = tn // LANES
            in_specs.append(pl.BlockSpec((None, LANES, k), lambda j, i: (layer, (src_block(j) + 1) * per, 0)))
            ops.append(b)
        scratch = [pltpu.VMEM((out_tn if kind == "kiwi" else b_rows, k), BF16)]
    else:
        in_specs.append(pl.BlockSpec((None, k, tn), lambda j, i: (layer, 0, src_block(j))))
        scratch = [pltpu.VMEM((k, tn), BF16)]
    for arr, bshape, imap in extras:
        in_specs.append(pl.BlockSpec(bshape, functools.partial(lambda j, i, f: f(i, j), f=imap)))
        ops.append(arr)
    if kind == "a_group":
        scratch.append(pltpu.VMEM((tn // LANES, tm, LANES), F32))
        out_shape = jax.ShapeDtypeStruct((n_tiles, dil, m // dil, tn), out_dtype)
        out_spec = pl.BlockSpec((None, dil, tm // dil, tn), lambda j, i: (j, 0, i, 0))
    else:
        out_shape = jax.ShapeDtypeStruct((m, n_tiles * out_tn), out_dtype)
        out_spec = pl.BlockSpec((tm, out_tn), lambda j, i: (i, j))
    return pl.pallas_call(
        functools.partial(_panel_kernel, kind=kind, n_extra=len(extras), dil=dil, shift=shift,
                          trans_b=trans_b, tn=out_tn, alpha=alpha),
        grid=(n_tiles, m // tm),
        in_specs=in_specs,
        out_specs=out_spec,
        out_shape=out_shape,
        scratch_shapes=scratch,
        compiler_params=_params(("arbitrary", "arbitrary")),
        name=name,
    )(*ops)


def _ln_kernel(z_ref, g_ref, b_ref, o_ref, ob_ref):
    z = z_ref[...]
    mu = jnp.mean(z, axis=-1, keepdims=True)
    zc = z - mu
    var = jnp.mean(zc * zc, axis=-1, keepdims=True)
    out = (zc * lax.rsqrt(var + LN_EPS)) * g_ref[...] + b_ref[...]
    o_ref[...] = out
    ob_ref[...] = out.astype(BF16)


def _layer_norm(z, g, b, layer):
    t, d = z.shape
    tm = _pick(t, (512, 256, 128))
    row = pl.BlockSpec((tm, d), lambda i: (i, 0))
    par = pl.BlockSpec((None, 1, d), lambda i: (layer, 0, 0))
    return pl.pallas_call(
        _ln_kernel,
        grid=(t // tm,),
        in_specs=[row, par, par],
        out_specs=[row, row],
        out_shape=[jax.ShapeDtypeStruct((t, d), F32), jax.ShapeDtypeStruct((t, d), BF16)],
        compiler_params=_params(("parallel",)),
        name="layer_norm",
    )(z, g, b)


def _merge_kernel(ya_ref, wa_ref, yb_ref, wb_ref, ga_ref, gb_ref, o_ref):
    pa = jnp.dot(ya_ref[...], wa_ref[...].astype(BF16), preferred_element_type=F32)
    pb = jnp.dot(yb_ref[...], wb_ref[...].astype(BF16), preferred_element_type=F32)
    o_ref[...] = (ga_ref[...].astype(F32) * pa + gb_ref[...].astype(F32) * pb).astype(o_ref.dtype)


def _gated_merge(ya, yb, w_pa, w_pb, gates, layer):
    t = ya.shape[0]
    d = w_pa.shape[2]
    tm = _pick(t, (1024, 512, 256))
    tn = d if d <= 1280 else _pick(d, (1024, 512, 256, 128))
    ka, kb = ya.shape[1], yb.shape[1]
    return pl.pallas_call(
        _merge_kernel,
        grid=(t // tm, d // tn),
        in_specs=[
            pl.BlockSpec((tm, ka), lambda i, j: (i, 0)),
            pl.BlockSpec((None, ka, tn), lambda i, j: (layer, 0, j)),
            pl.BlockSpec((tm, kb), lambda i, j: (i, 0)),
            pl.BlockSpec((None, kb, tn), lambda i, j: (layer, 0, j)),
            pl.BlockSpec((tm, tn), lambda i, j: (i, j)),
            pl.BlockSpec((tm, tn), lambda i, j: (i, j + d // tn)),
        ],
        out_specs=pl.BlockSpec((tm, tn), lambda i, j: (i, j)),
        out_shape=jax.ShapeDtypeStruct((t, d), BF16),
        compiler_params=_params(("parallel", "parallel")),
        name="gated_merge",
    )(ya, w_pa, yb, w_pb, gates, gates)


def _dil_kernel(q_ref, kp_ref, kc_ref, vp_ref, vc_ref, o_ref, lse_ref, *, nsub):
    i = pl.program_id(1)
    qrows = min(nsub, 2) * BAND
    qq = lax.broadcasted_iota(jnp.int32, (qrows, BAND + qrows), 0)
    kk = lax.broadcasted_iota(jnp.int32, (qrows, BAND + qrows), 1)
    in_band = (kk >= qq) & (kk <= qq + BAND)
    bias = jnp.where(in_band, 0.0, NEG_INF)
    bias_first = jnp.where(in_band & (kk >= jnp.where(i > 0, 0, BAND)), 0.0, NEG_INF)
    scale = HEAD_DIM ** -0.5
    chains = [(g, h) for g in range(nsub * BAND // qrows) for h in range(A_HEADS_PER_GROUP)]
    scores, v_bands = [], []
    for g, h in chains:
        rows = slice(g * qrows, (g + 1) * qrows)
        band = slice(g * qrows - BAND, (g + 1) * qrows)
        sl = slice(h * HEAD_DIM, (h + 1) * HEAD_DIM)
        if g == 0:
            k_band = jnp.concatenate([kp_ref[:, sl], kc_ref[rows, sl]], axis=0)
            v_bands.append(jnp.concatenate([vp_ref[:, sl], vc_ref[rows, sl]], axis=0))
        else:
            k_band = kc_ref[band, sl]
            v_bands.append(vc_ref[band, sl])
        s = lax.dot_general(q_ref[rows, sl], k_band, NT_DIMS, preferred_element_type=F32) * scale
        scores.append(s + (bias_first if g == 0 else bias))
    maxes = [jnp.max(s, axis=-1, keepdims=True) for s in scores]
    exps = [jnp.exp(s - m) for s, m in zip(scores, maxes)]
    unnorm = [jnp.dot(e.astype(BF16), v, preferred_element_type=F32) for e, v in zip(exps, v_bands)]
    sums = [jnp.sum(e, axis=-1, keepdims=True) for e in exps]
    outs = [o * (1.0 / l) for o, l in zip(unnorm, sums)]
    lses = [jnp.broadcast_to(m + jnp.log(l), (qrows, HEAD_DIM)) for m, l in zip(maxes, sums)]
    nh = A_HEADS_PER_GROUP
    ngrp = len(outs) // nh
    o_ref[...] = jnp.concatenate(
        [jnp.concatenate(outs[g * nh:(g + 1) * nh], axis=1) for g in range(ngrp)], axis=0)
    lse_ref[...] = jnp.concatenate(
        [jnp.concatenate(lses[g * nh:(g + 1) * nh], axis=1) for g in range(ngrp)], axis=0)


def _dilated_group(qkv, dil):
    sub = qkv.shape[2]
    assert sub % BAND == 0
    nsub = _pick(sub // BAND, (4, 2, 1))
    rows = nsub * BAND

    def spec(which):
        return pl.BlockSpec((None, None, rows, A_OUT), lambda r, i: (which, r, i, 0))

    def prev_spec(which):
        return pl.BlockSpec((None, None, BAND, A_OUT),
                            lambda r, i: (which, r, jnp.maximum(i * nsub - 1, 0), 0))

    out_spec = pl.BlockSpec((None, rows, A_OUT), lambda r, i: (r, i, 0))
    return pl.pallas_call(
        functools.partial(_dil_kernel, nsub=nsub),
        grid=(dil, sub // rows),
        in_specs=[spec(0), prev_spec(1), spec(1), prev_spec(2), spec(2)],
        out_specs=[out_spec, out_spec],
        out_shape=[jax.ShapeDtypeStruct((dil, sub, A_OUT), F32)] * 2,
        compiler_params=_params(("parallel", "parallel")),
        name=f"dilated_attn_d{dil}",
    )(qkv, qkv, qkv, qkv, qkv)


def _mix_kernel(*refs, dils):
    n = len(dils)
    o_refs, l_refs, y_ref = refs[:n], refs[n:2 * n], refs[2 * n]
    stage = refs[2 * n + 1:]
    tm = y_ref.shape[0]

    def natural(ref, slot, dil, cs):
        if dil == 1:
            return ref[0, :, cs]
        for r in range(dil):
            slot[pl.ds(r, tm // dil, stride=dil), :] = ref[r, :, cs]
        return slot[...]

    for c in range(A_OUT // LANES):
        cs = slice(c * LANES, (c + 1) * LANES)
        outs = [natural(o_refs[g], stage[2 * g], dils[g], cs) for g in range(n)]
        lses = [natural(l_refs[g], stage[2 * g + 1], dils[g], cs) for g in range(n)]
        m = functools.reduce(jnp.maximum, lses)
        es = [jnp.exp(l - m) for l in lses]
        inv = 1.0 / functools.reduce(lambda a, b: a + b, es)
        y = functools.reduce(lambda a, b: a + b, [(e * inv) * o for e, o in zip(es, outs)])
        y_ref[:, cs] = y.astype(y_ref.dtype)


def _mix_groups(outs, lses, dils, t):
    tm = _pick(t, (512, 256))
    specs = [pl.BlockSpec((d, tm // d, A_OUT), lambda i: (0, i, 0)) for d in dils]
    return pl.pallas_call(
        functools.partial(_mix_kernel, dils=tuple(dils)),
        grid=(t // tm,),
        in_specs=specs + specs,
        out_specs=pl.BlockSpec((tm, A_OUT), lambda i: (i, 0)),
        out_shape=jax.ShapeDtypeStruct((t, A_OUT), BF16),
        scratch_shapes=[pltpu.VMEM((tm, LANES), F32)] * (2 * len(dils)),
        compiler_params=_params(("parallel",)),
        name="mix_groups",
    )(*outs, *lses)


def _sortable(x):
    b = pltpu.bitcast(x, jnp.int32)
    return b ^ ((b >> 31) & jnp.int32(INT_MAX))


def _dsa_kernel(step_q_ref, step_c_ref, qi_ref, ki_ref, wi_ref, q_ref, k_ref, v_ref, o_ref,
                key_ref, bias_ref, qs_ref, wb_ref, thr_ref, mthr_ref, m_ref, acc_ref,
                *, tq, tkc, topk):
    i = step_q_ref[pl.program_id(0)]
    c = step_c_ref[pl.program_id(0)]
    last_c = ((i + 1) * tq - 1) // tkc
    n_ch = last_c + 1
    row_pos = i * tq + lax.broadcasted_iota(jnp.int32, (tq, tkc), 0)
    col_iota = lax.broadcasted_iota(jnp.int32, (tq, tkc), 1)
    sub = MXU_COLS
    row_sub = i * tq + lax.broadcasted_iota(jnp.int32, (tq, sub), 0)
    col_sub = lax.broadcasted_iota(jnp.int32, (tq, sub), 1)

    @pl.when(c == 0)
    def _select():
        lo_half = lax.broadcasted_iota(jnp.int32, (tq, LANES), 1) < IDX_DIM
        for hp in range(IDX_HEADS // 2):
            pair = qi_ref[:, hp * LANES:(hp + 1) * LANES]
            qs_ref[(2 * hp) * tq:(2 * hp + 1) * tq, :] = jnp.where(lo_half, pair, 0.0).astype(BF16)
            qs_ref[(2 * hp + 1) * tq:(2 * hp + 2) * tq, :] = jnp.where(lo_half, 0.0, pair).astype(BF16)
        w = wi_ref[...] * (IDX_DIM ** -0.5 * IDX_HEADS ** -0.5)
        for h in range(IDX_HEADS):
            wb_ref[h] = w[:, h:h + 1] + jnp.zeros((tq, LANES), F32)

        def score_chunk(kc, carry):
            off = pl.multiple_of(kc * tkc, tkc)
            for s in range(tkc // sub):
                kis = ki_ref[pl.ds(off + s * sub, sub), :].astype(BF16)
                r = lax.dot_general(qs_ref[...], kis, NT_DIMS, preferred_element_type=F32)
                halves = []
                for half in range(sub // LANES):
                    cs = slice(half * LANES, (half + 1) * LANES)
                    sc = jnp.zeros((tq, LANES), F32)
                    for h in range(IDX_HEADS):
                        sc = sc + wb_ref[h] * jnp.maximum(r[h * tq:(h + 1) * tq, cs], 0.0)
                    halves.append(sc)
                sc = jnp.concatenate(halves, axis=1)
                cols = slice(s * sub, (s + 1) * sub)
                sc = jnp.where(off + s * sub + col_sub <= row_sub, sc, NEG_INF)
                key_ref[kc, :, cols] = _sortable(sc)
            return carry

        lax.fori_loop(0, n_ch, score_chunk, 0)

        def count(pred):
            def body(kc, acc):
                off = pl.multiple_of(kc * tkc, tkc)
                part = jnp.where(pred(key_ref[kc], off), 1, 0).astype(jnp.int32)
                for s in range(tkc // LANES):
                    acc = acc + part[:, s * LANES:(s + 1) * LANES]
                return acc
            acc = lax.fori_loop(0, n_ch, body, jnp.zeros((tq, LANES), jnp.int32))
            return jnp.sum(acc, axis=1, keepdims=True)

        def unsettled(state):
            it, lo, hi, n_lo = state
            open_rows = jnp.where((n_lo == topk) | (lo == hi), 0, 1)
            return jnp.logical_and(it < 34, jnp.max(open_rows) > 0)

        def bisect(state):
            it, lo, hi, n_lo = state
            mid = (lo | hi) - ((lo ^ hi) >> 1)
            n_mid = count(lambda kk, off: kk >= mid)
            ok = n_mid >= topk
            return (it + 1, jnp.where(ok, mid, lo), jnp.where(ok, hi, mid - 1), jnp.where(ok, n_mid, n_lo))

        lo0 = jnp.full((tq, 1), INT_MIN, jnp.int32)
        hi0 = jnp.full((tq, 1), INT_MAX, jnp.int32)
        n0 = jnp.full((tq, 1), INT_MAX, jnp.int32)
        _, thr, _, n_ge = lax.while_loop(unsettled, lambda st: bisect(bisect(st)),
                                         (jnp.int32(0), lo0, hi0, n0))
        thr_ref[...] = thr
        mthr_ref[...] = jnp.full((tq, 1), INT_MAX, jnp.int32)

        @pl.when(jnp.max(n_ge) > topk)
        def _ties():
            need = topk - count(lambda kk, off: kk > thr)

            def bisect_idx(_, lohi):
                lo, hi = lohi
                mid = (lo + hi) >> 1
                ok = count(lambda kk, off: (kk == thr) & (off + col_iota <= mid)) >= need
                return jnp.where(ok, lo, mid + 1), jnp.where(ok, mid, hi)

            lo1 = jnp.zeros((tq, 1), jnp.int32)
            hi1 = jnp.full((tq, 1), 2 ** 30, jnp.int32)
            _, m_idx = lax.fori_loop(0, 31, bisect_idx, (lo1, hi1))
            mthr_ref[...] = m_idx

        m_ref[...] = jnp.full(m_ref.shape, M_FLOOR, F32)
        acc_ref[...] = jnp.zeros(acc_ref.shape, F32)

    @pl.when(c <= last_c)
    def _attend():
        off = pl.multiple_of(c * tkc, tkc)
        kk = key_ref[c]
        thr = thr_ref[...]
        pos = off + col_iota
        chosen = (kk > thr) | ((kk == thr) & (pos <= mthr_ref[...]))
        bias_ref[...] = jnp.where(chosen & (pos <= row_pos), 0.0, NEG_INF)
        ones = jnp.ones((tkc, HEAD_DIM), BF16)
        for h in range(B_HEADS):
            sl = slice(h * HEAD_DIM, (h + 1) * HEAD_DIM)
            s = lax.dot_general(q_ref[:, sl], k_ref[:, sl], NT_DIMS, preferred_element_type=F32) + bias_ref[...]
            m_old = m_ref[h]
            m_new = jnp.maximum(m_old, jnp.max(s, axis=-1, keepdims=True))
            p = jnp.exp2(s - m_new).astype(BF16)
            alpha = jnp.exp2(m_old - m_new)
            v_one = jnp.concatenate([v_ref[:, sl], ones], axis=1)
            acc_ref[h] = alpha * acc_ref[h] + jnp.dot(p, v_one, preferred_element_type=F32)
            m_ref[h] = m_new

    @pl.when(c == last_c)
    def _finish():
        for h in range(B_HEADS):
            a = acc_ref[h]
            o_ref[:, h * HEAD_DIM:(h + 1) * HEAD_DIM] = (a[:, :HEAD_DIM] / a[:, HEAD_DIM:]).astype(o_ref.dtype)


def _dsa(qi, kiwi, qk, v):
    t = qi.shape[0]
    tq = _pick(t, (256, 128))
    tkc = _pick(t, (1024, 512, 256))
    topk = min(DSA_TOPK, t // 4)
    assert tkc >= topk and t % tq == 0 and tkc % MXU_COLS == 0
    nq, nc = t // tq, t // tkc
    steps = [(i, c) for i in range(nq) for c in range(((i + 1) * tq - 1) // tkc + 1)]
    step_q = jnp.asarray([s[0] for s in steps], jnp.int32)
    step_c = jnp.asarray([s[1] for s in steps], jnp.int32)

    in_specs = [
        pl.BlockSpec((tq, IDX_WIDTH), lambda s, sq, sc: (sq[s], 0)),
        pl.BlockSpec((t, LANES), lambda s, sq, sc: (0, 0)),
        pl.BlockSpec((tq, LANES), lambda s, sq, sc: (sq[s], 1)),
        pl.BlockSpec((tq, B_WIDTH), lambda s, sq, sc: (sq[s], 0)),
        pl.BlockSpec((tkc, B_WIDTH), lambda s, sq, sc: (sc[s], 1)),
        pl.BlockSpec((tkc, B_WIDTH), lambda s, sq, sc: (sc[s], 0)),
    ]
    grid_spec = pltpu.PrefetchScalarGridSpec(
        num_scalar_prefetch=2,
        grid=(len(steps),),
        in_specs=in_specs,
        out_specs=pl.BlockSpec((tq, B_WIDTH), lambda s, sq, sc: (sq[s], 0)),
        scratch_shapes=[
            pltpu.VMEM((nc, tq, tkc), jnp.int32),
            pltpu.VMEM((tq, tkc), F32),
            pltpu.VMEM((IDX_HEADS * tq, LANES), BF16),
            pltpu.VMEM((IDX_HEADS, tq, LANES), F32),
            pltpu.VMEM((tq, 1), jnp.int32),
            pltpu.VMEM((tq, 1), jnp.int32),
            pltpu.VMEM((B_HEADS, tq, 1), F32),
            pltpu.VMEM((B_HEADS, tq, 2 * HEAD_DIM), F32),
        ],
    )
    return pl.pallas_call(
        functools.partial(_dsa_kernel, tq=tq, tkc=tkc, topk=topk),
        grid_spec=grid_spec,
        out_shape=jax.ShapeDtypeStruct((t, B_WIDTH), BF16),
        compiler_params=_params(("arbitrary",)),
        name="dsa",
    )(step_q, step_c, qi, kiwi, kiwi, qk, qk, v)


def _mem_layer_kernel(xb_ref, x_ref, wq_ref, kv_ref, wo_ref, g_ref, b_ref, o_ref, ob_ref, *, alpha):
    scale = HEAD_DIM ** -0.5
    q = jnp.dot(xb_ref[...], wq_ref[...], preferred_element_type=F32).astype(BF16)
    ksl = [slice(h * HEAD_DIM, (h + 1) * HEAD_DIM) for h in range(MEM_HEADS)]
    vsl = [slice(MEM_WIDTH + h * HEAD_DIM, MEM_WIDTH + (h + 1) * HEAD_DIM) for h in range(MEM_HEADS)]
    scores = [lax.dot_general(q[:, sl], kv_ref[:, sl], NT_DIMS, preferred_element_type=F32) * scale
              for sl in ksl]
    maxes = [jnp.max(s, axis=-1, keepdims=True) for s in scores]
    exps = [jnp.exp(s - m) for s, m in zip(scores, maxes)]
    unnorm = [jnp.dot(e.astype(BF16), kv_ref[:, sl], preferred_element_type=F32) for e, sl in zip(exps, vsl)]
    sums = [jnp.sum(e, axis=-1, keepdims=True) for e in exps]
    om = jnp.concatenate([o * (1.0 / l) for o, l in zip(unnorm, sums)], axis=1).astype(BF16)
    z = alpha * x_ref[...] + jnp.dot(om, wo_ref[...], preferred_element_type=F32)
    mu = jnp.mean(z, axis=-1, keepdims=True)
    zc = z - mu
    var = jnp.mean(zc * zc, axis=-1, keepdims=True)
    out = (zc * lax.rsqrt(var + LN_EPS)) * g_ref[...] + b_ref[...]
    o_ref[...] = out
    ob_ref[...] = out.astype(BF16)


def _mem_layer(xb, x, wq, kv, wo, g, b, layer, alpha):
    t, d = x.shape
    tm = _pick(t, (256, 128))
    row = lambda w: pl.BlockSpec((tm, w), lambda i: (i, 0))
    whole = lambda a: pl.BlockSpec(a.shape, lambda i: (0, 0))
    par = pl.BlockSpec((None, 1, d), lambda i: (layer, 0, 0))
    return pl.pallas_call(
        functools.partial(_mem_layer_kernel, alpha=alpha),
        grid=(t // tm,),
        in_specs=[row(d), row(d), whole(wq), whole(kv), whole(wo), par, par],
        out_specs=[row(d), row(d)],
        out_shape=[jax.ShapeDtypeStruct((t, d), F32), jax.ShapeDtypeStruct((t, d), BF16)],
        compiler_params=_params(("parallel",)),
        name="mem_layer",
    )(xb, x, wq, kv, wo, g, b)


def _cast_layer_kernel(w_ref, o_ref):
    o_ref[...] = w_ref[...].astype(o_ref.dtype)


def _layer_to_bf16(w, layer):
    _, r, c = w.shape
    tr = _pick(r, (512, 256, 128))
    return pl.pallas_call(
        _cast_layer_kernel,
        grid=(r // tr,),
        in_specs=[pl.BlockSpec((None, tr, c), lambda i: (layer, i, 0))],
        out_specs=pl.BlockSpec((tr, c), lambda i: (i, 0)),
        out_shape=jax.ShapeDtypeStruct((r, c), BF16),
        compiler_params=_params(("parallel",)),
        name="cast_layer",
    )(w)


def _cast_kernel(x_ref, o_ref):
    o_ref[...] = x_ref[...].astype(o_ref.dtype)


def _to_bf16(x):
    t, d = x.shape
    tm = _pick(t, (256, 128))
    spec = pl.BlockSpec((tm, d), lambda i: (i, 0))
    return pl.pallas_call(
        _cast_kernel, grid=(t // tm,), in_specs=[spec], out_specs=spec,
        out_shape=jax.ShapeDtypeStruct((t, d), BF16),
        compiler_params=_params(("parallel",)), name="cast_bf16",
    )(x)


def _rope_tables(t):
    pos = jnp.arange(t, dtype=jnp.int32).astype(F32)[:, None]
    half = HEAD_DIM // 2
    inv = ROPE_THETA ** (-jnp.arange(half, dtype=F32) / half)
    ang = pos * inv[None, :]
    cos128 = jnp.concatenate([jnp.cos(ang)] * 2, axis=1)
    sin128 = jnp.concatenate([-jnp.sin(ang), jnp.sin(ang)], axis=1)
    half = IDX_DIM // 2
    inv = ROPE_THETA ** (-jnp.arange(half, dtype=F32) / half)
    ang = pos * inv[None, :]
    z = jnp.zeros_like(ang)
    cos64 = jnp.concatenate([jnp.cos(ang)] * 4, axis=1)
    sin_lo = jnp.concatenate([-jnp.sin(ang), z] * 2, axis=1)
    sin_hi = jnp.concatenate([z, jnp.sin(ang)] * 2, axis=1)
    return (cos128, sin128), (cos64, sin_lo, sin_hi)


def _mixer_branches(xb, w_in_t, l, blk, rope128, rope64):
    t = xb.shape[0]
    proj = functools.partial(_panel_matmul, xb, w_in_t, l, trans_b=True)
    outs, lses, dils = [], [], []
    for g, (win, dil) in enumerate(DIL_GROUPS):
        assert win // dil == BAND
        qkv = proj(kind="a_group", extras=rope128, tn=COL_BLOCK, n_tiles=3, dil=dil,
                   src_block=lambda j, g=g: blk[0] + g + j * (A_WIDTH // COL_BLOCK), name=f"proj_a_g{g}")
        o, lse = _dilated_group(qkv, dil)
        outs.append(o)
        lses.append(lse)
        dils.append(dil)
    ya = _mix_groups(outs, lses, dils, t)
    qk_b = proj(kind="rope128_qk", extras=rope128, tn=COL_BLOCK, n_tiles=2 * B_WIDTH // COL_BLOCK,
                src_block=lambda j: blk[3] + j, name="proj_b_qk")
    v_b = proj(tn=COL_BLOCK, n_tiles=B_WIDTH // COL_BLOCK, src_block=lambda j: blk[5] + j, name="proj_b_v")
    qi = proj(kind="rope64", extras=rope64, out_dtype=F32, tn=COL_BLOCK, n_tiles=IDX_WIDTH // COL_BLOCK,
              src_block=lambda j: blk[6] + j, name="proj_qi")
    kiwi = proj(kind="kiwi", extras=rope64, out_dtype=F32, tn=LANES, n_tiles=1,
                src_block=lambda j: blk[7] * (COL_BLOCK // LANES), name="proj_kiwi")
    yb = _dsa(qi, kiwi, qk_b, v_b)
    return ya, yb


def kernel(x, mem, w_in, w_pa, w_pb, w_o, ln1_g, ln1_b, w_mq, w_mkv, w_mo, ln2_g, ln2_b, w_up, w_down, ln3_g, ln3_b):
    bsz, t, d = x.shape
    depth = w_in.shape[0]
    assert bsz == 1
    alpha = (2 * depth) ** 0.25

    w_in_t = jnp.swapaxes(w_in, 1, 2)
    sizes = (A_WIDTH, A_WIDTH, A_WIDTH, B_WIDTH, B_WIDTH, B_WIDTH, IDX_WIDTH, IDX_DIM, IDX_HEADS, d, d)
    offs = [0]
    for s in sizes:
        offs.append(offs[-1] + s)
    assert offs[-1] == w_in.shape[2] and all(o % COL_BLOCK == 0 for o in offs[:8])
    blk = [o // COL_BLOCK for o in offs[:8]]
    gate_shift = offs[9] - offs[7]
    assert 0 < gate_shift < LANES and (2 * d) % COL_BLOCK == 0
    lnp = [p.reshape(depth, 1, d) for p in (ln1_g, ln1_b, ln2_g, ln2_b, ln3_g, ln3_b)]

    (cos128, sin128), (cos64, sin_lo, sin_hi) = _rope_tables(t)
    tm_rows = _pick(t, (1024, 512, 256))

    xf = x[0]
    xb = _to_bf16(xf)
    memb = _to_bf16(mem[0])

    def row_tab(arr):
        return (arr, (tm_rows, LANES), lambda i, j: (i, 0))

    def tile(arr, tn, joff=0):
        return (arr, (tm_rows, tn), lambda i, j: (i, j + joff))

    rope128 = (row_tab(cos128), row_tab(sin128))
    rope64 = (row_tab(cos64), row_tab(sin_lo), row_tab(sin_hi))
    tn_d = d if d <= 1280 else _pick(d, (1024, 512, 256, 128))
    tn_p = _pick(d, (COL_BLOCK, 256, 128))

    for l in range(depth):
        ya, yb = _mixer_branches(xb, w_in_t, l, blk, rope128, rope64)
        gates = _panel_matmul(xb, w_in_t, l, kind="sigmoid", tn=COL_BLOCK, trans_b=True,
                              n_tiles=2 * d // COL_BLOCK, src_block=lambda j: blk[7] + j, shift=gate_shift,
                              name="proj_gate")
        merged = _gated_merge(ya, yb, w_pa, w_pb, gates, l)
        z = _panel_matmul(merged, w_o, l, kind="resid", extras=(tile(xf, tn_p),), alpha=alpha, out_dtype=F32,
                          tn=tn_p, n_tiles=d // tn_p, src_block=lambda j: j, name="mixer_out")
        xf, xb = _layer_norm(z, lnp[0], lnp[1], l)
        kvm = _matmul(memb, w_mkv, l, name="mem_kv")
        xf, xb = _mem_layer(xb, xf, _layer_to_bf16(w_mq, l), kvm, _layer_to_bf16(w_mo, l),
                            lnp[2], lnp[3], l, alpha)
        hdn = _panel_matmul(xb, w_up, l, kind="relu2", tn=tn_p, n_tiles=w_up.shape[2] // tn_p,
                            src_block=lambda j: j, name="mlp_up")
        z = _matmul(hdn, w_down, l, kind="resid", extras=(tile(xf, tn_d),), alpha=alpha, out_dtype=F32,
                    name="mlp_down")
        xf, xb = _layer_norm(z, lnp[4], lnp[5], l)
    return xf[None]
```

```python
import functools

import jax
import jax.numpy as jnp
from jax import lax
from jax.experimental import pallas as pl
from jax.experimental.pallas import tpu as pltpu

F32 = jnp.float32
BF16 = jnp.bfloat16

HEAD_DIM = 128
DIL_GROUPS = ((128, 1), (512, 4), (2048, 16))
A_HEADS_PER_GROUP = 4
A_GROUPS = len(DIL_GROUPS)
A_WIDTH = A_GROUPS * A_HEADS_PER_GROUP * HEAD_DIM
A_OUT = A_HEADS_PER_GROUP * HEAD_DIM
B_HEADS = 8
B_WIDTH = B_HEADS * HEAD_DIM
IDX_HEADS = 16
IDX_DIM = 64
IDX_WIDTH = IDX_HEADS * IDX_DIM
DSA_TOPK = 256
MEM_HEADS = 4
MEM_WIDTH = MEM_HEADS * HEAD_DIM
ROPE_THETA = 10000.0
LN_EPS = 1e-5
NEG_INF = -1e30
M_FLOOR = -1e29
BAND = 128
LOG2E = 1.4426950408889634

LANES = 128
MXU_COLS = 256
VMEM_LIMIT = 56 * 1024 * 1024
B_TILE_BYTES = 8 * 1024 * 1024
COL_BLOCK = 512

INT_MIN = -(2 ** 31)
INT_MAX = 2 ** 31 - 1
NT_DIMS = (((1,), (1,)), ((), ()))


def _pick(n, prefs):
    for p in prefs:
        if n % p == 0:
            return p
    return n


def _params(sem):
    return pltpu.CompilerParams(dimension_semantics=sem, vmem_limit_bytes=VMEM_LIMIT)


def _rope128(x, cos, sin):
    return x * cos + pltpu.roll(x, 64, 1) * sin


def _rope64(x, cos, sin_lo, sin_hi):
    return x * cos + pltpu.roll(x, 96, 1) * sin_lo + pltpu.roll(x, 32, 1) * sin_hi


def _rope128_tile(acc, cos, sin):
    return [_rope128(acc[:, c * LANES:(c + 1) * LANES], cos, sin) for c in range(acc.shape[1] // LANES)]


def _epilogue(kind, acc, extras, o_ref, stage_ref, dil, j, alpha):
    tn = acc.shape[1]
    if kind == "none":
        o_ref[...] = acc.astype(o_ref.dtype)
    elif kind == "resid":
        o_ref[...] = (alpha * extras[0][...] + acc).astype(o_ref.dtype)
    elif kind == "rope128":
        for c, y in enumerate(_rope128_tile(acc, extras[0][...], extras[1][...])):
            o_ref[:, c * LANES:(c + 1) * LANES] = y.astype(o_ref.dtype)
    elif kind == "rope128_qk":
        qs = jnp.where(j < B_WIDTH // COL_BLOCK, HEAD_DIM ** -0.5 * LOG2E, 1.0)
        for c, y in enumerate(_rope128_tile(acc, extras[0][...], extras[1][...])):
            o_ref[:, c * LANES:(c + 1) * LANES] = (y * qs).astype(o_ref.dtype)
    elif kind == "rope64":
        cos, s_lo, s_hi = extras[0][...], extras[1][...], extras[2][...]
        for c in range(tn // LANES):
            sl = slice(c * LANES, (c + 1) * LANES)
            o_ref[:, sl] = _rope64(acc[:, sl], cos, s_lo, s_hi).astype(o_ref.dtype)
    elif kind == "kiwi":
        cos, s_lo, s_hi = extras[0][...], extras[1][...], extras[2][...]
        o_ref[:, :LANES] = _rope64(acc[:, :LANES], cos, s_lo, s_hi).astype(o_ref.dtype)
        o_ref[:, LANES:] = acc[:, LANES:].astype(o_ref.dtype)
    elif kind == "a_group":
        cos = jnp.where(j < 2, extras[0][...], 1.0)
        sin = jnp.where(j < 2, extras[1][...], 0.0)
        for c, y in enumerate(_rope128_tile(acc, cos, sin)):
            stage_ref[c] = y

        rows = acc.shape[0] // dil
        for r in range(dil):
            for c in range(tn // LANES):
                src = stage_ref[c] if dil == 1 else stage_ref[c, pl.ds(r, rows, stride=dil), :]
                o_ref[r, :, c * LANES:(c + 1) * LANES] = src.astype(o_ref.dtype)
    elif kind == "sigmoid":
        o_ref[...] = (1.0 / (1.0 + jnp.exp(-acc))).astype(o_ref.dtype)
    elif kind == "relu2":
        r = jnp.maximum(acc, 0.0)
        o_ref[...] = (r * r).astype(o_ref.dtype)
    else:
        raise ValueError(kind)


def _mm_kernel(*refs, nk, kind, n_extra, dil, alpha):
    a_ref, b_ref = refs[0], refs[1]
    extras = refs[2:2 + n_extra]
    o_ref = refs[2 + n_extra]
    scratch = list(refs[3 + n_extra:])
    acc_ref = scratch.pop(0) if nk > 1 else None
    stage_ref = scratch.pop(0) if kind == "a_group" else None
    j = pl.program_id(1)
    k = pl.program_id(2)
    if nk == 1:
        part = jnp.dot(a_ref[...], b_ref[...].astype(BF16), preferred_element_type=F32)
        _epilogue(kind, part, extras, o_ref, stage_ref, dil, j, alpha)
        return

    @pl.when(k == 0)
    def _():
        acc_ref[...] = jnp.zeros(acc_ref.shape, F32)

    acc_ref[...] += jnp.dot(a_ref[...], b_ref[...].astype(BF16), preferred_element_type=F32)

    @pl.when(k == nk - 1)
    def _():
        _epilogue(kind, acc_ref[...], extras, o_ref, stage_ref, dil, j, alpha)


def _matmul(a, b, layer, *, kind="none", extras=(), out_dtype=BF16, name="mm",
            tn=None, n_tiles=None, src_block=None, dil=1, alpha=None):
    m, k = a.shape
    nb = b.shape[2]
    tm = _pick(m, (1024, 512, 256))
    if tn is None:
        tn = nb if nb <= 1280 else _pick(nb, (1024, 512, 256, 128))
    if n_tiles is None:
        n_tiles = nb // tn
    if src_block is None:
        src_block = lambda j: j
    itemsize = jnp.dtype(b.dtype).itemsize
    tk = k
    for cand in (k, 4096, 2048, 1024, 512):
        if cand <= k and k % cand == 0:
            tk = cand
            if cand * tn * itemsize <= B_TILE_BYTES:
                break
    nk = k // tk
    assert m % tm == 0 and k % tk == 0 and tm % dil == 0
    in_specs = [
        pl.BlockSpec((tm, tk), lambda i, j, kk: (i, kk)),
        pl.BlockSpec((None, tk, tn), lambda i, j, kk: (layer, kk, src_block(j))),
    ]
    ops = [a, b]
    for arr, bshape, imap in extras:
        in_specs.append(pl.BlockSpec(bshape, functools.partial(lambda i, j, kk, f: f(i, j), f=imap)))
        ops.append(arr)
    scratch = [pltpu.VMEM((tm, tn), F32)] if nk > 1 else []
    if kind == "a_group":
        scratch.append(pltpu.VMEM((tn // LANES, tm, LANES), F32))
        out_shape = jax.ShapeDtypeStruct((n_tiles, dil, m // dil, tn), out_dtype)
        out_spec = pl.BlockSpec((None, dil, tm // dil, tn), lambda i, j, kk: (j, 0, i, 0))
    else:
        out_shape = jax.ShapeDtypeStruct((m, n_tiles * tn), out_dtype)
        out_spec = pl.BlockSpec((tm, tn), lambda i, j, kk: (i, j))
    return pl.pallas_call(
        functools.partial(_mm_kernel, nk=nk, kind=kind, n_extra=len(extras), dil=dil, alpha=alpha),
        grid=(m // tm, n_tiles, nk),
        in_specs=in_specs,
        out_specs=out_spec,
        out_shape=out_shape,
        scratch_shapes=scratch,
        compiler_params=_params(("parallel", "parallel", "arbitrary")),
        name=name,
    )(*ops)


def _panel_kernel(*refs, kind, n_extra, dil, shift, trans_b, tn, alpha):
    a_ref, b_ref = refs[0], refs[1]
    n_b = 2 if shift else 1
    extras = refs[1 + n_b:1 + n_b + n_extra]
    o_ref = refs[1 + n_b + n_extra]
    scratch = list(refs[2 + n_b + n_extra:])
    bb_ref = scratch.pop(0)
    stage_ref = scratch.pop(0) if kind == "a_group" else None
    j = pl.program_id(0)
    i = pl.program_id(1)

    @pl.when(i == 0)
    def _():
        w = b_ref[...].astype(BF16)
        if kind == "kiwi":
            q = IDX_DIM
            bb_ref[0:q, :] = w[0:q]
            bb_ref[q:2 * q, :] = w[0:q]
            bb_ref[2 * q:3 * q, :] = w[q:2 * q]
            bb_ref[3 * q:4 * q, :] = w[q:2 * q]
        elif trans_b:
            bb_ref[:b_ref.shape[0], :] = w
            if shift:
                bb_ref[b_ref.shape[0]:, :] = refs[2][...].astype(BF16)
        else:
            bb_ref[...] = w

    if trans_b:
        acc = lax.dot_general(a_ref[...], bb_ref[shift:shift + tn, :], NT_DIMS, preferred_element_type=F32)
    else:
        acc = jnp.dot(a_ref[...], bb_ref[...], preferred_element_type=F32)
    _epilogue(kind, acc, extras, o_ref, stage_ref, dil, j, alpha)


def _panel_matmul(a, b, layer, *, tn, n_tiles, src_block, kind="none", extras=(), out_dtype=BF16,
                  name="mm", dil=1, shift=0, trans_b=False, alpha=None):
    m, k = a.shape
    tm = _pick(m, (1024, 512, 256))
    assert m % tm == 0 and tm % dil == 0 and tn % LANES == 0
    assert shift == 0 or (trans_b and shift % 16 == 0 and shift < LANES)
    out_tn = 2 * tn if kind == "kiwi" else tn
    b_rows = tn + (LANES if shift else 0)
    in_specs = [pl.BlockSpec((tm, k), lambda j, i: (i, 0))]
    ops = [a, b]
    if trans_b:
        in_specs.append(pl.BlockSpec((None, tn, k), lambda j, i: (layer, src_block(j), 0)))
        if shift:
            per = tn // LANES
            in_specs.append(pl.BlockSpec((None, LANES, k), lambda j, i: (layer, (src_block(j) + 1) * per, 0)))
            ops.append(b)
        scratch = [pltpu.VMEM((out_tn if kind == "kiwi" else b_rows, k), BF16)]
    else:
        in_specs.append(pl.BlockSpec((None, k, tn), lambda j, i: (layer, 0, src_block(j))))
        scratch = [pltpu.VMEM((k, tn), BF16)]
    for arr, bshape, imap in extras:
        in_specs.append(pl.BlockSpec(bshape, functools.partial(lambda j, i, f: f(i, j), f=imap)))
        ops.append(arr)
    if kind == "a_group":
        scratch.append(pltpu.VMEM((tn // LANES, tm, LANES), F32))
        out_shape = jax.ShapeDtypeStruct((n_tiles, dil, m // dil, tn), out_dtype)
        out_spec = pl.BlockSpec((None, dil, tm // dil, tn), lambda j, i: (j, 0, i, 0))
    else:
        out_shape = jax.ShapeDtypeStruct((m, n_tiles * out_tn), out_dtype)
        out_spec = pl.BlockSpec((tm, out_tn), lambda j, i: (i, j))
    return pl.pallas_call(
        functools.partial(_panel_kernel, kind=kind, n_extra=len(extras), dil=dil, shift=shift,
                          trans_b=trans_b, tn=out_tn, alpha=alpha),
        grid=(n_tiles, m // tm),
        in_specs=in_specs,
        out_specs=out_spec,
        out_shape=out_shape,
        scratch_shapes=scratch,
        compiler_params=_params(("arbitrary", "arbitrary")),
        name=name,
    )(*ops)


def _ln_kernel(z_ref, g_ref, b_ref, o_ref, ob_ref):
    z = z_ref[...]
    mu = jnp.mean(z, axis=-1, keepdims=True)
    zc = z - mu
    var = jnp.mean(zc * zc, axis=-1, keepdims=True)
    out = (zc * lax.rsqrt(var + LN_EPS)) * g_ref[...] + b_ref[...]
    o_ref[...] = out
    ob_ref[...] = out.astype(BF16)


def _layer_norm(z, g, b, layer):
    t, d = z.shape
    tm = _pick(t, (512, 256, 128))
    row = pl.BlockSpec((tm, d), lambda i: (i, 0))
    par = pl.BlockSpec((None, 1, d), lambda i: (layer, 0, 0))
    return pl.pallas_call(
        _ln_kernel,
        grid=(t // tm,),
        in_specs=[row, par, par],
        out_specs=[row, row],
        out_shape=[jax.ShapeDtypeStruct((t, d), F32), jax.ShapeDtypeStruct((t, d), BF16)],
        compiler_params=_params(("parallel",)),
        name="layer_norm",
    )(z, g, b)


def _merge_kernel(ya_ref, wa_ref, yb_ref, wb_ref, ga_ref, gb_ref, o_ref):
    pa = jnp.dot(ya_ref[...], wa_ref[...].astype(BF16), preferred_element_type=F32)
    pb = jnp.dot(yb_ref[...], wb_ref[...].astype(BF16), preferred_element_type=F32)
    o_ref[...] = (ga_ref[...].astype(F32) * pa + gb_ref[...].astype(F32) * pb).astype(o_ref.dtype)


def _gated_merge(ya, yb, w_pa, w_pb, gates, layer):
    t = ya.shape[0]
    d = w_pa.shape[2]
    tm = _pick(t, (1024, 512, 256))
    tn = d if d <= 1280 else _pick(d, (1024, 512, 256, 128))
    ka, kb = ya.shape[1], yb.shape[1]
    return pl.pallas_call(
        _merge_kernel,
        grid=(t // tm, d // tn),
        in_specs=[
            pl.BlockSpec((tm, ka), lambda i, j: (i, 0)),
            pl.BlockSpec((None, ka, tn), lambda i, j: (layer, 0, j)),
            pl.BlockSpec((tm, kb), lambda i, j: (i, 0)),
            pl.BlockSpec((None, kb, tn), lambda i, j: (layer, 0, j)),
            pl.BlockSpec((tm, tn), lambda i, j: (i, j)),
            pl.BlockSpec((tm, tn), lambda i, j: (i, j + d // tn)),
        ],
        out_specs=pl.BlockSpec((tm, tn), lambda i, j: (i, j)),
        out_shape=jax.ShapeDtypeStruct((t, d), BF16),
        compiler_params=_params(("parallel", "parallel")),
        name="gated_merge",
    )(ya, w_pa, yb, w_pb, gates, gates)


def _dil_kernel(q_ref, kp_ref, kc_ref, vp_ref, vc_ref, o_ref, lse_ref, *, nsub):
    i = pl.program_id(1)
    qrows = min(nsub, 2) * BAND
    qq = lax.broadcasted_iota(jnp.int32, (qrows, BAND + qrows), 0)
    kk = lax.broadcasted_iota(jnp.int32, (qrows, BAND + qrows), 1)
    in_band = (kk >= qq) & (kk <= qq + BAND)
    bias = jnp.where(in_band, 0.0, NEG_INF)
    bias_first = jnp.where(in_band & (kk >= jnp.where(i > 0, 0, BAND)), 0.0, NEG_INF)
    scale = HEAD_DIM ** -0.5
    chains = [(g, h) for g in range(nsub * BAND // qrows) for h in range(A_HEADS_PER_GROUP)]
    scores, v_bands = [], []
    for g, h in chains:
        rows = slice(g * qrows, (g + 1) * qrows)
        band = slice(g * qrows - BAND, (g + 1) * qrows)
        sl = slice(h * HEAD_DIM, (h + 1) * HEAD_DIM)
        if g == 0:
            k_band = jnp.concatenate([kp_ref[:, sl], kc_ref[rows, sl]], axis=0)
            v_bands.append(jnp.concatenate([vp_ref[:, sl], vc_ref[rows, sl]], axis=0))
        else:
            k_band = kc_ref[band, sl]
            v_bands.append(vc_ref[band, sl])
        s = lax.dot_general(q_ref[rows, sl], k_band, NT_DIMS, preferred_element_type=F32) * scale
        scores.append(s + (bias_first if g == 0 else bias))
    maxes = [jnp.max(s, axis=-1, keepdims=True) for s in scores]
    exps = [jnp.exp(s - m) for s, m in zip(scores, maxes)]
    unnorm = [jnp.dot(e.astype(BF16), v, preferred_element_type=F32) for e, v in zip(exps, v_bands)]
    sums = [jnp.sum(e, axis=-1, keepdims=True) for e in exps]
    outs = [o * (1.0 / l) for o, l in zip(unnorm, sums)]
    lses = [jnp.broadcast_to(m + jnp.log(l), (qrows, HEAD_DIM)) for m, l in zip(maxes, sums)]
    nh = A_HEADS_PER_GROUP
    ngrp = len(outs) // nh
    o_ref[...] = jnp.concatenate(
        [jnp.concatenate(outs[g * nh:(g + 1) * nh], axis=1) for g in range(ngrp)], axis=0)
    lse_ref[...] = jnp.concatenate(
        [jnp.concatenate(lses[g * nh:(g + 1) * nh], axis=1) for g in range(ngrp)], axis=0)


def _dilated_group(qkv, dil):
    sub = qkv.shape[2]
    assert sub % BAND == 0
    nsub = _pick(sub // BAND, (4, 2, 1))
    rows = nsub * BAND

    def spec(which):
        return pl.BlockSpec((None, None, rows, A_OUT), lambda r, i: (which, r, i, 0))

    def prev_spec(which):
        return pl.BlockSpec((None, None, BAND, A_OUT),
                            lambda r, i: (which, r, jnp.maximum(i * nsub - 1, 0), 0))

    out_spec = pl.BlockSpec((None, rows, A_OUT), lambda r, i: (r, i, 0))
    return pl.pallas_call(
        functools.partial(_dil_kernel, nsub=nsub),
        grid=(dil, sub // rows),
        in_specs=[spec(0), prev_spec(1), spec(1), prev_spec(2), spec(2)],
        out_specs=[out_spec, out_spec],
        out_shape=[jax.ShapeDtypeStruct((dil, sub, A_OUT), F32)] * 2,
        compiler_params=_params(("parallel", "parallel")),
        name=f"dilated_attn_d{dil}",
    )(qkv, qkv, qkv, qkv, qkv)


def _mix_kernel(*refs, dils):
    n = len(dils)
    o_refs, l_refs, y_ref = refs[:n], refs[n:2 * n], refs[2 * n]
    stage = refs[2 * n + 1:]
    tm = y_ref.shape[0]

    def natural(ref, slot, dil, cs):
        if dil == 1:
            return ref[0, :, cs]
        for r in range(dil):
            slot[pl.ds(r, tm // dil, stride=dil), :] = ref[r, :, cs]
        return slot[...]

    for c in range(A_OUT // LANES):
        cs = slice(c * LANES, (c + 1) * LANES)
        outs = [natural(o_refs[g], stage[2 * g], dils[g], cs) for g in range(n)]
        lses = [natural(l_refs[g], stage[2 * g + 1], dils[g], cs) for g in range(n)]
        m = functools.reduce(jnp.maximum, lses)
        es = [jnp.exp(l - m) for l in lses]
        inv = 1.0 / functools.reduce(lambda a, b: a + b, es)
        y = functools.reduce(lambda a, b: a + b, [(e * inv) * o for e, o in zip(es, outs)])
        y_ref[:, cs] = y.astype(y_ref.dtype)


def _mix_groups(outs, lses, dils, t):
    tm = _pick(t, (512, 256))
    specs = [pl.BlockSpec((d, tm // d, A_OUT), lambda i: (0, i, 0)) for d in dils]
    return pl.pallas_call(
        functools.partial(_mix_kernel, dils=tuple(dils)),
        grid=(t // tm,),
        in_specs=specs + specs,
        out_specs=pl.BlockSpec((tm, A_OUT), lambda i: (i, 0)),
        out_shape=jax.ShapeDtypeStruct((t, A_OUT), BF16),
        scratch_shapes=[pltpu.VMEM((tm, LANES), F32)] * (2 * len(dils)),
        compiler_params=_params(("parallel",)),
        name="mix_groups",
    )(*outs, *lses)


def _sortable(x):
    b = pltpu.bitcast(x, jnp.int32)
    return b ^ ((b >> 31) & jnp.int32(INT_MAX))


def _dsa_kernel(step_q_ref, step_c_ref, qi_ref, ki_ref, wi_ref, q_ref, k_ref, v_ref, o_ref,
                key_ref, bias_ref, qs_ref, wb_ref, thr_ref, mthr_ref, m_ref, acc_ref,
                *, tq, tkc, topk):
    i = step_q_ref[pl.program_id(0)]
    c = step_c_ref[pl.program_id(0)]
    last_c = ((i + 1) * tq - 1) // tkc
    n_ch = last_c + 1
    row_pos = i * tq + lax.broadcasted_iota(jnp.int32, (tq, tkc), 0)
    col_iota = lax.broadcasted_iota(jnp.int32, (tq, tkc), 1)
    sub = MXU_COLS
    row_sub = i * tq + lax.broadcasted_iota(jnp.int32, (tq, sub), 0)
    col_sub = lax.broadcasted_iota(jnp.int32, (tq, sub), 1)

    @pl.when(c == 0)
    def _select():
        lo_half = lax.broadcasted_iota(jnp.int32, (tq, LANES), 1) < IDX_DIM
        for hp in range(IDX_HEADS // 2):
            pair = qi_ref[:, hp * LANES:(hp + 1) * LANES]
            qs_ref[(2 * hp) * tq:(2 * hp + 1) * tq, :] = jnp.where(lo_half, pair, 0.0).astype(BF16)
            qs_ref[(2 * hp + 1) * tq:(2 * hp + 2) * tq, :] = jnp.where(lo_half, 0.0, pair).astype(BF16)
        w = wi_ref[...] * (IDX_DIM ** -0.5 * IDX_HEADS ** -0.5)
        for h in range(IDX_HEADS):
            wb_ref[h] = w[:, h:h + 1] + jnp.zeros((tq, LANES), F32)

        def score_chunk(kc, carry):
            off = pl.multiple_of(kc * tkc, tkc)
            for s in range(tkc // sub):
                kis = ki_ref[pl.ds(off + s * sub, sub), :].astype(BF16)
                r = lax.dot_general(qs_ref[...], kis, NT_DIMS, preferred_element_type=F32)
                halves = []
                for half in range(sub // LANES):
                    cs = slice(half * LANES, (half + 1) * LANES)
                    sc = jnp.zeros((tq, LANES), F32)
                    for h in range(IDX_HEADS):
                        sc = sc + wb_ref[h] * jnp.maximum(r[h * tq:(h + 1) * tq, cs], 0.0)
                    halves.append(sc)
                sc = jnp.concatenate(halves, axis=1)
                cols = slice(s * sub, (s + 1) * sub)
                sc = jnp.where(off + s * sub + col_sub <= row_sub, sc, NEG_INF)
                key_ref[kc, :, cols] = _sortable(sc)
            return carry

        lax.fori_loop(0, n_ch, score_chunk, 0)

        def count(pred):
            def body(kc, acc):
                off = pl.multiple_of(kc * tkc, tkc)
                part = jnp.where(pred(key_ref[kc], off), 1, 0).astype(jnp.int32)
                for s in range(tkc // LANES):
                    acc = acc + part[:, s * LANES:(s + 1) * LANES]
                return acc
            acc = lax.fori_loop(0, n_ch, body, jnp.zeros((tq, LANES), jnp.int32))
            return jnp.sum(acc, axis=1, keepdims=True)

        def unsettled(state):
            it, lo, hi, n_lo = state
            open_rows = jnp.where((n_lo == topk) | (lo == hi), 0, 1)
            return jnp.logical_and(it < 34, jnp.max(open_rows) > 0)

        def bisect(state):
            it, lo, hi, n_lo = state
            mid = (lo | hi) - ((lo ^ hi) >> 1)
            n_mid = count(lambda kk, off: kk >= mid)
            ok = n_mid >= topk
            return (it + 1, jnp.where(ok, mid, lo), jnp.where(ok, hi, mid - 1), jnp.where(ok, n_mid, n_lo))

        lo0 = jnp.full((tq, 1), INT_MIN, jnp.int32)
        hi0 = jnp.full((tq, 1), INT_MAX, jnp.int32)
        n0 = jnp.full((tq, 1), INT_MAX, jnp.int32)
        _, thr, _, n_ge = lax.while_loop(unsettled, lambda st: bisect(bisect(st)),
                                         (jnp.int32(0), lo0, hi0, n0))
        thr_ref[...] = thr
        mthr_ref[...] = jnp.full((tq, 1), INT_MAX, jnp.int32)

        @pl.when(jnp.max(n_ge) > topk)
        def _ties():
            need = topk - count(lambda kk, off: kk > thr)

            def bisect_idx(_, lohi):
                lo, hi = lohi
                mid = (lo + hi) >> 1
                ok = count(lambda kk, off: (kk == thr) & (off + col_iota <= mid)) >= need
                return jnp.where(ok, lo, mid + 1), jnp.where(ok, mid, hi)

            lo1 = jnp.zeros((tq, 1), jnp.int32)
            hi1 = jnp.full((tq, 1), 2 ** 30, jnp.int32)
            _, m_idx = lax.fori_loop(0, 31, bisect_idx, (lo1, hi1))
            mthr_ref[...] = m_idx

        m_ref[...] = jnp.full(m_ref.shape, M_FLOOR, F32)
        acc_ref[...] = jnp.zeros(acc_ref.shape, F32)

    @pl.when(c <= last_c)
    def _attend():
        off = pl.multiple_of(c * tkc, tkc)
        kk = key_ref[c]
        thr = thr_ref[...]
        pos = off + col_iota
        tie = jnp.where(kk == thr, jnp.where(pos <= mthr_ref[...], 0.0, NEG_INF), NEG_INF)
        chosen = jnp.where(kk > thr, 0.0, tie)
        bias_ref[...] = jnp.where(pos <= row_pos, chosen, NEG_INF)
        ones = jnp.ones((tkc, HEAD_DIM), BF16)
        for h in range(B_HEADS):
            sl = slice(h * HEAD_DIM, (h + 1) * HEAD_DIM)
            s = lax.dot_general(q_ref[:, sl], k_ref[:, sl], NT_DIMS, preferred_element_type=F32) + bias_ref[...]
            m_old = m_ref[h]
            m_new = jnp.maximum(m_old, jnp.max(s, axis=-1, keepdims=True))
            p = jnp.exp2(s - m_new).astype(BF16)
            alpha = jnp.exp2(m_old - m_new)
            v_one = jnp.concatenate([v_ref[:, sl], ones], axis=1)
            acc_ref[h] = alpha * acc_ref[h] + jnp.dot(p, v_one, preferred_element_type=F32)
            m_ref[h] = m_new

    @pl.when(c == last_c)
    def _finish():
        for h in range(B_HEADS):
            a = acc_ref[h]
            o_ref[:, h * HEAD_DIM:(h + 1) * HEAD_DIM] = (a[:, :HEAD_DIM] / a[:, HEAD_DIM:]).astype(o_ref.dtype)


def _dsa(qi, kiwi, qk, v):
    t = qi.shape[0]
    tq = _pick(t, (256, 128))
    tkc = _pick(t, (1024, 512, 256))
    topk = min(DSA_TOPK, t // 4)
    assert tkc >= topk and t % tq == 0 and tkc % MXU_COLS == 0
    nq, nc = t // tq, t // tkc
    steps = [(i, c) for i in range(nq) for c in range(((i + 1) * tq - 1) // tkc + 1)]
    step_q = jnp.asarray([s[0] for s in steps], jnp.int32)
    step_c = jnp.asarray([s[1] for s in steps], jnp.int32)

    in_specs = [
        pl.BlockSpec((tq, IDX_WIDTH), lambda s, sq, sc: (sq[s], 0)),
        pl.BlockSpec((t, LANES), lambda s, sq, sc: (0, 0)),
        pl.BlockSpec((tq, LANES), lambda s, sq, sc: (sq[s], 1)),
        pl.BlockSpec((tq, B_WIDTH), lambda s, sq, sc: (sq[s], 0)),
        pl.BlockSpec((tkc, B_WIDTH), lambda s, sq, sc: (sc[s], 1)),
        pl.BlockSpec((tkc, B_WIDTH), lambda s, sq, sc: (sc[s], 0)),
    ]
    grid_spec = pltpu.PrefetchScalarGridSpec(
        num_scalar_prefetch=2,
        grid=(len(steps),),
        in_specs=in_specs,
        out_specs=pl.BlockSpec((tq, B_WIDTH), lambda s, sq, sc: (sq[s], 0)),
        scratch_shapes=[
            pltpu.VMEM((nc, tq, tkc), jnp.int32),
            pltpu.VMEM((tq, tkc), F32),
            pltpu.VMEM((IDX_HEADS * tq, LANES), BF16),
            pltpu.VMEM((IDX_HEADS, tq, LANES), F32),
            pltpu.VMEM((tq, 1), jnp.int32),
            pltpu.VMEM((tq, 1), jnp.int32),
            pltpu.VMEM((B_HEADS, tq, 1), F32),
            pltpu.VMEM((B_HEADS, tq, 2 * HEAD_DIM), F32),
        ],
    )
    return pl.pallas_call(
        functools.partial(_dsa_kernel, tq=tq, tkc=tkc, topk=topk),
        grid_spec=grid_spec,
        out_shape=jax.ShapeDtypeStruct((t, B_WIDTH), BF16),
        compiler_params=_params(("arbitrary",)),
        name="dsa",
    )(step_q, step_c, qi, kiwi, kiwi, qk, qk, v)


def _mem_layer_kernel(xb_ref, x_ref, wq_ref, kv_ref, wo_ref, g_ref, b_ref, o_ref, ob_ref, *, alpha):
    scale = HEAD_DIM ** -0.5
    q = jnp.dot(xb_ref[...], wq_ref[...], preferred_element_type=F32).astype(BF16)
    ksl = [slice(h * HEAD_DIM, (h + 1) * HEAD_DIM) for h in range(MEM_HEADS)]
    vsl = [slice(MEM_WIDTH + h * HEAD_DIM, MEM_WIDTH + (h + 1) * HEAD_DIM) for h in range(MEM_HEADS)]
    scores = [lax.dot_general(q[:, sl], kv_ref[:, sl], NT_DIMS, preferred_element_type=F32) * scale
              for sl in ksl]
    maxes = [jnp.max(s, axis=-1, keepdims=True) for s in scores]
    exps = [jnp.exp(s - m) for s, m in zip(scores, maxes)]
    unnorm = [jnp.dot(e.astype(BF16), kv_ref[:, sl], preferred_element_type=F32) for e, sl in zip(exps, vsl)]
    sums = [jnp.sum(e, axis=-1, keepdims=True) for e in exps]
    om = jnp.concatenate([o * (1.0 / l) for o, l in zip(unnorm, sums)], axis=1).astype(BF16)
    z = alpha * x_ref[...] + jnp.dot(om, wo_ref[...], preferred_element_type=F32)
    mu = jnp.mean(z, axis=-1, keepdims=True)
    zc = z - mu
    var = jnp.mean(zc * zc, axis=-1, keepdims=True)
    out = (zc * lax.rsqrt(var + LN_EPS)) * g_ref[...] + b_ref[...]
    o_ref[...] = out
    ob_ref[...] = out.astype(BF16)


def _mem_layer(xb, x, wq, kv, wo, g, b, layer, alpha):
    t, d = x.shape
    tm = _pick(t, (256, 128))
    row = lambda w: pl.BlockSpec((tm, w), lambda i: (i, 0))
    whole = lambda a: pl.BlockSpec(a.shape, lambda i: (0, 0))
    par = pl.BlockSpec((None, 1, d), lambda i: (layer, 0, 0))
    return pl.pallas_call(
        functools.partial(_mem_layer_kernel, alpha=alpha),
        grid=(t // tm,),
        in_specs=[row(d), row(d), whole(wq), whole(kv), whole(wo), par, par],
        out_specs=[row(d), row(d)],
        out_shape=[jax.ShapeDtypeStruct((t, d), F32), jax.ShapeDtypeStruct((t, d), BF16)],
        compiler_params=_params(("parallel",)),
        name="mem_layer",
    )(xb, x, wq, kv, wo, g, b)


def _cast_layer_kernel(w_ref, o_ref):
    o_ref[...] = w_ref[...].astype(o_ref.dtype)


def _layer_to_bf16(w, layer):
    _, r, c = w.shape
    tr = _pick(r, (512, 256, 128))
    return pl.pallas_call(
        _cast_layer_kernel,
        grid=(r // tr,),
        in_specs=[pl.BlockSpec((None, tr, c), lambda i: (layer, i, 0))],
        out_specs=pl.BlockSpec((tr, c), lambda i: (i, 0)),
        out_shape=jax.ShapeDtypeStruct((r, c), BF16),
        compiler_params=_params(("parallel",)),
        name="cast_layer",
    )(w)


def _cast_kernel(x_ref, o_ref):
    o_ref[...] = x_ref[...].astype(o_ref.dtype)


def _to_bf16(x):
    t, d = x.shape
    tm = _pick(t, (256, 128))
    spec = pl.BlockSpec((tm, d), lambda i: (i, 0))
    return pl.pallas_call(
        _cast_kernel, grid=(t // tm,), in_specs=[spec], out_specs=spec,
        out_shape=jax.ShapeDtypeStruct((t, d), BF16),
        compiler_params=_params(("parallel",)), name="cast_bf16",
    )(x)


def _rope_tables(t):
    pos = jnp.arange(t, dtype=jnp.int32).astype(F32)[:, None]
    half = HEAD_DIM // 2
    inv = ROPE_THETA ** (-jnp.arange(half, dtype=F32) / half)
    ang = pos * inv[None, :]
    cos128 = jnp.concatenate([jnp.cos(ang)] * 2, axis=1)
    sin128 = jnp.concatenate([-jnp.sin(ang), jnp.sin(ang)], axis=1)
    half = IDX_DIM // 2
    inv = ROPE_THETA ** (-jnp.arange(half, dtype=F32) / half)
    ang = pos * inv[None, :]
    z = jnp.zeros_like(ang)
    cos64 = jnp.concatenate([jnp.cos(ang)] * 4, axis=1)
    sin_lo = jnp.concatenate([-jnp.sin(ang), z] * 2, axis=1)
    sin_hi = jnp.concatenate([z, jnp.sin(ang)] * 2, axis=1)
    return (cos128, sin128), (cos64, sin_lo, sin_hi)


def _mixer_branches(xb, w_in_t, l, blk, rope128, rope64):
    t = xb.shape[0]
    proj = functools.partial(_panel_matmul, xb, w_in_t, l, trans_b=True)
    outs, lses, dils = [], [], []
    for g, (win, dil) in enumerate(DIL_GROUPS):
        assert win // dil == BAND
        qkv = proj(kind="a_group", extras=rope128, tn=COL_BLOCK, n_tiles=3, dil=dil,
                   src_block=lambda j, g=g: blk[0] + g + j * (A_WIDTH // COL_BLOCK), name=f"proj_a_g{g}")
        o, lse = _dilated_group(qkv, dil)
        outs.append(o)
        lses.append(lse)
        dils.append(dil)
    ya = _mix_groups(outs, lses, dils, t)
    qk_b = proj(kind="rope128_qk", extras=rope128, tn=COL_BLOCK, n_tiles=2 * B_WIDTH // COL_BLOCK,
                src_block=lambda j: blk[3] + j, name="proj_b_qk")
    v_b = proj(tn=COL_BLOCK, n_tiles=B_WIDTH // COL_BLOCK, src_block=lambda j: blk[5] + j, name="proj_b_v")
    qi = proj(kind="rope64", extras=rope64, out_dtype=F32, tn=COL_BLOCK, n_tiles=IDX_WIDTH // COL_BLOCK,
              src_block=lambda j: blk[6] + j, name="proj_qi")
    kiwi = proj(kind="kiwi", extras=rope64, out_dtype=F32, tn=LANES, n_tiles=1,
                src_block=lambda j: blk[7] * (COL_BLOCK // LANES), name="proj_kiwi")
    yb = _dsa(qi, kiwi, qk_b, v_b)
    return ya, yb


def kernel(x, mem, w_in, w_pa, w_pb, w_o, ln1_g, ln1_b, w_mq, w_mkv, w_mo, ln2_g, ln2_b, w_up, w_down, ln3_g, ln3_b):
    bsz, t, d = x.shape
    depth = w_in.shape[0]
    assert bsz == 1
    alpha = (2 * depth) ** 0.25

    w_in_t = jnp.swapaxes(w_in, 1, 2)
    sizes = (A_WIDTH, A_WIDTH, A_WIDTH, B_WIDTH, B_WIDTH, B_WIDTH, IDX_WIDTH, IDX_DIM, IDX_HEADS, d, d)
    offs = [0]
    for s in sizes:
        offs.append(offs[-1] + s)
    assert offs[-1] == w_in.shape[2] and all(o % COL_BLOCK == 0 for o in offs[:8])
    blk = [o // COL_BLOCK for o in offs[:8]]
    gate_shift = offs[9] - offs[7]
    assert 0 < gate_shift < LANES and (2 * d) % COL_BLOCK == 0
    lnp = [p.reshape(depth, 1, d) for p in (ln1_g, ln1_b, ln2_g, ln2_b, ln3_g, ln3_b)]

    (cos128, sin128), (cos64, sin_lo, sin_hi) = _rope_tables(t)
    tm_rows = _pick(t, (1024, 512, 256))

    xf = x[0]
    xb = _to_bf16(xf)
    memb = _to_bf16(mem[0])

    def row_tab(arr):
        return (arr, (tm_rows, LANES), lambda i, j: (i, 0))

    def tile(arr, tn, joff=0):
        return (arr, (tm_rows, tn), lambda i, j: (i, j + joff))

    rope128 = (row_tab(cos128), row_tab(sin128))
    rope64 = (row_tab(cos64), row_tab(sin_lo), row_tab(sin_hi))
    tn_d = d if d <= 1280 else _pick(d, (1024, 512, 256, 128))
    tn_p = _pick(d, (COL_BLOCK, 256, 128))

    for l in range(depth):
        ya, yb = _mixer_branches(xb, w_in_t, l, blk, rope128, rope64)
        gates = _panel_matmul(xb, w_in_t, l, kind="sigmoid", tn=COL_BLOCK, trans_b=True,
                              n_tiles=2 * d // COL_BLOCK, src_block=lambda j: blk[7] + j, shift=gate_shift,
                              name="proj_gate")
        merged = _gated_merge(ya, yb, w_pa, w_pb, gates, l)
        z = _panel_matmul(merged, w_o, l, kind="resid", extras=(tile(xf, tn_p),), alpha=alpha, out_dtype=F32,
                          tn=tn_p, n_tiles=d // tn_p, src_block=lambda j: j, name="mixer_out")
        xf, xb = _layer_norm(z, lnp[0], lnp[1], l)
        kvm = _matmul(memb, w_mkv, l, name="mem_kv")
        xf, xb = _mem_layer(xb, xf, _layer_to_bf16(w_mq, l), kvm, _layer_to_bf16(w_mo, l),
                            lnp[2], lnp[3], l, alpha)
        hdn = _panel_matmul(xb, w_up, l, kind="relu2", tn=tn_p, n_tiles=w_up.shape[2] // tn_p,
                            src_block=lambda j: j, name="mlp_up")
        z = _matmul(hdn, w_down, l, kind="resid", extras=(tile(xf, tn_d),), alpha=alpha, out_dtype=F32,
                    name="mlp_down")
        xf, xb = _layer_norm(z, lnp[4], lnp[5], l)
    return xf[None]
```

```python
import functools

import jax
import jax.numpy as jnp
from jax import lax
from jax.experimental import pallas as pl
from jax.experimental.pallas import tpu as pltpu

F32 = jnp.float32
BF16 = jnp.bfloat16

HEAD_DIM = 128
DIL_GROUPS = ((128, 1), (512, 4), (2048, 16))
A_HEADS_PER_GROUP = 4
A_GROUPS = len(DIL_GROUPS)
A_WIDTH = A_GROUPS * A_HEADS_PER_GROUP * HEAD_DIM
A_OUT = A_HEADS_PER_GROUP * HEAD_DIM
B_HEADS = 8
B_WIDTH = B_HEADS * HEAD_DIM
IDX_HEADS = 16
IDX_DIM = 64
IDX_WIDTH = IDX_HEADS * IDX_DIM
DSA_TOPK = 256
MEM_HEADS = 4
MEM_WIDTH = MEM_HEADS * HEAD_DIM
ROPE_THETA = 10000.0
LN_EPS = 1e-5
NEG_INF = -1e30
M_FLOOR = -1e29
BAND = 128
LOG2E = 1.4426950408889634

LANES = 128
MXU_COLS = 256
VMEM_LIMIT = 56 * 1024 * 1024
B_TILE_BYTES = 8 * 1024 * 1024
COL_BLOCK = 512

INT_MIN = -(2 ** 31)
INT_MAX = 2 ** 31 - 1
NT_DIMS = (((1,), (1,)), ((), ()))


def _pick(n, prefs):
    for p in prefs:
        if n % p == 0:
            return p
    return n


def _params(sem):
    return pltpu.CompilerParams(dimension_semantics=sem, vmem_limit_bytes=VMEM_LIMIT)


def _rope128(x, cos, sin):
    return x * cos + pltpu.roll(x, 64, 1) * sin


def _rope64(x, cos, sin_lo, sin_hi):
    return x * cos + pltpu.roll(x, 96, 1) * sin_lo + pltpu.roll(x, 32, 1) * sin_hi


def _rope128_tile(acc, cos, sin):
    return [_rope128(acc[:, c * LANES:(c + 1) * LANES], cos, sin) for c in range(acc.shape[1] // LANES)]


def _epilogue(kind, acc, extras, o_ref, stage_ref, dil, j, alpha):
    tn = acc.shape[1]
    if kind == "none":
        o_ref[...] = acc.astype(o_ref.dtype)
    elif kind == "resid":
        o_ref[...] = (alpha * extras[0][...] + acc).astype(o_ref.dtype)
    elif kind == "rope128":
        for c, y in enumerate(_rope128_tile(acc, extras[0][...], extras[1][...])):
            o_ref[:, c * LANES:(c + 1) * LANES] = y.astype(o_ref.dtype)
    elif kind == "rope128_qk":
        qs = jnp.where(j < B_WIDTH // COL_BLOCK, HEAD_DIM ** -0.5 * LOG2E, 1.0)
        for c, y in enumerate(_rope128_tile(acc, extras[0][...], extras[1][...])):
            o_ref[:, c * LANES:(c + 1) * LANES] = (y * qs).astype(o_ref.dtype)
    elif kind == "rope64":
        cos, s_lo, s_hi = extras[0][...], extras[1][...], extras[2][...]
        for c in range(tn // LANES):
            sl = slice(c * LANES, (c + 1) * LANES)
            o_ref[:, sl] = _rope64(acc[:, sl], cos, s_lo, s_hi).astype(o_ref.dtype)
    elif kind == "kiwi":
        cos, s_lo, s_hi = extras[0][...], extras[1][...], extras[2][...]
        o_ref[:, :LANES] = _rope64(acc[:, :LANES], cos, s_lo, s_hi).astype(o_ref.dtype)
        o_ref[:, LANES:] = acc[:, LANES:].astype(o_ref.dtype)
    elif kind == "a_group":
        cos = jnp.where(j < 2, extras[0][...], 1.0)
        sin = jnp.where(j < 2, extras[1][...], 0.0)
        for c, y in enumerate(_rope128_tile(acc, cos, sin)):
            stage_ref[c] = y

        rows = acc.shape[0] // dil
        for r in range(dil):
            for c in range(tn // LANES):
                src = stage_ref[c] if dil == 1 else stage_ref[c, pl.ds(r, rows, stride=dil), :]
                o_ref[r, :, c * LANES:(c + 1) * LANES] = src.astype(o_ref.dtype)
    elif kind == "sigmoid":
        o_ref[...] = (1.0 / (1.0 + jnp.exp(-acc))).astype(o_ref.dtype)
    elif kind == "relu2":
        r = jnp.maximum(acc, 0.0)
        o_ref[...] = (r * r).astype(o_ref.dtype)
    else:
        raise ValueError(kind)


def _mm_kernel(*refs, nk, kind, n_extra, dil, alpha):
    a_ref, b_ref = refs[0], refs[1]
    extras = refs[2:2 + n_extra]
    o_ref = refs[2 + n_extra]
    scratch = list(refs[3 + n_extra:])
    acc_ref = scratch.pop(0) if nk > 1 else None
    stage_ref = scratch.pop(0) if kind == "a_group" else None
    j = pl.program_id(1)
    k = pl.program_id(2)
    if nk == 1:
        part = jnp.dot(a_ref[...], b_ref[...].astype(BF16), preferred_element_type=F32)
        _epilogue(kind, part, extras, o_ref, stage_ref, dil, j, alpha)
        return

    @pl.when(k == 0)
    def _():
        acc_ref[...] = jnp.zeros(acc_ref.shape, F32)

    acc_ref[...] += jnp.dot(a_ref[...], b_ref[...].astype(BF16), preferred_element_type=F32)

    @pl.when(k == nk - 1)
    def _():
        _epilogue(kind, acc_ref[...], extras, o_ref, stage_ref, dil, j, alpha)


def _matmul(a, b, layer, *, kind="none", extras=(), out_dtype=BF16, name="mm",
            tn=None, n_tiles=None, src_block=None, dil=1, alpha=None):
    m, k = a.shape
    nb = b.shape[2]
    tm = _pick(m, (1024, 512, 256))
    if tn is None:
        tn = nb if nb <= 1280 else _pick(nb, (1024, 512, 256, 128))
    if n_tiles is None:
        n_tiles = nb // tn
    if src_block is None:
        src_block = lambda j: j
    itemsize = jnp.dtype(b.dtype).itemsize
    tk = k
    for cand in (k, 4096, 2048, 1024, 512):
        if cand <= k and k % cand == 0:
            tk = cand
            if cand * tn * itemsize <= B_TILE_BYTES:
                break
    nk = k // tk
    assert m % tm == 0 and k % tk == 0 and tm % dil == 0
    in_specs = [
        pl.BlockSpec((tm, tk), lambda i, j, kk: (i, kk)),
        pl.BlockSpec((None, tk, tn), lambda i, j, kk: (layer, kk, src_block(j))),
    ]
    ops = [a, b]
    for arr, bshape, imap in extras:
        in_specs.append(pl.BlockSpec(bshape, functools.partial(lambda i, j, kk, f: f(i, j), f=imap)))
        ops.append(arr)
    scratch = [pltpu.VMEM((tm, tn), F32)] if nk > 1 else []
    if kind == "a_group":
        scratch.append(pltpu.VMEM((tn // LANES, tm, LANES), F32))
        out_shape = jax.ShapeDtypeStruct((n_tiles, dil, m // dil, tn), out_dtype)
        out_spec = pl.BlockSpec((None, dil, tm // dil, tn), lambda i, j, kk: (j, 0, i, 0))
    else:
        out_shape = jax.ShapeDtypeStruct((m, n_tiles * tn), out_dtype)
        out_spec = pl.BlockSpec((tm, tn), lambda i, j, kk: (i, j))
    return pl.pallas_call(
        functools.partial(_mm_kernel, nk=nk, kind=kind, n_extra=len(extras), dil=dil, alpha=alpha),
        grid=(m // tm, n_tiles, nk),
        in_specs=in_specs,
        out_specs=out_spec,
        out_shape=out_shape,
        scratch_shapes=scratch,
        compiler_params=_params(("parallel", "parallel", "arbitrary")),
        name=name,
    )(*ops)


def _panel_kernel(*refs, kind, n_extra, dil, shift, trans_b, tn, alpha):
    a_ref, b_ref = refs[0], refs[1]
    n_b = 2 if shift else 1
    extras = refs[1 + n_b:1 + n_b + n_extra]
    o_ref = refs[1 + n_b + n_extra]
    scratch = list(refs[2 + n_b + n_extra:])
    bb_ref = scratch.pop(0)
    stage_ref = scratch.pop(0) if kind == "a_group" else None
    j = pl.program_id(0)
    i = pl.program_id(1)

    @pl.when(i == 0)
    def _():
        w = b_ref[...].astype(BF16)
        if kind == "kiwi":
            q = IDX_DIM
            bb_ref[0:q, :] = w[0:q]
            bb_ref[q:2 * q, :] = w[0:q]
            bb_ref[2 * q:3 * q, :] = w[q:2 * q]
            bb_ref[3 * q:4 * q, :] = w[q:2 * q]
        elif trans_b:
            bb_ref[:b_ref.shape[0], :] = w
            if shift:
                bb_ref[b_ref.shape[0]:, :] = refs[2][...].astype(BF16)
        else:
            bb_ref[...] = w

    if trans_b:
        acc = lax.dot_general(a_ref[...], bb_ref[shift:shift + tn, :], NT_DIMS, preferred_element_type=F32)
    else:
        acc = jnp.dot(a_ref[...], bb_ref[...], preferred_element_type=F32)
    _epilogue(kind, acc, extras, o_ref, stage_ref, dil, j, alpha)


def _panel_matmul(a, b, layer, *, tn, n_tiles, src_block, kind="none", extras=(), out_dtype=BF16,
                  name="mm", dil=1, shift=0, trans_b=False, alpha=None):
    m, k = a.shape
    tm = _pick(m, (1024, 512, 256))
    assert m % tm == 0 and tm % dil == 0 and tn % LANES == 0
    assert shift == 0 or (trans_b and shift % 16 == 0 and shift < LANES)
    out_tn = 2 * tn if kind == "kiwi" else tn
    b_rows = tn + (LANES if shift else 0)
    in_specs = [pl.BlockSpec((tm, k), lambda j, i: (i, 0))]
    ops = [a, b]
    if trans_b:
        in_specs.append(pl.BlockSpec((None, tn, k), lambda j, i: (layer, src_block(j), 0)))
        if shift:
            per = tn // LANES
            in_specs.append(pl.BlockSpec((None, LANES, k), lambda j, i: (layer, (src_block(j) + 1) * per, 0)))
            ops.append(b)
        scratch = [pltpu.VMEM((out_tn if kind == "kiwi" else b_rows, k), BF16)]
    else:
        in_specs.append(pl.BlockSpec((None, k, tn), lambda j, i: (layer, 0, src_block(j))))
        scratch = [pltpu.VMEM((k, tn), BF16)]
    for arr, bshape, imap in extras:
        in_specs.append(pl.BlockSpec(bshape, functools.partial(lambda j, i, f: f(i, j), f=imap)))
        ops.append(arr)
    if kind == "a_group":
        scratch.append(pltpu.VMEM((tn // LANES, tm, LANES), F32))
        out_shape = jax.ShapeDtypeStruct((n_tiles, dil, m // dil, tn), out_dtype)
        out_spec = pl.BlockSpec((None, dil, tm // dil, tn), lambda j, i: (j, 0, i, 0))
    else:
        out_shape = jax.ShapeDtypeStruct((m, n_tiles * out_tn), out_dtype)
        out_spec = pl.BlockSpec((tm, out_tn), lambda j, i: (i, j))
    return pl.pallas_call(
        functools.partial(_panel_kernel, kind=kind, n_extra=len(extras), dil=dil, shift=shift,
                          trans_b=trans_b, tn=out_tn, alpha=alpha),
        grid=(n_tiles, m // tm),
        in_specs=in_specs,
        out_specs=out_spec,
        out_shape=out_shape,
        scratch_shapes=scratch,
        compiler_params=_params(("arbitrary", "arbitrary")),
        name=name,
    )(*ops)


def _ln_kernel(z_ref, g_ref, b_ref, o_ref, ob_ref):
    z = z_ref[...]
    mu = jnp.mean(z, axis=-1, keepdims=True)
    zc = z - mu
    var = jnp.mean(zc * zc, axis=-1, keepdims=True)
    out = (zc * lax.rsqrt(var + LN_EPS)) * g_ref[...] + b_ref[...]
    o_ref[...] = out
    ob_ref[...] = out.astype(BF16)


def _layer_norm(z, g, b, layer):
    t, d = z.shape
    tm = _pick(t, (512, 256, 128))
    row = pl.BlockSpec((tm, d), lambda i: (i, 0))
    par = pl.BlockSpec((None, 1, d), lambda i: (layer, 0, 0))
    return pl.pallas_call(
        _ln_kernel,
        grid=(t // tm,),
        in_specs=[row, par, par],
        out_specs=[row, row],
        out_shape=[jax.ShapeDtypeStruct((t, d), F32), jax.ShapeDtypeStruct((t, d), BF16)],
        compiler_params=_params(("parallel",)),
        name="layer_norm",
    )(z, g, b)


def _merge_kernel(ya_ref, wa_ref, yb_ref, wb_ref, ga_ref, gb_ref, o_ref):
    pa = jnp.dot(ya_ref[...], wa_ref[...].astype(BF16), preferred_element_type=F32)
    pb = jnp.dot(yb_ref[...], wb_ref[...].astype(BF16), preferred_element_type=F32)
    o_ref[...] = (ga_ref[...].astype(F32) * pa + gb_ref[...].astype(F32) * pb).astype(o_ref.dtype)


def _gated_merge(ya, yb, w_pa, w_pb, gates, layer):
    t = ya.shape[0]
    d = w_pa.shape[2]
    tm = _pick(t, (1024, 512, 256))
    tn = d if d <= 1280 else _pick(d, (1024, 512, 256, 128))
    ka, kb = ya.shape[1], yb.shape[1]
    return pl.pallas_call(
        _merge_kernel,
        grid=(t // tm, d // tn),
        in_specs=[
            pl.BlockSpec((tm, ka), lambda i, j: (i, 0)),
            pl.BlockSpec((None, ka, tn), lambda i, j: (layer, 0, j)),
            pl.BlockSpec((tm, kb), lambda i, j: (i, 0)),
            pl.BlockSpec((None, kb, tn), lambda i, j: (layer, 0, j)),
            pl.BlockSpec((tm, tn), lambda i, j: (i, j)),
            pl.BlockSpec((tm, tn), lambda i, j: (i, j + d // tn)),
        ],
        out_specs=pl.BlockSpec((tm, tn), lambda i, j: (i, j)),
        out_shape=jax.ShapeDtypeStruct((t, d), BF16),
        compiler_params=_params(("parallel", "parallel")),
        name="gated_merge",
    )(ya, w_pa, yb, w_pb, gates, gates)


def _dil_kernel(q_ref, kp_ref, kc_ref, vp_ref, vc_ref, o_ref, lse_ref, *, nsub):
    i = pl.program_id(1)
    qrows = min(nsub, 2) * BAND
    qq = lax.broadcasted_iota(jnp.int32, (qrows, BAND + qrows), 0)
    kk = lax.broadcasted_iota(jnp.int32, (qrows, BAND + qrows), 1)
    in_band = (kk >= qq) & (kk <= qq + BAND)
    bias = jnp.where(in_band, 0.0, NEG_INF)
    bias_first = jnp.where(in_band & (kk >= jnp.where(i > 0, 0, BAND)), 0.0, NEG_INF)
    scale = HEAD_DIM ** -0.5
    chains = [(g, h) for g in range(nsub * BAND // qrows) for h in range(A_HEADS_PER_GROUP)]
    scores, v_bands = [], []
    for g, h in chains:
        rows = slice(g * qrows, (g + 1) * qrows)
        band = slice(g * qrows - BAND, (g + 1) * qrows)
        sl = slice(h * HEAD_DIM, (h + 1) * HEAD_DIM)
        if g == 0:
            k_band = jnp.concatenate([kp_ref[:, sl], kc_ref[rows, sl]], axis=0)
            v_bands.append(jnp.concatenate([vp_ref[:, sl], vc_ref[rows, sl]], axis=0))
        else:
            k_band = kc_ref[band, sl]
            v_bands.append(vc_ref[band, sl])
        s = lax.dot_general(q_ref[rows, sl], k_band, NT_DIMS, preferred_element_type=F32) * scale
        scores.append(s + (bias_first if g == 0 else bias))
    maxes = [jnp.max(s, axis=-1, keepdims=True) for s in scores]
    exps = [jnp.exp(s - m) for s, m in zip(scores, maxes)]
    unnorm = [jnp.dot(e.astype(BF16), v, preferred_element_type=F32) for e, v in zip(exps, v_bands)]
    sums = [jnp.sum(e, axis=-1, keepdims=True) for e in exps]
    outs = [o * (1.0 / l) for o, l in zip(unnorm, sums)]
    lses = [jnp.broadcast_to(m + jnp.log(l), (qrows, HEAD_DIM)) for m, l in zip(maxes, sums)]
    nh = A_HEADS_PER_GROUP
    ngrp = len(outs) // nh
    o_ref[...] = jnp.concatenate(
        [jnp.concatenate(outs[g * nh:(g + 1) * nh], axis=1) for g in range(ngrp)], axis=0)
    lse_ref[...] = jnp.concatenate(
        [jnp.concatenate(lses[g * nh:(g + 1) * nh], axis=1) for g in range(ngrp)], axis=0)


def _dilated_group(qkv, dil):
    sub = qkv.shape[2]
    assert sub % BAND == 0
    nsub = _pick(sub // BAND, (4, 2, 1))
    rows = nsub * BAND

    def spec(which):
        return pl.BlockSpec((None, None, rows, A_OUT), lambda r, i: (which, r, i, 0))

    def prev_spec(which):
        return pl.BlockSpec((None, None, BAND, A_OUT),
                            lambda r, i: (which, r, jnp.maximum(i * nsub - 1, 0), 0))

    out_spec = pl.BlockSpec((None, rows, A_OUT), lambda r, i: (r, i, 0))
    return pl.pallas_call(
        functools.partial(_dil_kernel, nsub=nsub),
        grid=(dil, sub // rows),
        in_specs=[spec(0), prev_spec(1), spec(1), prev_spec(2), spec(2)],
        out_specs=[out_spec, out_spec],
        out_shape=[jax.ShapeDtypeStruct((dil, sub, A_OUT), F32)] * 2,
        compiler_params=_params(("parallel", "parallel")),
        name=f"dilated_attn_d{dil}",
    )(qkv, qkv, qkv, qkv, qkv)


def _mix_kernel(*refs, dils):
    n = len(dils)
    o_refs, l_refs, y_ref = refs[:n], refs[n:2 * n], refs[2 * n]
    stage = refs[2 * n + 1:]
    tm = y_ref.shape[0]

    def natural(ref, slot, dil, cs):
        if dil == 1:
            return ref[0, :, cs]
        for r in range(dil):
            slot[pl.ds(r, tm // dil, stride=dil), :] = ref[r, :, cs]
        return slot[...]

    for c in range(A_OUT // LANES):
        cs = slice(c * LANES, (c + 1) * LANES)
        outs = [natural(o_refs[g], stage[2 * g], dils[g], cs) for g in range(n)]
        lses = [natural(l_refs[g], stage[2 * g + 1], dils[g], cs) for g in range(n)]
        m = functools.reduce(jnp.maximum, lses)
        es = [jnp.exp(l - m) for l in lses]
        inv = 1.0 / functools.reduce(lambda a, b: a + b, es)
        y = functools.reduce(lambda a, b: a + b, [(e * inv) * o for e, o in zip(es, outs)])
        y_ref[:, cs] = y.astype(y_ref.dtype)


def _mix_groups(outs, lses, dils, t):
    tm = _pick(t, (512, 256))
    specs = [pl.BlockSpec((d, tm // d, A_OUT), lambda i: (0, i, 0)) for d in dils]
    return pl.pallas_call(
        functools.partial(_mix_kernel, dils=tuple(dils)),
        grid=(t // tm,),
        in_specs=specs + specs,
        out_specs=pl.BlockSpec((tm, A_OUT), lambda i: (i, 0)),
        out_shape=jax.ShapeDtypeStruct((t, A_OUT), BF16),
        scratch_shapes=[pltpu.VMEM((tm, LANES), F32)] * (2 * len(dils)),
        compiler_params=_params(("parallel",)),
        name="mix_groups",
    )(*outs, *lses)


def _sortable(x):
    b = pltpu.bitcast(x, jnp.int32)
    return b ^ ((b >> 31) & jnp.int32(INT_MAX))


def _dsa_kernel(step_q_ref, step_c_ref, qi_ref, ki_ref, wi_ref, q_ref, k_ref, v_ref, o_ref,
                key_ref, bias_ref, qs_ref, wb_ref, thr_ref, mthr_ref, m_ref, acc_ref,
                *, tq, tkc, topk):
    i = step_q_ref[pl.program_id(0)]
    c = step_c_ref[pl.program_id(0)]
    last_c = ((i + 1) * tq - 1) // tkc
    n_ch = last_c + 1
    row_pos = i * tq + lax.broadcasted_iota(jnp.int32, (tq, tkc), 0)
    col_iota = lax.broadcasted_iota(jnp.int32, (tq, tkc), 1)
    sub = MXU_COLS
    row_sub = i * tq + lax.broadcasted_iota(jnp.int32, (tq, sub), 0)
    col_sub = lax.broadcasted_iota(jnp.int32, (tq, sub), 1)

    @pl.when(c == 0)
    def _select():
        lo_half = lax.broadcasted_iota(jnp.int32, (tq, LANES), 1) < IDX_DIM
        for hp in range(IDX_HEADS // 2):
            pair = qi_ref[:, hp * LANES:(hp + 1) * LANES]
            qs_ref[(2 * hp) * tq:(2 * hp + 1) * tq, :] = jnp.where(lo_half, pair, 0.0).astype(BF16)
            qs_ref[(2 * hp + 1) * tq:(2 * hp + 2) * tq, :] = jnp.where(lo_half, 0.0, pair).astype(BF16)
        w = wi_ref[...] * (IDX_DIM ** -0.5 * IDX_HEADS ** -0.5)
        for h in range(IDX_HEADS):
            wb_ref[h] = w[:, h:h + 1] + jnp.zeros((tq, LANES), F32)

        def score_chunk(kc, carry):
            off = pl.multiple_of(kc * tkc, tkc)
            for s in range(tkc // sub):
                kis = ki_ref[pl.ds(off + s * sub, sub), :].astype(BF16)
                r = lax.dot_general(qs_ref[...], kis, NT_DIMS, preferred_element_type=F32)
                halves = []
                for half in range(sub // LANES):
                    cs = slice(half * LANES, (half + 1) * LANES)
                    sc = jnp.zeros((tq, LANES), F32)
                    for h in range(IDX_HEADS):
                        sc = sc + wb_ref[h] * jnp.maximum(r[h * tq:(h + 1) * tq, cs], 0.0)
                    halves.append(sc)
                sc = jnp.concatenate(halves, axis=1)
                cols = slice(s * sub, (s + 1) * sub)
                sc = jnp.where(off + s * sub + col_sub <= row_sub, sc, NEG_INF)
                key_ref[kc, :, cols] = _sortable(sc)
            return carry

        lax.fori_loop(0, n_ch, score_chunk, 0)

        def count(pred):
            def body(kc, acc):
                off = pl.multiple_of(kc * tkc, tkc)
                part = jnp.where(pred(key_ref[kc], off), 1, 0).astype(jnp.int32)
                for s in range(tkc // LANES):
                    acc = acc + part[:, s * LANES:(s + 1) * LANES]
                return acc
            acc = lax.fori_loop(0, n_ch, body, jnp.zeros((tq, LANES), jnp.int32))
            return jnp.sum(acc, axis=1, keepdims=True)

        def unsettled(state):
            it, lo, hi, n_lo = state
            open_rows = jnp.where(n_lo == topk, 0, jnp.where(lo == hi, 0, 1))
            return jnp.logical_and(it < 36, jnp.max(open_rows) > 0)

        def bisect(state):
            it, lo, hi, n_lo = state
            mid = (lo | hi) - ((lo ^ hi) >> 1)
            n_mid = count(lambda kk, off: kk >= mid)
            ok = n_mid >= topk
            return (it + 1, jnp.where(ok, mid, lo), jnp.where(ok, hi, mid - 1), jnp.where(ok, n_mid, n_lo))

        lo0 = jnp.full((tq, 1), INT_MIN, jnp.int32)
        hi0 = jnp.full((tq, 1), INT_MAX, jnp.int32)
        n0 = jnp.full((tq, 1), INT_MAX, jnp.int32)
        _, thr, _, n_ge = lax.while_loop(unsettled, lambda st: bisect(bisect(bisect(bisect(st)))),
                                         (jnp.int32(0), lo0, hi0, n0))
        thr_ref[...] = thr
        mthr_ref[...] = jnp.full((tq, 1), INT_MAX, jnp.int32)

        @pl.when(jnp.max(n_ge) > topk)
        def _ties():
            need = topk - count(lambda kk, off: kk > thr)

            def bisect_idx(_, lohi):
                lo, hi = lohi
                mid = (lo + hi) >> 1
                ok = count(lambda kk, off: (kk == thr) & (off + col_iota <= mid)) >= need
                return jnp.where(ok, lo, mid + 1), jnp.where(ok, mid, hi)

            lo1 = jnp.zeros((tq, 1), jnp.int32)
            hi1 = jnp.full((tq, 1), 2 ** 30, jnp.int32)
            _, m_idx = lax.fori_loop(0, 31, bisect_idx, (lo1, hi1))
            mthr_ref[...] = m_idx

        m_ref[...] = jnp.full(m_ref.shape, M_FLOOR, F32)
        acc_ref[...] = jnp.zeros(acc_ref.shape, F32)

    @pl.when(c <= last_c)
    def _attend():
        off = pl.multiple_of(c * tkc, tkc)
        kk = key_ref[c]
        thr = thr_ref[...]
        pos = off + col_iota
        tie = jnp.where(kk == thr, jnp.where(pos <= mthr_ref[...], 0.0, NEG_INF), NEG_INF)
        chosen = jnp.where(kk > thr, 0.0, tie)
        bias_ref[...] = jnp.where(pos <= row_pos, chosen, NEG_INF)
        ones = jnp.ones((tkc, HEAD_DIM), BF16)
        for h in range(B_HEADS):
            sl = slice(h * HEAD_DIM, (h + 1) * HEAD_DIM)
            s = lax.dot_general(q_ref[:, sl], k_ref[:, sl], NT_DIMS, preferred_element_type=F32) + bias_ref[...]
            m_old = m_ref[h]
            m_new = jnp.maximum(m_old, jnp.max(s, axis=-1, keepdims=True))
            p = jnp.exp2(s - m_new).astype(BF16)
            alpha = jnp.exp2(m_old - m_new)
            v_one = jnp.concatenate([v_ref[:, sl], ones], axis=1)
            acc_ref[h] = alpha * acc_ref[h] + jnp.dot(p, v_one, preferred_element_type=F32)
            m_ref[h] = m_new

    @pl.when(c == last_c)
    def _finish():
        for h in range(B_HEADS):
            a = acc_ref[h]
            o_ref[:, h * HEAD_DIM:(h + 1) * HEAD_DIM] = (a[:, :HEAD_DIM] / a[:, HEAD_DIM:]).astype(o_ref.dtype)


def _dsa(qi, kiwi, qk, v):
    t = qi.shape[0]
    tq = _pick(t, (256, 128))
    tkc = _pick(t, (1024, 512, 256))
    topk = min(DSA_TOPK, t // 4)
    assert tkc >= topk and t % tq == 0 and tkc % MXU_COLS == 0
    nq, nc = t // tq, t // tkc
    steps = [(i, c) for i in range(nq) for c in range(((i + 1) * tq - 1) // tkc + 1)]
    step_q = jnp.asarray([s[0] for s in steps], jnp.int32)
    step_c = jnp.asarray([s[1] for s in steps], jnp.int32)

    in_specs = [
        pl.BlockSpec((tq, IDX_WIDTH), lambda s, sq, sc: (sq[s], 0)),
        pl.BlockSpec((t, LANES), lambda s, sq, sc: (0, 0)),
        pl.BlockSpec((tq, LANES), lambda s, sq, sc: (sq[s], 1)),
        pl.BlockSpec((tq, B_WIDTH), lambda s, sq, sc: (sq[s], 0)),
        pl.BlockSpec((tkc, B_WIDTH), lambda s, sq, sc: (sc[s], 1)),
        pl.BlockSpec((tkc, B_WIDTH), lambda s, sq, sc: (sc[s], 0)),
    ]
    grid_spec = pltpu.PrefetchScalarGridSpec(
        num_scalar_prefetch=2,
        grid=(len(steps),),
        in_specs=in_specs,
        out_specs=pl.BlockSpec((tq, B_WIDTH), lambda s, sq, sc: (sq[s], 0)),
        scratch_shapes=[
            pltpu.VMEM((nc, tq, tkc), jnp.int32),
            pltpu.VMEM((tq, tkc), F32),
            pltpu.VMEM((IDX_HEADS * tq, LANES), BF16),
            pltpu.VMEM((IDX_HEADS, tq, LANES), F32),
            pltpu.VMEM((tq, 1), jnp.int32),
            pltpu.VMEM((tq, 1), jnp.int32),
            pltpu.VMEM((B_HEADS, tq, 1), F32),
            pltpu.VMEM((B_HEADS, tq, 2 * HEAD_DIM), F32),
        ],
    )
    return pl.pallas_call(
        functools.partial(_dsa_kernel, tq=tq, tkc=tkc, topk=topk),
        grid_spec=grid_spec,
        out_shape=jax.ShapeDtypeStruct((t, B_WIDTH), BF16),
        compiler_params=_params(("arbitrary",)),
        name="dsa",
    )(step_q, step_c, qi, kiwi, kiwi, qk, qk, v)


def _mem_layer_kernel(xb_ref, x_ref, wq_ref, kv_ref, wo_ref, g_ref, b_ref, o_ref, ob_ref, *, alpha):
    scale = HEAD_DIM ** -0.5
    q = jnp.dot(xb_ref[...], wq_ref[...], preferred_element_type=F32).astype(BF16)
    ksl = [slice(h * HEAD_DIM, (h + 1) * HEAD_DIM) for h in range(MEM_HEADS)]
    vsl = [slice(MEM_WIDTH + h * HEAD_DIM, MEM_WIDTH + (h + 1) * HEAD_DIM) for h in range(MEM_HEADS)]
    scores = [lax.dot_general(q[:, sl], kv_ref[:, sl], NT_DIMS, preferred_element_type=F32) * scale
              for sl in ksl]
    maxes = [jnp.max(s, axis=-1, keepdims=True) for s in scores]
    exps = [jnp.exp(s - m) for s, m in zip(scores, maxes)]
    unnorm = [jnp.dot(e.astype(BF16), kv_ref[:, sl], preferred_element_type=F32) for e, sl in zip(exps, vsl)]
    sums = [jnp.sum(e, axis=-1, keepdims=True) for e in exps]
    om = jnp.concatenate([o * (1.0 / l) for o, l in zip(unnorm, sums)], axis=1).astype(BF16)
    z = alpha * x_ref[...] + jnp.dot(om, wo_ref[...], preferred_element_type=F32)
    mu = jnp.mean(z, axis=-1, keepdims=True)
    zc = z - mu
    var = jnp.mean(zc * zc, axis=-1, keepdims=True)
    out = (zc * lax.rsqrt(var + LN_EPS)) * g_ref[...] + b_ref[...]
    o_ref[...] = out
    ob_ref[...] = out.astype(BF16)


def _mem_layer(xb, x, wq, kv, wo, g, b, layer, alpha):
    t, d = x.shape
    tm = _pick(t, (256, 128))
    row = lambda w: pl.BlockSpec((tm, w), lambda i: (i, 0))
    whole = lambda a: pl.BlockSpec(a.shape, lambda i: (0, 0))
    par = pl.BlockSpec((None, 1, d), lambda i: (layer, 0, 0))
    return pl.pallas_call(
        functools.partial(_mem_layer_kernel, alpha=alpha),
        grid=(t // tm,),
        in_specs=[row(d), row(d), whole(wq), whole(kv), whole(wo), par, par],
        out_specs=[row(d), row(d)],
        out_shape=[jax.ShapeDtypeStruct((t, d), F32), jax.ShapeDtypeStruct((t, d), BF16)],
        compiler_params=_params(("parallel",)),
        name="mem_layer",
    )(xb, x, wq, kv, wo, g, b)


def _cast_layer_kernel(w_ref, o_ref):
    o_ref[...] = w_ref[...].astype(o_ref.dtype)


def _layer_to_bf16(w, layer):
    _, r, c = w.shape
    tr = _pick(r, (512, 256, 128))
    return pl.pallas_call(
        _cast_layer_kernel,
        grid=(r // tr,),
        in_specs=[pl.BlockSpec((None, tr, c), lambda i: (layer, i, 0))],
        out_specs=pl.BlockSpec((tr, c), lambda i: (i, 0)),
        out_shape=jax.ShapeDtypeStruct((r, c), BF16),
        compiler_params=_params(("parallel",)),
        name="cast_layer",
    )(w)


def _cast_kernel(x_ref, o_ref):
    o_ref[...] = x_ref[...].astype(o_ref.dtype)


def _to_bf16(x):
    t, d = x.shape
    tm = _pick(t, (256, 128))
    spec = pl.BlockSpec((tm, d), lambda i: (i, 0))
    return pl.pallas_call(
        _cast_kernel, grid=(t // tm,), in_specs=[spec], out_specs=spec,
        out_shape=jax.ShapeDtypeStruct((t, d), BF16),
        compiler_params=_params(("parallel",)), name="cast_bf16",
    )(x)


def _rope_tables(t):
    pos = jnp.arange(t, dtype=jnp.int32).astype(F32)[:, None]
    half = HEAD_DIM // 2
    inv = ROPE_THETA ** (-jnp.arange(half, dtype=F32) / half)
    ang = pos * inv[None, :]
    cos128 = jnp.concatenate([jnp.cos(ang)] * 2, axis=1)
    sin128 = jnp.concatenate([-jnp.sin(ang), jnp.sin(ang)], axis=1)
    half = IDX_DIM // 2
    inv = ROPE_THETA ** (-jnp.arange(half, dtype=F32) / half)
    ang = pos * inv[None, :]
    z = jnp.zeros_like(ang)
    cos64 = jnp.concatenate([jnp.cos(ang)] * 4, axis=1)
    sin_lo = jnp.concatenate([-jnp.sin(ang), z] * 2, axis=1)
    sin_hi = jnp.concatenate([z, jnp.sin(ang)] * 2, axis=1)
    return (cos128, sin128), (cos64, sin_lo, sin_hi)


def _mixer_branches(xb, w_in_t, l, blk, rope128, rope64):
    t = xb.shape[0]
    proj = functools.partial(_panel_matmul, xb, w_in_t, l, trans_b=True)
    outs, lses, dils = [], [], []
    for g, (win, dil) in enumerate(DIL_GROUPS):
        assert win // dil == BAND
        qkv = proj(kind="a_group", extras=rope128, tn=COL_BLOCK, n_tiles=3, dil=dil,
                   src_block=lambda j, g=g: blk[0] + g + j * (A_WIDTH // COL_BLOCK), name=f"proj_a_g{g}")
        o, lse = _dilated_group(qkv, dil)
        outs.append(o)
        lses.append(lse)
        dils.append(dil)
    ya = _mix_groups(outs, lses, dils, t)
    qk_b = proj(kind="rope128_qk", extras=rope128, tn=COL_BLOCK, n_tiles=2 * B_WIDTH // COL_BLOCK,
                src_block=lambda j: blk[3] + j, name="proj_b_qk")
    v_b = proj(tn=COL_BLOCK, n_tiles=B_WIDTH // COL_BLOCK, src_block=lambda j: blk[5] + j, name="proj_b_v")
    qi = proj(kind="rope64", extras=rope64, out_dtype=F32, tn=COL_BLOCK, n_tiles=IDX_WIDTH // COL_BLOCK,
              src_block=lambda j: blk[6] + j, name="proj_qi")
    kiwi = proj(kind="kiwi", extras=rope64, out_dtype=F32, tn=LANES, n_tiles=1,
                src_block=lambda j: blk[7] * (COL_BLOCK // LANES), name="proj_kiwi")
    yb = _dsa(qi, kiwi, qk_b, v_b)
    return ya, yb


def kernel(x, mem, w_in, w_pa, w_pb, w_o, ln1_g, ln1_b, w_mq, w_mkv, w_mo, ln2_g, ln2_b, w_up, w_down, ln3_g, ln3_b):
    bsz, t, d = x.shape
    depth = w_in.shape[0]
    assert bsz == 1
    alpha = (2 * depth) ** 0.25

    w_in_t = jnp.swapaxes(w_in, 1, 2)
    sizes = (A_WIDTH, A_WIDTH, A_WIDTH, B_WIDTH, B_WIDTH, B_WIDTH, IDX_WIDTH, IDX_DIM, IDX_HEADS, d, d)
    offs = [0]
    for s in sizes:
        offs.append(offs[-1] + s)
    assert offs[-1] == w_in.shape[2] and all(o % COL_BLOCK == 0 for o in offs[:8])
    blk = [o // COL_BLOCK for o in offs[:8]]
    gate_shift = offs[9] - offs[7]
    assert 0 < gate_shift < LANES and (2 * d) % COL_BLOCK == 0
    lnp = [p.reshape(depth, 1, d) for p in (ln1_g, ln1_b, ln2_g, ln2_b, ln3_g, ln3_b)]

    (cos128, sin128), (cos64, sin_lo, sin_hi) = _rope_tables(t)
    tm_rows = _pick(t, (1024, 512, 256))

    xf = x[0]
    xb = _to_bf16(xf)
    memb = _to_bf16(mem[0])

    def row_tab(arr):
        return (arr, (tm_rows, LANES), lambda i, j: (i, 0))

    def tile(arr, tn, joff=0):
        return (arr, (tm_rows, tn), lambda i, j: (i, j + joff))

    rope128 = (row_tab(cos128), row_tab(sin128))
    rope64 = (row_tab(cos64), row_tab(sin_lo), row_tab(sin_hi))
    tn_d = d if d <= 1280 else _pick(d, (1024, 512, 256, 128))
    tn_p = _pick(d, (COL_BLOCK, 256, 128))

    for l in range(depth):
        ya, yb = _mixer_branches(xb, w_in_t, l, blk, rope128, rope64)
        gates = _panel_matmul(xb, w_in_t, l, kind="sigmoid", tn=COL_BLOCK, trans_b=True,
                              n_tiles=2 * d // COL_BLOCK, src_block=lambda j: blk[7] + j, shift=gate_shift,
                              name="proj_gate")
        merged = _gated_merge(ya, yb, w_pa, w_pb, gates, l)
        z = _panel_matmul(merged, w_o, l, kind="resid", extras=(tile(xf, tn_p),), alpha=alpha, out_dtype=F32,
                          tn=tn_p, n_tiles=d // tn_p, src_block=lambda j: j, name="mixer_out")
        xf, xb = _layer_norm(z, lnp[0], lnp[1], l)
        kvm = _matmul(memb, w_mkv, l, name="mem_kv")
        xf, xb = _mem_layer(xb, xf, _layer_to_bf16(w_mq, l), kvm, _layer_to_bf16(w_mo, l),
                            lnp[2], lnp[3], l, alpha)
        hdn = _panel_matmul(xb, w_up, l, kind="relu2", tn=tn_p, n_tiles=w_up.shape[2] // tn_p,
                            src_block=lambda j: j, name="mlp_up")
        z = _matmul(hdn, w_down, l, kind="resid", extras=(tile(xf, tn_d),), alpha=alpha, out_dtype=F32,
                    name="mlp_down")
        xf, xb = _layer_norm(z, lnp[4], lnp[5], l)
    return xf[None]
```
